```python
import numpy as np
import jax
import jax.numpy as jnp
from jax import lax

D_MODEL = 2048
BATCH = 4
SEQ = 4096
DEPTH = 2

NORM_EPS = 1e-6
BLOCK_Q = 128

SB_HEADS = 16
SB_HEAD_DIM = 64
SB_WIDTH = SB_HEADS * SB_HEAD_DIM

RWKV_HEADS = 16
RWKV_HEAD_DIM = 64
RWKV_WIDTH = RWKV_HEADS * RWKV_HEAD_DIM
RWKV_DECAY_LORA = 96
RWKV_ICLR_LORA = 96
RWKV_VRES_LORA = 64
RWKV_GATE_LORA = 256
RWKV_GN_EPS = 64e-5
RWKV_COLS = 3 * RWKV_WIDTH + RWKV_DECAY_LORA + RWKV_ICLR_LORA + RWKV_GATE_LORA

MLA_HEADS = 16
MLA_Q_LORA = 768
MLA_KV_LORA = 256
MLA_NOPE_DIM = 64
MLA_ROPE_DIM = 32
MLA_V_DIM = 64
MLA_WIDTH = MLA_HEADS * MLA_V_DIM
MLA_COLS = MLA_Q_LORA + MLA_KV_LORA + MLA_ROPE_DIM
ROPE_THETA = 10000.0

N_BRANCHES = 3
N_IN = 3 * SB_WIDTH + RWKV_COLS + MLA_COLS + N_BRANCHES * D_MODEL

N_GROUPS = 8
EXPERTS_PER_GROUP = 8
N_EXPERTS = N_GROUPS * EXPERTS_PER_GROUP
TOP_K_IN_GROUP = 2
EXPERT_FF = 512
MOE_BLOCK = 128

kernel_name = 'hybrid_stickbreak_rwkv7_mla_hiermoe'


def _split(t, sizes):
    return jnp.split(t, np.cumsum(sizes)[:-1].tolist(), axis=-1)


def _rms_norm(x, g, eps=NORM_EPS):
    xf = x.astype(jnp.float32)
    y = xf * lax.rsqrt(jnp.mean(xf * xf, axis=-1, keepdims=True) + eps)
    return (y * g.astype(jnp.float32)).astype(x.dtype)


def _rope_tables(positions):
    inv_freq = ROPE_THETA ** (-jnp.arange(0, MLA_ROPE_DIM, 2, dtype=jnp.float32) / MLA_ROPE_DIM)
    ang = positions.astype(jnp.float32)[..., None] * inv_freq
    return jnp.cos(ang), jnp.sin(ang)


def _rope(x, cos, sin):
    xf = x.astype(jnp.float32)
    x1, x2 = jnp.split(xf, 2, axis=-1)
    return jnp.concatenate([x1 * cos - x2 * sin, x2 * cos + x1 * sin], axis=-1).astype(x.dtype)


def _sweep_query_blocks(block_fn, q):
    B, S, H, d = q.shape
    nb = S // BLOCK_Q
    qb = q.reshape(B, nb, BLOCK_Q, H, d).transpose(1, 0, 3, 2, 4)
    o = lax.map(block_fn, (qb, jnp.arange(nb)))
    return o.transpose(1, 0, 3, 2, 4).reshape(B, S, H * o.shape[-1])


def _stick_breaking_attention(q, k, v):
    S, d = q.shape[1], q.shape[-1]
    scale = d ** -0.5
    kh = k.transpose(0, 2, 1, 3)
    vh = v.transpose(0, 2, 1, 3)
    key_pos = jnp.arange(S)

    def block(args):
        qb, i = args
        q_pos = i * BLOCK_Q + jnp.arange(BLOCK_Q)
        before = key_pos[None, :] < q_pos[:, None]
        z = jnp.einsum('bhqd,bhkd->bhqk', qb, kh).astype(jnp.float32) * scale
        log_keep = jnp.where(before, jax.nn.log_sigmoid(-z), 0.0)
        log_after = lax.cumsum(log_keep, axis=3, reverse=True) - log_keep
        w = jnp.where(before, jnp.exp(jax.nn.log_sigmoid(z) + log_after), 0.0)
        return jnp.einsum('bhqk,bhkd->bhqd', w.astype(vh.dtype), vh)

    return _sweep_query_blocks(block, q)


def _causal_softmax_attention(q, k, v, scale):
    S = q.shape[1]
    kh = k.transpose(0, 2, 1, 3)
    vh = v.transpose(0, 2, 1, 3)
    key_pos = jnp.arange(S)

    def block(args):
        qb, i = args
        q_pos = i * BLOCK_Q + jnp.arange(BLOCK_Q)
        s = jnp.einsum('bhqd,bhkd->bhqk', qb, kh).astype(jnp.float32) * scale
        s = jnp.where(key_pos[None, :] <= q_pos[:, None], s, -jnp.inf)
        p = jax.nn.softmax(s, axis=-1)
        return jnp.einsum('bhqk,bhkd->bhqd', p.astype(vh.dtype), vh)

    return _sweep_query_blocks(block, q)


def _rwkv7_scan(r, w, k, v, a, b):
    B, S, H, N = r.shape

    def step(state, inp):
        r_t, w_t, k_t, v_t, a_t, b_t = inp
        sa = jnp.einsum('bhvk,bhk->bhv', state, a_t)
        state = (state * w_t[:, :, None, :] + sa[..., None] * b_t[:, :, None, :]
                 + v_t[..., None] * k_t[:, :, None, :])
        return state, jnp.einsum('bhvk,bhk->bhv', state, r_t)

    xs = tuple(t.transpose(1, 0, 2, 3) for t in (r, w, k, v, a, b))
    _, y = lax.scan(step, jnp.zeros((B, H, N, N), jnp.float32), xs)
    return y.transpose(1, 0, 2, 3)


def _rwkv7_branch(cols, mu, w0, w2, a0, a2, g2, k_k, k_a, r_k, ln_g, ln_b, v_first, v_res):
    B, S, _ = cols.shape
    f32 = jnp.float32
    heads = lambda t: t.reshape(B, S, RWKV_HEADS, RWKV_HEAD_DIM)
    prev = jnp.concatenate([jnp.zeros_like(cols[:, :1]), cols[:, :-1]], axis=1)
    cols = cols + (prev - cols) * mu
    r, k, v, wd, ad, gd = _split(cols, [RWKV_WIDTH] * 3 + [RWKV_DECAY_LORA, RWKV_ICLR_LORA, RWKV_GATE_LORA])
    w = -jax.nn.softplus(-(w0 + jnp.tanh(wd) @ w2).astype(f32)) - 0.5
    decay = jnp.exp(-jnp.exp(w))
    a = jax.nn.sigmoid((a0 + ad @ a2).astype(f32))
    g = jax.nn.sigmoid(gd) @ g2
    if v_res is None:
        v_first = v
    else:
        v0, v1, v2 = v_res
        v = v + (v_first - v) * jax.nn.sigmoid(v0 + (v @ v1) @ v2)
    kk = heads((k * k_k).astype(f32))
    kk = kk / jnp.maximum(jnp.sqrt(jnp.sum(kk * kk, axis=-1, keepdims=True)), 1e-12)
    k = k.astype(f32) * (1.0 + (a - 1.0) * k_a)
    rh, kh, vh, ah = heads(r.astype(f32)), heads(k), heads(v.astype(f32)), heads(a)
    y = _rwkv7_scan(rh, heads(decay), kh, vh, -kk, kk * ah)
    mean = jnp.mean(y, axis=-1, keepdims=True)
    var = jnp.mean(jnp.square(y - mean), axis=-1, keepdims=True)
    y = ((y - mean) * lax.rsqrt(var + RWKV_GN_EPS)).reshape(B, S, RWKV_WIDTH) * ln_g + ln_b
    bonus = jnp.sum(rh * kh * r_k, axis=-1, keepdims=True) * vh
    y = (y + bonus.reshape(B, S, RWKV_WIDTH)) * g
    return y.astype(cols.dtype), v_first


def _mla_branch(cols, cos, sin, q_norm_g, w_uq, kv_norm_g, w_ukv):
    B, S, _ = cols.shape
    c_q, c_kv, k_pe = _split(cols, [MLA_Q_LORA, MLA_KV_LORA, MLA_ROPE_DIM])
    q = (_rms_norm(c_q, q_norm_g) @ w_uq).reshape(B, S, MLA_HEADS, MLA_NOPE_DIM + MLA_ROPE_DIM)
    kv = (_rms_norm(c_kv, kv_norm_g) @ w_ukv).reshape(B, S, MLA_HEADS, MLA_NOPE_DIM + MLA_V_DIM)
    q_nope, q_pe = q[..., :MLA_NOPE_DIM], q[..., MLA_NOPE_DIM:]
    k_nope, v = kv[..., :MLA_NOPE_DIM], kv[..., MLA_NOPE_DIM:]
    q_pe = _rope(q_pe, cos[:, :, None], sin[:, :, None])
    k_pe = _rope(k_pe, cos, sin)
    q = jnp.concatenate([q_nope, q_pe], axis=-1)
    k = jnp.concatenate([k_nope, jnp.broadcast_to(k_pe[:, :, None], (B, S, MLA_HEADS, MLA_ROPE_DIM))], axis=-1)
    return _causal_softmax_attention(q, k, v, (MLA_NOPE_DIM + MLA_ROPE_DIM) ** -0.5)


def _hier_moe(h, rg_w, rg_b, re_w, re_b, w_gate, w_up, w_down):
    B, S, D = h.shape
    T = B * S
    A = T * TOP_K_IN_GROUP
    f32 = jnp.float32
    ht = h.reshape(T, D)
    tok = jnp.arange(T)
    g_logits = (ht @ rg_w).astype(f32) + rg_b.astype(f32)
    grp = jnp.argmax(g_logits, axis=-1)
    p_grp = jax.nn.softmax(g_logits, axis=-1)[tok, grp]
    e_logits = ((ht @ re_w).astype(f32) + re_b.astype(f32)).reshape(T, N_GROUPS, EXPERTS_PER_GROUP)
    p_in, idx_in = lax.top_k(jax.nn.softmax(e_logits[tok, grp], axis=-1), TOP_K_IN_GROUP)
    gate = p_grp[:, None] * p_in / jnp.sum(p_in, axis=-1, keepdims=True)
    expert = grp[:, None] * EXPERTS_PER_GROUP + idx_in
    flat_e = expert.reshape(A)
    order = jnp.argsort(flat_e)
    se = flat_e[order]
    st = jnp.repeat(tok, TOP_K_IN_GROUP)[order].astype(jnp.int32)
    sw = gate.reshape(A)[order]
    counts = jnp.bincount(flat_e, length=N_EXPERTS)
    padded = (counts + MOE_BLOCK - 1) // MOE_BLOCK * MOE_BLOCK
    start = jnp.cumsum(counts) - counts
    pend = jnp.cumsum(padded)
    pstart = pend - padded
    dest = pstart[se] + jnp.arange(A) - start[se]
    nb = -(-A // MOE_BLOCK) + N_EXPERTS
    P = nb * MOE_BLOCK
    row_tok = jnp.full((P,), T, jnp.int32).at[dest].set(st)
    row_w = jnp.zeros((P,), f32).at[dest].set(sw)
    block_e = jnp.minimum(jnp.searchsorted(pend, jnp.arange(nb) * MOE_BLOCK, side='right'), N_EXPERTS - 1)
    xb = jnp.concatenate([ht, jnp.zeros((1, D), ht.dtype)], axis=0)[row_tok].reshape(nb, MOE_BLOCK, D)

    def block(args):
        xe, e = args
        return (jax.nn.silu(xe @ w_gate[e]) * (xe @ w_up[e])) @ w_down[e]

    yb = lax.map(block, (xb, block_e)).reshape(P, D)
    out = jnp.zeros((T + 1, D), f32).at[row_tok].add(yb.astype(f32) * row_w[:, None])
    return out[:T].reshape(B, S, D).astype(h.dtype)


def setup_inputs(seed: int = 0) -> dict:
    key = jax.random.key(seed)
    keys = iter(jax.random.split(key, 40))
    f32 = jnp.float32
    L = DEPTH

    def nrm(shape, scale):
        return jax.random.normal(next(keys), shape, f32) * scale

    def unif(shape, lo, hi):
        return jax.random.uniform(next(keys), shape, f32, lo, hi)

    def gain(shape):
        return 1.0 + nrm(shape, 0.05)

    x = nrm((BATCH, SEQ, D_MODEL), 1.0)
    positions = (jax.random.randint(next(keys), (BATCH, 1), 0, 1024, dtype=jnp.int32)
                 + jnp.arange(SEQ, dtype=jnp.int32)[None, :])
    return {
        'x': x,
        'positions': positions,
        'attn_norm_g': gain((L, D_MODEL)),
        'w_in': nrm((L, D_MODEL, N_IN), D_MODEL ** -0.5),
        'rwkv_mu': unif((L, RWKV_COLS), 0.0, 1.0),
        'rwkv_w0': unif((L, RWKV_WIDTH), -5.0, 0.5),
        'rwkv_w2': nrm((L, RWKV_DECAY_LORA, RWKV_WIDTH), 0.1 * RWKV_DECAY_LORA ** -0.5),
        'rwkv_a0': nrm((L, RWKV_WIDTH), 0.5),
        'rwkv_a2': nrm((L, RWKV_ICLR_LORA, RWKV_WIDTH), 0.5 * RWKV_ICLR_LORA ** -0.5),
        'rwkv_g2': nrm((L, RWKV_GATE_LORA, RWKV_WIDTH), RWKV_GATE_LORA ** -0.5),
        'rwkv_k_k': 0.85 + nrm((L, RWKV_WIDTH), 0.05),
        'rwkv_k_a': gain((L, RWKV_WIDTH)),
        'rwkv_r_k': nrm((L, RWKV_HEADS, RWKV_HEAD_DIM), 0.1),
        'rwkv_ln_g': gain((L, RWKV_WIDTH)),
        'rwkv_ln_b': nrm((L, RWKV_WIDTH), 0.02),
        'rwkv_v0': 1.0 + nrm((L - 1, RWKV_WIDTH), 0.1),
        'rwkv_v1': nrm((L - 1, RWKV_WIDTH, RWKV_VRES_LORA), RWKV_WIDTH ** -0.5),
        'rwkv_v2': nrm((L - 1, RWKV_VRES_LORA, RWKV_WIDTH), 0.5 * RWKV_VRES_LORA ** -0.5),
        'mla_q_norm_g': gain((L, MLA_Q_LORA)),
        'mla_w_uq': nrm((L, MLA_Q_LORA, MLA_HEADS * (MLA_NOPE_DIM + MLA_ROPE_DIM)), MLA_Q_LORA ** -0.5),
        'mla_kv_norm_g': gain((L, MLA_KV_LORA)),
        'mla_w_ukv': nrm((L, MLA_KV_LORA, MLA_HEADS * (MLA_NOPE_DIM + MLA_V_DIM)), MLA_KV_LORA ** -0.5),
        'w_br_sb': nrm((L, SB_WIDTH, D_MODEL), SB_WIDTH ** -0.5),
        'w_br_rwkv': nrm((L, RWKV_WIDTH, D_MODEL), RWKV_WIDTH ** -0.5),
        'w_br_mla': nrm((L, MLA_WIDTH, D_MODEL), MLA_WIDTH ** -0.5),
        'w_out': nrm((L, D_MODEL, D_MODEL), D_MODEL ** -0.5),
        'ffn_norm_g': gain((L, D_MODEL)),
        'router_group_w': nrm((L, D_MODEL, N_GROUPS), D_MODEL ** -0.5),
        'router_group_b': nrm((L, N_GROUPS), 0.01),
        'router_expert_w': nrm((L, D_MODEL, N_EXPERTS), D_MODEL ** -0.5),
        'router_expert_b': nrm((L, N_EXPERTS), 0.01),
        'expert_w_gate': nrm((L, N_EXPERTS, D_MODEL, EXPERT_FF), D_MODEL ** -0.5),
        'expert_w_up': nrm((L, N_EXPERTS, D_MODEL, EXPERT_FF), D_MODEL ** -0.5),
        'expert_w_down': nrm((L, N_EXPERTS, EXPERT_FF, D_MODEL), EXPERT_FF ** -0.5),
        'final_norm_g': gain((D_MODEL,)),
    }


def reference(x, positions, attn_norm_g, w_in, rwkv_mu, rwkv_w0, rwkv_w2, rwkv_a0, rwkv_a2,
              rwkv_g2, rwkv_k_k, rwkv_k_a, rwkv_r_k, rwkv_ln_g, rwkv_ln_b, rwkv_v0, rwkv_v1,
              rwkv_v2, mla_q_norm_g, mla_w_uq, mla_kv_norm_g, mla_w_ukv, w_br_sb, w_br_rwkv,
              w_br_mla, w_out, ffn_norm_g, router_group_w, router_group_b, router_expert_w,
              router_expert_b, expert_w_gate, expert_w_up, expert_w_down, final_norm_g):
    B, S, _ = x.shape
    cos, sin = _rope_tables(positions)
    sb_heads = lambda t: t.reshape(B, S, SB_HEADS, SB_HEAD_DIM)
    v_first = None
    for l in range(DEPTH):
        h = _rms_norm(x, attn_norm_g[l])
        proj = h @ w_in[l]
        sb_q, sb_k, sb_v, rwkv_cols, mla_cols, gate_cols = _split(
            proj, [SB_WIDTH] * 3 + [RWKV_COLS, MLA_COLS, N_BRANCHES * D_MODEL])
        o_sb = _stick_breaking_attention(sb_heads(sb_q), sb_heads(sb_k), sb_heads(sb_v))
        v_res = None if l == 0 else (rwkv_v0[l - 1], rwkv_v1[l - 1], rwkv_v2[l - 1])
        o_rwkv, v_first = _rwkv7_branch(rwkv_cols, rwkv_mu[l], rwkv_w0[l], rwkv_w2[l], rwkv_a0[l],
                                        rwkv_a2[l], rwkv_g2[l], rwkv_k_k[l], rwkv_k_a[l], rwkv_r_k[l],
                                        rwkv_ln_g[l], rwkv_ln_b[l], v_first, v_res)
        o_mla = _mla_branch(mla_cols, cos, sin, mla_q_norm_g[l], mla_w_uq[l], mla_kv_norm_g[l], mla_w_ukv[l])
        g_sb, g_rwkv, g_mla = jnp.split(jax.nn.sigmoid(gate_cols), N_BRANCHES, axis=-1)
        merged = (g_sb * (o_sb @ w_br_sb[l]) + g_rwkv * (o_rwkv @ w_br_rwkv[l])
                  + g_mla * (o_mla @ w_br_mla[l]))
        x = x + merged @ w_out[l]
        h = _rms_norm(x, ffn_norm_g[l])
        x = x + _hier_moe(h, router_group_w[l], router_group_b[l], router_expert_w[l], router_expert_b[l],
                          expert_w_gate[l], expert_w_up[l], expert_w_down[l])
    return _rms_norm(x, final_norm_g)
```

```python
import functools

import numpy as np
import jax
import jax.numpy as jnp
from jax import lax
from jax.experimental import pallas as pl
from jax.experimental.pallas import tpu as pltpu

F32 = jnp.float32
BF16 = jnp.bfloat16
HIGHEST = lax.Precision.HIGHEST

NORM_EPS = 1e-6
HEADS = 16
HEAD_DIM = 64
WIDTH = HEADS * HEAD_DIM
LANES = 128
RWKV_DECAY_LORA = 96
RWKV_ICLR_LORA = 96
RWKV_GATE_LORA = 256
RWKV_LORA_PAD = 128
RWKV_GN_EPS = 64e-5
RWKV_CHUNK = 64
MLA_Q_LORA = 768
MLA_KV_LORA = 256
MLA_ROPE_DIM = 32
ROPE_THETA = 10000.0
N_GROUPS = 8
EXPERTS_PER_GROUP = 8
N_EXPERTS = N_GROUPS * EXPERTS_PER_GROUP
TOP_K = 2
MOE_ROWS = 256
VMEM_LIMIT = 48 * 1024 * 1024


def _params(*sem):
    return pltpu.CompilerParams(dimension_semantics=sem, vmem_limit_bytes=VMEM_LIMIT)


def _pick(n, cands):
    for c in cands:
        if n % c == 0:
            return c
    raise ValueError(f"no tile for {n} in {cands}")


def _rmsnorm_body(x_ref, g_ref, o_ref):
    x = x_ref[...].astype(F32)
    y = x * lax.rsqrt(jnp.mean(x * x, axis=-1, keepdims=True) + NORM_EPS)
    o_ref[...] = (y * g_ref[...]).astype(o_ref.dtype)


def rmsnorm(x, g, out_dtype, *, width=None, col_block=0, name="rmsnorm"):
    m = x.shape[0]
    width = x.shape[1] if width is None else width
    tm = _pick(m, (512, 256, 128, 64, 32, 16, 8))
    return pl.pallas_call(
        _rmsnorm_body,
        grid=(m // tm,),
        in_specs=[pl.BlockSpec((tm, width), lambda i: (i, col_block)),
                  pl.BlockSpec((1, width), lambda i: (0, 0))],
        out_specs=pl.BlockSpec((tm, width), lambda i: (i, 0)),
        out_shape=jax.ShapeDtypeStruct((m, width), out_dtype),
        compiler_params=_params("parallel"),
        name=name,
    )(x, g.reshape(1, width).astype(F32))


def _mm_body(*refs, n_extra, epilogue):
    a_ref, w_ref = refs[:2]
    extra = [r[...] for r in refs[2:2 + n_extra]]
    outs = refs[2 + n_extra:]
    acc = jnp.dot(a_ref[...], w_ref[...], preferred_element_type=F32)
    res = epilogue(acc, *extra)
    for o, r in zip(outs, res):
        o[...] = r.astype(o.dtype)


def matmul(a, w, out_dtypes, *, epilogue=None, extras=(), a_col_block=0, name="matmul"):
    m = a.shape[0]
    k, n = w.shape
    tm = _pick(m, (512, 256, 128, 64, 32, 16, 8))
    tn = n if n <= 1280 else _pick(n, (512, 256, 128))
    if epilogue is None:
        epilogue = lambda acc: (acc,)
    in_specs = [pl.BlockSpec((tm, k), lambda i, j: (i, a_col_block)),
                pl.BlockSpec((k, tn), lambda i, j: (0, j))]
    args = [a, w]
    for arr, kind, off in extras:
        if kind == "row":
            in_specs.append(pl.BlockSpec((tm, arr.shape[1]), lambda i, j: (i, 0)))
        elif kind == "col":
            in_specs.append(pl.BlockSpec((1, tn), lambda i, j: (0, j)))
        else:
            in_specs.append(pl.BlockSpec((tm, tn), lambda i, j, off=off: (i, j + off)))
        args.append(arr)
    return pl.pallas_call(
        functools.partial(_mm_body, n_extra=len(extras), epilogue=epilogue),
        grid=(m // tm, n // tn),
        in_specs=in_specs,
        out_specs=[pl.BlockSpec((tm, tn), lambda i, j: (i, j)) for _ in out_dtypes],
        out_shape=[jax.ShapeDtypeStruct((m, n), dt) for dt in out_dtypes],
        compiler_params=_params("parallel", "parallel"),
        name=name,
    )(*args)


def _rope_slots(x, cos_t, sin_t):
    n = x.shape[-1]
    reps = n // LANES
    lane = lax.broadcasted_iota(jnp.int32, x.shape, 1) % LANES
    from_hi = pltpu.roll(x, n - MLA_ROPE_DIM // 2, 1)
    from_lo = pltpu.roll(x, MLA_ROPE_DIM // 2, 1)
    swapped = jnp.where(lane < HEAD_DIM + MLA_ROPE_DIM // 2, from_hi, from_lo)
    if reps > 1:
        cos_t = jnp.concatenate([cos_t] * reps, axis=1)
        sin_t = jnp.concatenate([sin_t] * reps, axis=1)
    return x * cos_t + swapped * sin_t


def _sb_body(q_ref, k_ref, v_ref, o_ref, *, tq, tk):
    q_start = pl.program_id(2) * tq
    n_kb = (q_start + tq) // tk
    row = lax.broadcasted_iota(jnp.int32, (tq, tk), 0) + q_start
    col = lax.broadcasted_iota(jnp.int32, (tq, tk), 1)
    later = (lax.broadcasted_iota(jnp.int32, (tk, tk), 0)
             > lax.broadcasted_iota(jnp.int32, (tk, tk), 1)).astype(BF16)
    outs = []
    for hh in range(2):
        lanes = slice(hh * HEAD_DIM, (hh + 1) * HEAD_DIM)
        q = q_ref[0, :, lanes]

        def body(it, carry, q=q, lanes=lanes):
            acc, run = carry
            ks = pl.multiple_of((n_kb - 1 - it) * tk, tk)
            k = k_ref[0, pl.ds(ks, tk), lanes]
            v = v_ref[0, pl.ds(ks, tk), lanes]
            z = lax.dot_general(q, k, (((1,), (1,)), ((), ())), preferred_element_type=F32)
            before = (col + ks) < row
            softplus = jnp.maximum(z, 0.0) + jnp.log(1.0 + jnp.exp(-jnp.abs(z)))
            log_keep = jnp.where(before, -softplus, 0.0)
            hi = log_keep.astype(BF16)
            lo = (log_keep - hi.astype(F32)).astype(BF16)
            log_after = (jnp.dot(hi, later, preferred_element_type=F32)
                         + jnp.dot(lo, later, preferred_element_type=F32)) + run
            wgt = jnp.where(before, jnp.exp(z - softplus + log_after), 0.0)
            acc = acc + jnp.dot(wgt.astype(BF16), v, preferred_element_type=F32)
            run = log_after[:, :1] + log_keep[:, :1]
            return acc, run

        acc, _ = lax.fori_loop(0, n_kb, body,
                               (jnp.zeros((tq, HEAD_DIM), F32), jnp.zeros((tq, 1), F32)))
        outs.append(acc)
    o_ref[0] = jnp.concatenate(outs, axis=1).astype(o_ref.dtype)


def sb_attention(qkv, batch, seq):
    tq = _pick(seq, (256, 128))
    tk = 128
    x = qkv.reshape(batch, seq, 3 * WIDTH)
    hp = WIDTH // LANES
    out = pl.pallas_call(
        functools.partial(_sb_body, tq=tq, tk=tk),
        grid=(batch, hp, seq // tq),
        in_specs=[pl.BlockSpec((1, tq, LANES), lambda b, h, i: (b, i, h)),
                  pl.BlockSpec((1, seq, LANES), lambda b, h, i: (b, 0, hp + h)),
                  pl.BlockSpec((1, seq, LANES), lambda b, h, i: (b, 0, 2 * hp + h))],
        out_specs=pl.BlockSpec((1, tq, LANES), lambda b, h, i: (b, i, h)),
        out_shape=jax.ShapeDtypeStruct((batch, seq, WIDTH), BF16),
        compiler_params=_params("parallel", "parallel", "parallel"),
        name="sb_attention",
    )(x, x, x)
    return out.reshape(batch * seq, WIDTH)


def _mla_body(q_ref, k_ref, v_ref, o_ref, *, tq, tk):
    q_start = pl.program_id(2) * tq
    n_kb = (q_start + tq) // tk
    row = lax.broadcasted_iota(jnp.int32, (tq, tk), 0) + q_start
    col = lax.broadcasted_iota(jnp.int32, (tq, tk), 1)
    outs = []
    for hh in range(2):
        q = q_ref[0, :, hh * LANES:(hh + 1) * LANES]

        def body(kb, carry, q=q, hh=hh):
            m, l, acc = carry
            ks = pl.multiple_of(kb * tk, tk)
            k = k_ref[0, pl.ds(ks, tk), hh * LANES:(hh + 1) * LANES]
            v = v_ref[0, pl.ds(ks, tk), hh * HEAD_DIM:(hh + 1) * HEAD_DIM]
            s = lax.dot_general(q, k, (((1,), (1,)), ((), ())), preferred_element_type=F32)
            s = jnp.where((col + ks) <= row, s, -1e30)
            m_new = jnp.maximum(m, jnp.max(s, axis=-1, keepdims=True))
            p = jnp.exp(s - m_new)
            corr = jnp.exp(m - m_new)
            l = corr * l + jnp.sum(p, axis=-1, keepdims=True)
            acc = corr * acc + jnp.dot(p.astype(BF16), v, preferred_element_type=F32)
            return m_new, l, acc

        m, l, acc = lax.fori_loop(
            0, n_kb, body,
            (jnp.full((tq, 1), -1e30, F32), jnp.zeros((tq, 1), F32), jnp.zeros((tq, HEAD_DIM), F32)))
        outs.append(acc / l)
    o_ref[0] = jnp.concatenate(outs, axis=1).astype(o_ref.dtype)


def mla_attention(q, k, v, batch, seq):
    tq = _pick(seq, (256, 128))
    tk = 128
    hp = HEADS // 2
    out = pl.pallas_call(
        functools.partial(_mla_body, tq=tq, tk=tk),
        grid=(batch, hp, seq // tq),
        in_specs=[pl.BlockSpec((1, tq, 2 * LANES), lambda b, h, i: (b, i, h)),
                  pl.BlockSpec((1, seq, 2 * LANES), lambda b, h, i: (b, 0, h)),
                  pl.BlockSpec((1, seq, LANES), lambda b, h, i: (b, 0, h))],
        out_specs=pl.BlockSpec((1, tq, LANES), lambda b, h, i: (b, i, h)),
        out_shape=jax.ShapeDtypeStruct((batch, seq, WIDTH), BF16),
        compiler_params=_params("parallel", "parallel", "parallel"),
        name="mla_attention",
    )(q.reshape(batch, seq, HEADS * LANES), k.reshape(batch, seq, HEADS * LANES),
      v.reshape(batch, seq, WIDTH))
    return out.reshape(batch * seq, WIDTH)


RWKV_IN = 3 * WIDTH + 2 * RWKV_LORA_PAD + RWKV_GATE_LORA


def _rwkv_prep_body(*refs, tiles_per_seq, has_vres):
    if has_vres:
        (cols_ref, prev_ref, mu_ref, w0_ref, w2_ref, a0_ref, a2_ref, g2_ref, kk_ref, ka_ref,
         v0_ref, v1_ref, v2_ref, vfirst_ref,
         r_out, lw_out, k_out, v_out, kk_out, a_out, g_out) = refs
    else:
        (cols_ref, prev_ref, mu_ref, w0_ref, w2_ref, a0_ref, a2_ref, g2_ref, kk_ref, ka_ref,
         r_out, lw_out, k_out, v_out, kk_out, a_out, g_out) = refs
    x = cols_ref[...]
    tm = x.shape[0]
    starts_seq = (pl.program_id(0) % tiles_per_seq) == 0
    last_prev = jnp.where(starts_seq, 0.0, prev_ref[7:8, :])
    first_row = lax.broadcasted_iota(jnp.int32, x.shape, 0) == 0
    prev = jnp.where(first_row, last_prev, pltpu.roll(x, 1, 0))
    x = x + (prev - x) * mu_ref[...]
    r = x[:, :WIDTH]
    k = x[:, WIDTH:2 * WIDTH]
    v = x[:, 2 * WIDTH:3 * WIDTH]
    o = 3 * WIDTH
    wd = x[:, o:o + RWKV_LORA_PAD]
    ad = x[:, o + RWKV_LORA_PAD:o + 2 * RWKV_LORA_PAD]
    gd = x[:, o + 2 * RWKV_LORA_PAD:]
    dot = functools.partial(jnp.dot, preferred_element_type=F32)
    wpre = w0_ref[...] + dot(jnp.tanh(wd).astype(BF16), w2_ref[...])
    w = -jax.nn.softplus(-wpre) - 0.5
    lw_out[...] = -jnp.exp(w)
    a = jax.nn.sigmoid(a0_ref[...] + dot(ad.astype(BF16), a2_ref[...]))
    g_out[...] = dot(jax.nn.sigmoid(gd).astype(BF16), g2_ref[...])
    if has_vres:
        low = dot(v.astype(BF16), v1_ref[...])
        mix = jax.nn.sigmoid(v0_ref[...] + dot(low.astype(BF16), v2_ref[...]))
        v = v + (vfirst_ref[...] - v) * mix
    r_out[...] = r
    v_out[...] = v
    kk_out[...] = k * kk_ref[...]
    k_out[...] = k * (1.0 + (a - 1.0) * ka_ref[...])
    a_out[...] = a


def rwkv_prep(cols, seq, mu, w0, w2, a0, a2, g2, k_k, k_a, vres):
    m = cols.shape[0]
    tm = _pick(seq, (256, 128, 64, 32, 16, 8))
    row = lambda t: t.reshape(1, -1).astype(F32)
    args = [cols, cols, row(mu), row(w0), w2, row(a0), a2, g2, row(k_k), row(k_a)]
    full = lambda arr: pl.BlockSpec(arr.shape, lambda i: (0, 0))
    tile = pl.BlockSpec((tm, WIDTH), lambda i: (i, 0))
    in_specs = [pl.BlockSpec((tm, RWKV_IN), lambda i: (i, 0)),
                pl.BlockSpec((8, RWKV_IN), lambda i: (jnp.maximum(i * (tm // 8) - 1, 0), 0))]
    in_specs += [full(t) for t in args[2:]]
    if vres is not None:
        v0, v1, v2, v_first = vres
        extra = [row(v0), v1, v2]
        args += extra + [v_first]
        in_specs += [full(t) for t in extra] + [tile]
    return pl.pallas_call(
        functools.partial(_rwkv_prep_body, tiles_per_seq=seq // tm, has_vres=vres is not None),
        grid=(m // tm,),
        in_specs=in_specs,
        out_specs=[tile] * 7,
        out_shape=[jax.ShapeDtypeStruct((m, WIDTH), F32)] * 7,
        compiler_params=_params("parallel"),
        name="rwkv_prep",
    )(*args)


def _rwkv_chunk(r, lw, k, v, kk_raw, a, state):
    c = r.shape[0]
    dot = functools.partial(jnp.dot, preferred_element_type=F32, precision=HIGHEST)
    dot_nt = lambda x, y: lax.dot_general(x, y, (((1,), (1,)), ((), ())),
                                          preferred_element_type=F32, precision=HIGHEST)
    dot_tn = lambda x, y: lax.dot_general(x, y, (((0,), (0,)), ((), ())),
                                          preferred_element_type=F32, precision=HIGHEST)
    ti = lax.broadcasted_iota(jnp.int32, (c, c), 0)
    tj = lax.broadcasted_iota(jnp.int32, (c, c), 1)
    norm = jnp.sqrt(jnp.sum(kk_raw * kk_raw, axis=-1, keepdims=True))
    kk = kk_raw / jnp.maximum(norm, 1e-12)
    alpha = -kk
    beta = kk * a
    cl = dot((ti >= tj).astype(F32), lw)
    a_t = alpha * jnp.exp(cl - lw)
    r_t = r * jnp.exp(cl)
    inv = jnp.exp(-cl)
    b_t = beta * inv
    k_t = k * inv
    ar = jnp.concatenate([a_t, r_t], axis=0)
    bk = jnp.concatenate([b_t, k_t], axis=0)
    pair = dot_nt(ar, bk)
    ars = dot_nt(ar, state)
    strict = ti > tj
    lower = jnp.where(strict, pair[:c, :c], 0.0)
    rhs = ars[:c] + dot(jnp.where(strict, pair[:c, c:], 0.0), v)
    tinv = (ti == tj).astype(F32)
    half = 1
    while half < c:
        off = (ti // (2 * half) == tj // (2 * half)) & (ti % (2 * half) >= half) & (tj % (2 * half) < half)
        tinv = tinv + dot(dot(tinv, jnp.where(off, lower, 0.0)), tinv)
        half *= 2
    u = dot(tinv, rhs)
    incl = ti >= tj
    uv = jnp.concatenate([u, v], axis=0)
    rb_rk = jnp.concatenate([jnp.where(incl, pair[c:, :c], 0.0),
                             jnp.where(incl, pair[c:, c:], 0.0)], axis=1)
    y = ars[c:] + dot(rb_rk, uv)
    tail = jnp.exp(cl[c - 1:c, :] - cl)
    bk_tail = jnp.concatenate([beta * tail, k * tail], axis=0)
    new_state = state * jnp.exp(cl[c - 1:c, :]) + dot_tn(uv, bk_tail)
    return y, new_state


def _rwkv_scan_body(r_ref, lw_ref, k_ref, v_ref, kk_ref, a_ref, g_ref, lng_ref, lnb_ref, rk_ref,
                    o_ref, state_ref, *, chunk):
    @pl.when(pl.program_id(2) == 0)
    def _():
        state_ref[...] = jnp.zeros_like(state_ref)

    n_chunks = r_ref.shape[1] // chunk

    def body(ci, _):
        rows = pl.ds(pl.multiple_of(ci * chunk, chunk), chunk)
        outs = []
        for hh in range(2):
            lanes = slice(hh * HEAD_DIM, (hh + 1) * HEAD_DIM)
            r = r_ref[0, rows, lanes]
            k = k_ref[0, rows, lanes]
            v = v_ref[0, rows, lanes]
            y, new_state = _rwkv_chunk(r, lw_ref[0, rows, lanes], k, v, kk_ref[0, rows, lanes],
                                       a_ref[0, rows, lanes], state_ref[hh])
            state_ref[hh] = new_state
            mean = jnp.mean(y, axis=-1, keepdims=True)
            var = jnp.mean(jnp.square(y - mean), axis=-1, keepdims=True)
            y = (y - mean) * lax.rsqrt(var + RWKV_GN_EPS) * lng_ref[:, lanes] + lnb_ref[:, lanes]
            bonus = jnp.sum(r * k * rk_ref[:, lanes], axis=-1, keepdims=True) * v
            outs.append((y + bonus) * g_ref[0, rows, lanes])
        o_ref[0, rows, :] = jnp.concatenate(outs, axis=1).astype(o_ref.dtype)
        return 0

    lax.fori_loop(0, n_chunks, body, 0)


def rwkv_scan(r, lw, k, v, kk, a, g, ln_g, ln_b, r_k, batch, seq):
    chunk = min(RWKV_CHUNK, seq)
    rows = _pick(seq, (256, 128, 64, 32, 16, 8))
    hp = WIDTH // LANES
    seq_spec = pl.BlockSpec((1, rows, LANES), lambda b, h, c: (b, c, h))
    par_spec = pl.BlockSpec((1, LANES), lambda b, h, c: (0, h))
    shaped = [t.reshape(batch, seq, WIDTH) for t in (r, lw, k, v, kk, a, g)]
    pars = [t.reshape(1, WIDTH).astype(F32) for t in (ln_g, ln_b, r_k)]
    out = pl.pallas_call(
        functools.partial(_rwkv_scan_body, chunk=chunk),
        grid=(batch, hp, seq // rows),
        in_specs=[seq_spec] * 7 + [par_spec] * 3,
        out_specs=seq_spec,
        out_shape=jax.ShapeDtypeStruct((batch, seq, WIDTH), BF16),
        scratch_shapes=[pltpu.VMEM((2, HEAD_DIM, HEAD_DIM), F32)],
        compiler_params=_params("parallel", "parallel", "arbitrary"),
        name="rwkv_scan",
    )(*shaped, *pars)
    return out.reshape(batch * seq, WIDTH)


def _merge_body(a1, a2, a3, w1, w2, w3, g1, g2, g3, o_ref):
    dot = functools.partial(jnp.dot, preferred_element_type=F32)
    acc = g1[...] * dot(a1[...], w1[...])
    acc += g2[...] * dot(a2[...], w2[...])
    acc += g3[...] * dot(a3[...], w3[...])
    o_ref[...] = acc.astype(o_ref.dtype)


def merge_branches(outs, weights, gates):
    m = outs[0].shape[0]
    d = weights[0].shape[1]
    tm = _pick(m, (512, 256, 128, 64, 32, 16, 8))
    tn = _pick(d, (512, 256, 128))
    nb = d // tn
    a_spec = pl.BlockSpec((tm, WIDTH), lambda i, j: (i, 0))
    w_spec = pl.BlockSpec((WIDTH, tn), lambda i, j: (0, j))
    g_specs = [pl.BlockSpec((tm, tn), lambda i, j, b=b: (i, j + b * nb)) for b in range(3)]
    return pl.pallas_call(
        _merge_body,
        grid=(m // tm, nb),
        in_specs=[a_spec] * 3 + [w_spec] * 3 + g_specs,
        out_specs=pl.BlockSpec((tm, tn), lambda i, j: (i, j)),
        out_shape=jax.ShapeDtypeStruct((m, d), BF16),
        compiler_params=_params("parallel", "parallel"),
        name="merge_branches",
    )(*outs, *weights, gates, gates, gates)


ROUTER_PAD = 128


def _router_body(x_ref, g_ref, w_ref, b_ref, h_ref, logit_ref):
    x = x_ref[...]
    y = x * lax.rsqrt(jnp.mean(x * x, axis=-1, keepdims=True) + NORM_EPS)
    h = y * g_ref[...]
    h_ref[...] = h
    logit_ref[...] = jnp.dot(h, w_ref[...], preferred_element_type=F32, precision=HIGHEST) + b_ref[...]


def router(x, g, w_pad, b_pad):
    m, d = x.shape
    tm = _pick(m, (256, 128, 64, 32, 16, 8))
    return pl.pallas_call(
        _router_body,
        grid=(m // tm,),
        in_specs=[pl.BlockSpec((tm, d), lambda i: (i, 0)),
                  pl.BlockSpec((1, d), lambda i: (0, 0)),
                  pl.BlockSpec((d, ROUTER_PAD), lambda i: (0, 0)),
                  pl.BlockSpec((1, ROUTER_PAD), lambda i: (0, 0))],
        out_specs=[pl.BlockSpec((tm, d), lambda i: (i, 0)),
                   pl.BlockSpec((tm, ROUTER_PAD), lambda i: (i, 0))],
        out_shape=[jax.ShapeDtypeStruct((m, d), F32), jax.ShapeDtypeStruct((m, ROUTER_PAD), F32)],
        compiler_params=_params("parallel"),
        name="ffn_norm_router",
    )(x, g.reshape(1, d).astype(F32), w_pad, b_pad)


def _gather_body(idx_ref, nrows_ref, src_ref, out_ref, zeros_ref, sem, *, rows):
    base = pl.program_id(0) * rows

    @pl.when(pl.program_id(0) == 0)
    def _():
        zeros_ref[...] = jnp.zeros_like(zeros_ref)

    def row_copy(r):
        return pltpu.make_async_copy(src_ref.at[pl.ds(idx_ref[base + r], 1)],
                                     out_ref.at[pl.ds(base + r, 1)], sem)

    @pl.when(base < nrows_ref[0])
    def _():
        def start(r, _):
            row_copy(r).start()
            return 0

        def wait(r, _):
            row_copy(r).wait()
            return 0

        lax.fori_loop(0, rows, start, 0)
        lax.fori_loop(0, rows, wait, 0)

    @pl.when(base >= nrows_ref[0])
    def _():
        fill = pltpu.make_async_copy(zeros_ref, out_ref.at[pl.ds(base, rows)], sem)
        fill.start()
        fill.wait()


def gather_rows(src, idx, n_rows, name):
    p = idx.shape[0]
    rows = _pick(p, (256, 128, 64, 32, 16, 8))
    return pl.pallas_call(
        functools.partial(_gather_body, rows=rows),
        grid_spec=pltpu.PrefetchScalarGridSpec(
            num_scalar_prefetch=2,
            grid=(p // rows,),
            in_specs=[pl.BlockSpec(memory_space=pl.ANY)],
            out_specs=pl.BlockSpec(memory_space=pl.ANY),
            scratch_shapes=[pltpu.VMEM((rows, src.shape[1]), src.dtype),
                            pltpu.SemaphoreType.DMA(())]),
        out_shape=jax.ShapeDtypeStruct((p, src.shape[1]), src.dtype),
        compiler_params=_params("arbitrary"),
        name=name,
    )(idx, n_rows, src)


def _expert_body(be_ref, nb_ref, x_ref, wg_ref, wu_ref, wd_ref, o_ref, wg_s, wu_s, wd_s):
    i = pl.program_id(0)
    e = be_ref[i]
    e_prev = be_ref[jnp.maximum(i - 1, 0)]

    @pl.when((i == 0) | (e != e_prev))
    def _():
        wg_s[...] = wg_ref[0].astype(BF16)
        wu_s[...] = wu_ref[0].astype(BF16)
        wd_s[...] = wd_ref[0].astype(BF16)

    @pl.when(i < nb_ref[0])
    def _():
        x = x_ref[...].astype(BF16)
        gate = jnp.dot(x, wg_s[...], preferred_element_type=F32)
        up = jnp.dot(x, wu_s[...], preferred_element_type=F32)
        mid = (jax.nn.silu(gate) * up).astype(BF16)
        o_ref[...] = jnp.dot(mid, wd_s[...], preferred_element_type=F32)

    @pl.when(i >= nb_ref[0])
    def _():
        o_ref[...] = jnp.zeros_like(o_ref)


def expert_ffn(xb, block_e, n_blocks, w_gate, w_up, w_down):
    p, d = xb.shape
    ff = w_gate.shape[2]
    return pl.pallas_call(
        _expert_body,
        grid_spec=pltpu.PrefetchScalarGridSpec(
            num_scalar_prefetch=2,
            grid=(p // MOE_ROWS,),
            in_specs=[pl.BlockSpec((MOE_ROWS, d), lambda i, be, nb: (i, 0)),
                      pl.BlockSpec((1, d, ff), lambda i, be, nb: (be[i], 0, 0)),
                      pl.BlockSpec((1, d, ff), lambda i, be, nb: (be[i], 0, 0)),
                      pl.BlockSpec((1, ff, d), lambda i, be, nb: (be[i], 0, 0))],
            out_specs=pl.BlockSpec((MOE_ROWS, d), lambda i, be, nb: (i, 0)),
            scratch_shapes=[pltpu.VMEM((d, ff), BF16), pltpu.VMEM((d, ff), BF16),
                            pltpu.VMEM((ff, d), BF16)]),
        out_shape=jax.ShapeDtypeStruct((p, d), F32),
        compiler_params=_params("arbitrary"),
        name="expert_ffn",
    )(block_e, n_blocks, xb, w_gate, w_up, w_down)


def _combine_body(x_ref, y0_ref, y1_ref, gate_ref, o_ref):
    gate = gate_ref[...]
    o_ref[...] = x_ref[...] + (gate[:, 0:1] * y0_ref[...] + gate[:, 1:2] * y1_ref[...])


def moe_combine(x, y0, y1, gate):
    m, d = x.shape
    tm = _pick(m, (256, 128, 64, 32, 16, 8))
    tile = pl.BlockSpec((tm, d), lambda i: (i, 0))
    return pl.pallas_call(
        _combine_body,
        grid=(m // tm,),
        in_specs=[tile, tile, tile, pl.BlockSpec((tm, TOP_K), lambda i: (i, 0))],
        out_specs=tile,
        out_shape=jax.ShapeDtypeStruct((m, d), F32),
        compiler_params=_params("parallel"),
        name="moe_combine",
    )(x, y0, y1, gate)


def _routing_tables(logits):
    t = logits.shape[0]
    n_assign = t * TOP_K
    tok = jnp.arange(t)
    g_logits = logits[:, :N_GROUPS]
    grp = jnp.argmax(g_logits, axis=-1)
    p_grp = jax.nn.softmax(g_logits, axis=-1)[tok, grp]
    e_logits = logits[:, N_GROUPS:N_GROUPS + N_EXPERTS].reshape(t, N_GROUPS, EXPERTS_PER_GROUP)
    p_in, idx_in = lax.top_k(jax.nn.softmax(e_logits[tok, grp], axis=-1), TOP_K)
    gate = p_grp[:, None] * p_in / jnp.sum(p_in, axis=-1, keepdims=True)
    flat_e = (grp[:, None] * EXPERTS_PER_GROUP + idx_in).reshape(n_assign).astype(jnp.int32)
    onehot = (flat_e[:, None] == jnp.arange(N_EXPERTS, dtype=jnp.int32)[None, :]).astype(jnp.int32)
    rank = jnp.sum((jnp.cumsum(onehot, axis=0) - onehot) * onehot, axis=1)
    counts = jnp.sum(onehot, axis=0)
    padded = (counts + MOE_ROWS - 1) // MOE_ROWS * MOE_ROWS
    pend = jnp.cumsum(padded)
    dest = ((pend - padded)[flat_e] + rank).astype(jnp.int32)
    n_blocks_max = -(-n_assign // MOE_ROWS) + N_EXPERTS
    row_tok = jnp.zeros((n_blocks_max * MOE_ROWS,), jnp.int32).at[dest].set(
        jnp.repeat(tok, TOP_K).astype(jnp.int32))
    block_e = jnp.minimum(
        jnp.searchsorted(pend, jnp.arange(n_blocks_max) * MOE_ROWS, side="right"),
        N_EXPERTS - 1).astype(jnp.int32)
    n_blocks = (pend[-1:] // MOE_ROWS).astype(jnp.int32)
    return gate.astype(F32), dest.reshape(t, TOP_K), row_tok, block_e, n_blocks


def hier_moe_residual(x, norm_g, rg_w, rg_b, re_w, re_b, w_gate, w_up, w_down):
    t, d = x.shape
    w_pad = jnp.zeros((d, ROUTER_PAD), F32).at[:, :N_GROUPS].set(rg_w)
    w_pad = w_pad.at[:, N_GROUPS:N_GROUPS + N_EXPERTS].set(re_w)
    b_pad = jnp.zeros((1, ROUTER_PAD), F32).at[0, :N_GROUPS].set(rg_b)
    b_pad = b_pad.at[0, N_GROUPS:N_GROUPS + N_EXPERTS].set(re_b)
    h, logits = router(x, norm_g, w_pad, b_pad)
    gate, dest, row_tok, block_e, n_blocks = _routing_tables(logits)
    xb = gather_rows(h, row_tok, n_blocks * MOE_ROWS, "moe_dispatch")
    yb = expert_ffn(xb, block_e, n_blocks, w_gate, w_up, w_down)
    all_rows = jnp.full((1,), t, jnp.int32)
    y0 = gather_rows(yb, dest[:, 0], all_rows, "moe_collect0")
    y1 = gather_rows(yb, dest[:, 1], all_rows, "moe_collect1")
    return moe_combine(x, y0, y1, gate)


def _mla_q_weight(w_uq):
    w = w_uq.reshape(MLA_Q_LORA, HEADS, HEAD_DIM + MLA_ROPE_DIM)
    w = jnp.pad(w, ((0, 0), (0, 0), (0, LANES - HEAD_DIM - MLA_ROPE_DIM)))
    return w.reshape(MLA_Q_LORA, HEADS * LANES).astype(BF16)


def _mla_kv_weights(w_ukv):
    w = w_ukv.reshape(MLA_KV_LORA, HEADS, 2 * HEAD_DIM)
    wk = jnp.pad(w[:, :, :HEAD_DIM], ((0, 0), (0, 0), (0, LANES - HEAD_DIM)))
    wv = w[:, :, HEAD_DIM:]
    return (wk.reshape(MLA_KV_LORA, HEADS * LANES).astype(BF16),
            wv.reshape(MLA_KV_LORA, WIDTH).astype(BF16))


def _rope_slot_tables(positions):
    half = MLA_ROPE_DIM // 2
    inv_freq = ROPE_THETA ** (-jnp.arange(0, MLA_ROPE_DIM, 2, dtype=F32) / MLA_ROPE_DIM)
    ang = positions.astype(F32).reshape(-1, 1) * inv_freq[None, :]
    cos, sin = jnp.cos(ang), jnp.sin(ang)
    t = ang.shape[0]
    pad = jnp.zeros((t, LANES - HEAD_DIM - 2 * half), F32)
    cos_t = jnp.concatenate([jnp.ones((t, HEAD_DIM), F32), cos, cos, pad], axis=1)
    sin_t = jnp.concatenate([jnp.zeros((t, HEAD_DIM), F32), -sin, sin, pad], axis=1)
    return cos_t, sin_t


def _mixer_block(x, l, batch, seq, cos_t, sin_t, v_first, p):
    t, d = x.shape
    h = rmsnorm(x, p["attn_norm_g"][l], BF16, name="attn_norm")
    w_in = p["w_in"][l]
    o_rwkv = 3 * WIDTH
    n_rwkv = 3 * WIDTH + RWKV_DECAY_LORA + RWKV_ICLR_LORA + RWKV_GATE_LORA
    o_mla = o_rwkv + n_rwkv
    n_mla = MLA_Q_LORA + MLA_KV_LORA + MLA_ROPE_DIM
    o_gate = o_mla + n_mla

    qscale = jnp.concatenate([jnp.full((1, WIDTH), HEAD_DIM ** -0.5, F32), jnp.ones((1, 2 * WIDTH), F32)], axis=1)
    (qkv,) = matmul(h, w_in[:, :o_rwkv].astype(BF16), [BF16],
                    epilogue=lambda acc, s: (acc * s,), extras=[(qscale, "col", 0)], name="proj_sb")
    o_sb = sb_attention(qkv, batch, seq)

    wr = w_in[:, o_mla - n_rwkv:o_mla]
    zpad = jnp.zeros((d, RWKV_LORA_PAD - RWKV_DECAY_LORA), F32)
    c0 = 3 * WIDTH
    w_rwkv = jnp.concatenate([wr[:, :c0], wr[:, c0:c0 + RWKV_DECAY_LORA], zpad,
                              wr[:, c0 + RWKV_DECAY_LORA:c0 + 2 * RWKV_DECAY_LORA], zpad,
                              wr[:, c0 + 2 * RWKV_DECAY_LORA:]], axis=1).astype(BF16)
    (cols,) = matmul(h, w_rwkv, [F32], name="proj_rwkv")
    mu = p["rwkv_mu"][l]
    mpad = jnp.zeros((RWKV_LORA_PAD - RWKV_DECAY_LORA,), F32)
    mu_pad = jnp.concatenate([mu[:c0], mu[c0:c0 + RWKV_DECAY_LORA], mpad,
                              mu[c0 + RWKV_DECAY_LORA:c0 + 2 * RWKV_DECAY_LORA], mpad,
                              mu[c0 + 2 * RWKV_DECAY_LORA:]])
    rpad = ((0, RWKV_LORA_PAD - RWKV_DECAY_LORA), (0, 0))
    vres = None
    if l > 0:
        vres = (p["rwkv_v0"][l - 1], p["rwkv_v1"][l - 1].astype(BF16), p["rwkv_v2"][l - 1].astype(BF16), v_first)
    r, lw, k, v, kk, a, g = rwkv_prep(
        cols, seq, mu_pad, p["rwkv_w0"][l], jnp.pad(p["rwkv_w2"][l], rpad).astype(BF16),
        p["rwkv_a0"][l], jnp.pad(p["rwkv_a2"][l], rpad).astype(BF16), p["rwkv_g2"][l].astype(BF16),
        p["rwkv_k_k"][l], p["rwkv_k_a"][l], vres)
    if l == 0:
        v_first = v
    o_rw = rwkv_scan(r, lw, k, v, kk, a, g, p["rwkv_ln_g"][l], p["rwkv_ln_b"][l], p["rwkv_r_k"][l],
                     batch, seq)

    wm = w_in[:, o_mla:o_gate]
    w_mla = jnp.concatenate([wm[:, :MLA_Q_LORA + MLA_KV_LORA], jnp.zeros((d, HEAD_DIM), F32),
                             wm[:, MLA_Q_LORA + MLA_KV_LORA:],
                             jnp.zeros((d, LANES - HEAD_DIM - MLA_ROPE_DIM), F32)], axis=1).astype(BF16)
    (mcols,) = matmul(h, w_mla, [F32], name="proj_mla")
    cq = rmsnorm(mcols, p["mla_q_norm_g"][l], BF16, width=MLA_Q_LORA, col_block=0, name="mla_q_norm")
    ckv = rmsnorm(mcols, p["mla_kv_norm_g"][l], BF16, width=MLA_KV_LORA,
                  col_block=MLA_Q_LORA // MLA_KV_LORA, name="mla_kv_norm")
    mla_scale = (HEAD_DIM + MLA_ROPE_DIM) ** -0.5
    (q_mla,) = matmul(cq, _mla_q_weight(p["mla_w_uq"][l]), [BF16],
                      epilogue=lambda acc, c, s: (_rope_slots(acc, c, s) * mla_scale,),
                      extras=[(cos_t, "row", 0), (sin_t, "row", 0)], name="mla_q_up")
    wk, wv = _mla_kv_weights(p["mla_w_ukv"][l])
    kpe_block = (MLA_Q_LORA + MLA_KV_LORA) // LANES

    def k_epilogue(acc, kpe, c, s):
        kr = _rope_slots(kpe, c, s)
        return (acc + jnp.concatenate([kr] * (acc.shape[1] // LANES), axis=1),)

    kpe = mcols[:, kpe_block * LANES:(kpe_block + 1) * LANES]
    (k_mla,) = matmul(ckv, wk, [BF16], epilogue=k_epilogue,
                      extras=[(kpe, "row", 0), (cos_t, "row", 0), (sin_t, "row", 0)], name="mla_k_up")
    (v_mla,) = matmul(ckv, wv, [BF16], name="mla_v_up")
    o_mla_out = mla_attention(q_mla, k_mla, v_mla, batch, seq)

    (gates,) = matmul(h, w_in[:, o_gate:].astype(BF16), [F32],
                      epilogue=lambda acc: (jax.nn.sigmoid(acc),), name="proj_gates")
    merged = merge_branches([o_sb, o_rw, o_mla_out],
                            [p["w_br_sb"][l].astype(BF16), p["w_br_rwkv"][l].astype(BF16),
                             p["w_br_mla"][l].astype(BF16)], gates)
    (x,) = matmul(merged, p["w_out"][l].astype(BF16), [F32],
                  epilogue=lambda acc, res: (res + acc,), extras=[(x, "tile", 0)], name="out_proj")
    return x, v_first


def kernel(x, positions, attn_norm_g, w_in, rwkv_mu, rwkv_w0, rwkv_w2, rwkv_a0, rwkv_a2, rwkv_g2, rwkv_k_k, rwkv_k_a, rwkv_r_k, rwkv_ln_g, rwkv_ln_b, rwkv_v0, rwkv_v1, rwkv_v2, mla_q_norm_g, mla_w_uq, mla_kv_norm_g, mla_w_ukv, w_br_sb, w_br_rwkv, w_br_mla, w_out, ffn_norm_g, router_group_w, router_group_b, router_expert_w, router_expert_b, expert_w_gate, expert_w_up, expert_w_down, final_norm_g):
    p = dict(attn_norm_g=attn_norm_g, w_in=w_in, rwkv_mu=rwkv_mu, rwkv_w0=rwkv_w0, rwkv_w2=rwkv_w2,
             rwkv_a0=rwkv_a0, rwkv_a2=rwkv_a2, rwkv_g2=rwkv_g2, rwkv_k_k=rwkv_k_k, rwkv_k_a=rwkv_k_a,
             rwkv_r_k=rwkv_r_k, rwkv_ln_g=rwkv_ln_g, rwkv_ln_b=rwkv_ln_b, rwkv_v0=rwkv_v0,
             rwkv_v1=rwkv_v1, rwkv_v2=rwkv_v2, mla_q_norm_g=mla_q_norm_g, mla_w_uq=mla_w_uq,
             mla_kv_norm_g=mla_kv_norm_g, mla_w_ukv=mla_w_ukv, w_br_sb=w_br_sb, w_br_rwkv=w_br_rwkv,
             w_br_mla=w_br_mla, w_out=w_out)
    batch, seq, d = x.shape
    depth = w_in.shape[0]
    cos_t, sin_t = _rope_slot_tables(positions)
    xt = x.reshape(batch * seq, d)
    v_first = None
    for l in range(depth):
        xt, v_first = _mixer_block(xt, l, batch, seq, cos_t, sin_t, v_first, p)
        xt = hier_moe_residual(xt, ffn_norm_g[l], router_group_w[l], router_group_b[l],
                               router_expert_w[l], router_expert_b[l],
                               expert_w_gate[l], expert_w_up[l], expert_w_down[l])
    return rmsnorm(xt, final_norm_g, x.dtype, name="final_norm").reshape(batch, seq, d)
```

```python
import functools

import jax
import jax.numpy as jnp
from jax import lax
from jax.experimental import pallas as pl
from jax.experimental.pallas import tpu as pltpu

F32 = jnp.float32
BF16 = jnp.bfloat16
HIGHEST = lax.Precision.HIGHEST

NORM_EPS = 1e-6
HEADS = 16
HEAD_DIM = 64
WIDTH = HEADS * HEAD_DIM
LANES = 128
RWKV_DECAY_LORA = 96
RWKV_ICLR_LORA = 96
RWKV_GATE_LORA = 256
RWKV_LORA_PAD = 128
RWKV_GN_EPS = 64e-5
RWKV_CHUNK = 64
MLA_Q_LORA = 768
MLA_KV_LORA = 256
MLA_ROPE_DIM = 32
ROPE_THETA = 10000.0
N_GROUPS = 8
EXPERTS_PER_GROUP = 8
N_EXPERTS = N_GROUPS * EXPERTS_PER_GROUP
TOP_K = 2
MOE_ROWS = 256
VMEM_LIMIT = 48 * 1024 * 1024


def _params(*sem):
    return pltpu.CompilerParams(dimension_semantics=sem, vmem_limit_bytes=VMEM_LIMIT)


def _pick(n, cands):
    for c in cands:
        if n % c == 0:
            return c
    raise ValueError(f"no tile for {n} in {cands}")


def _dot(x, y):
    return jnp.dot(x, y, preferred_element_type=F32)


def _dot_nt(x, y):
    return lax.dot_general(x, y, (((1,), (1,)), ((), ())), preferred_element_type=F32)


def _dot_tn(x, y):
    return lax.dot_general(x, y, (((0,), (0,)), ((), ())), preferred_element_type=F32)


def _split(x):
    hi = x.astype(BF16)
    return hi, (x - hi.astype(F32)).astype(BF16)


def _rmsnorm_body(x_ref, g_ref, o_ref):
    x = x_ref[...].astype(F32)
    y = x * lax.rsqrt(jnp.mean(x * x, axis=-1, keepdims=True) + NORM_EPS)
    o_ref[...] = (y * g_ref[...]).astype(o_ref.dtype)


def rmsnorm(x, g, out_dtype, *, width=None, col_block=0, name="rmsnorm"):
    m = x.shape[0]
    width = x.shape[1] if width is None else width
    tm = _pick(m, (512, 256, 128, 64, 32, 16, 8))
    return pl.pallas_call(
        _rmsnorm_body,
        grid=(m // tm,),
        in_specs=[pl.BlockSpec((tm, width), lambda i: (i, col_block)),
                  pl.BlockSpec((1, width), lambda i: (0, 0))],
        out_specs=pl.BlockSpec((tm, width), lambda i: (i, 0)),
        out_shape=jax.ShapeDtypeStruct((m, width), out_dtype),
        compiler_params=_params("parallel"),
        name=name,
    )(x, g.reshape(1, width).astype(F32))


def _mm_body(*refs, n_extra, epilogue):
    a_ref, w_ref = refs[:2]
    extra = [r[...] for r in refs[2:2 + n_extra]]
    outs = refs[2 + n_extra:]
    acc = _dot(a_ref[...], w_ref[...])
    res = epilogue(acc, *extra)
    for o, r in zip(outs, res):
        o[...] = r.astype(o.dtype)


def matmul(a, w, out_dtypes, *, epilogue=None, extras=(), a_col_block=0, name="matmul"):
    m = a.shape[0]
    k, n = w.shape
    tm = _pick(m, (512, 256, 128, 64, 32, 16, 8))
    tn = n if n <= 1280 else _pick(n, (512, 256, 128))
    if epilogue is None:
        epilogue = lambda acc: (acc,)
    in_specs = [pl.BlockSpec((tm, k), lambda i, j: (i, a_col_block)),
                pl.BlockSpec((k, tn), lambda i, j: (0, j))]
    args = [a, w]
    for arr, kind, off in extras:
        if kind == "row":
            in_specs.append(pl.BlockSpec((tm, arr.shape[1]), lambda i, j: (i, 0)))
        elif kind == "col":
            in_specs.append(pl.BlockSpec((1, tn), lambda i, j: (0, j)))
        else:
            in_specs.append(pl.BlockSpec((tm, tn), lambda i, j, off=off: (i, j + off)))
        args.append(arr)
    return pl.pallas_call(
        functools.partial(_mm_body, n_extra=len(extras), epilogue=epilogue),
        grid=(m // tm, n // tn),
        in_specs=in_specs,
        out_specs=[pl.BlockSpec((tm, tn), lambda i, j: (i, j)) for _ in out_dtypes],
        out_shape=[jax.ShapeDtypeStruct((m, n), dt) for dt in out_dtypes],
        compiler_params=_params("parallel", "parallel"),
        name=name,
    )(*args)


def _rope_slots(x, cos_t, sin_t):
    n = x.shape[-1]
    reps = n // LANES
    lane = lax.broadcasted_iota(jnp.int32, x.shape, 1) % LANES
    from_hi = pltpu.roll(x, n - MLA_ROPE_DIM // 2, 1)
    from_lo = pltpu.roll(x, MLA_ROPE_DIM // 2, 1)
    swapped = jnp.where(lane < HEAD_DIM + MLA_ROPE_DIM // 2, from_hi, from_lo)
    if reps > 1:
        cos_t = jnp.concatenate([cos_t] * reps, axis=1)
        sin_t = jnp.concatenate([sin_t] * reps, axis=1)
    return x * cos_t + swapped * sin_t


def _sb_body(q_ref, k_ref, v_ref, o_ref, *, t):
    qi = pl.program_id(2)
    rows = lax.broadcasted_iota(jnp.int32, (t, t), 0)
    cols = lax.broadcasted_iota(jnp.int32, (t, t), 1)
    later = (rows > cols).astype(BF16)
    heads = [slice(hh * HEAD_DIM, (hh + 1) * HEAD_DIM) for hh in range(2)]
    q = [q_ref[0, :, h] for h in heads]

    def block(ks, carry, diagonal):
        z, softplus, log_keep, parts = [], [], [], []
        for hh, h in enumerate(heads):
            zz = _dot_nt(q[hh], k_ref[0, pl.ds(ks, t), h])
            sp = jnp.maximum(zz, 0.0) + jnp.log(1.0 + jnp.exp(-jnp.abs(zz)))
            lk = jnp.where(cols < rows, -sp, 0.0) if diagonal else -sp
            z.append(zz)
            softplus.append(sp)
            log_keep.append(lk)
            parts.extend(_split(lk))
        sums = _dot(jnp.concatenate(parts, axis=0), later)
        out = []
        for hh, h in enumerate(heads):
            acc, run = carry[2 * hh], carry[2 * hh + 1]
            log_after = sums[2 * hh * t:(2 * hh + 1) * t] + sums[(2 * hh + 1) * t:(2 * hh + 2) * t] + run
            wgt = jnp.exp(z[hh] - softplus[hh] + log_after)
            if diagonal:
                wgt = jnp.where(cols < rows, wgt, 0.0)
            acc = acc + _dot(wgt.astype(BF16), v_ref[0, pl.ds(ks, t), h])
            run = log_after[:, :1] + log_keep[hh][:, :1]
            out += [acc, run]
        return tuple(out)

    zeros = (jnp.zeros((t, HEAD_DIM), F32), jnp.zeros((t, 1), F32)) * 2
    carry = block(pl.multiple_of(qi * t, t), zeros, True)
    carry = lax.fori_loop(
        0, qi, lambda it, c: block(pl.multiple_of((qi - 1 - it) * t, t), c, False), carry)
    o_ref[0] = jnp.concatenate([carry[0], carry[2]], axis=1).astype(o_ref.dtype)


def sb_attention(qkv, batch, seq):
    t = _pick(seq, (256, 128))
    x = qkv.reshape(batch, seq, 3 * WIDTH)
    hp = WIDTH // LANES
    out = pl.pallas_call(
        functools.partial(_sb_body, t=t),
        grid=(batch, hp, seq // t),
        in_specs=[pl.BlockSpec((1, t, LANES), lambda b, h, i: (b, i, h)),
                  pl.BlockSpec((1, seq, LANES), lambda b, h, i: (b, 0, hp + h)),
                  pl.BlockSpec((1, seq, LANES), lambda b, h, i: (b, 0, 2 * hp + h))],
        out_specs=pl.BlockSpec((1, t, LANES), lambda b, h, i: (b, i, h)),
        out_shape=jax.ShapeDtypeStruct((batch, seq, WIDTH), BF16),
        compiler_params=_params("parallel", "parallel", "parallel"),
        name="sb_attention",
    )(x, x, x)
    return out.reshape(batch * seq, WIDTH)


def _mla_body(q_ref, k_ref, v_ref, o_ref, *, t):
    qi = pl.program_id(2)
    rows = lax.broadcasted_iota(jnp.int32, (t, t), 0)
    cols = lax.broadcasted_iota(jnp.int32, (t, t), 1)
    q = [q_ref[0, :, hh * LANES:(hh + 1) * LANES] for hh in range(2)]

    def block(ks, carry, diagonal):
        out = []
        for hh in range(2):
            m, l, acc = carry[3 * hh:3 * hh + 3]
            s = _dot_nt(q[hh], k_ref[0, pl.ds(ks, t), hh * LANES:(hh + 1) * LANES])
            if diagonal:
                s = jnp.where(cols <= rows, s, -1e30)
            m_new = jnp.maximum(m, jnp.max(s, axis=-1, keepdims=True))
            p = jnp.exp(s - m_new)
            corr = jnp.exp(m - m_new)
            l = corr * l + jnp.sum(p, axis=-1, keepdims=True)
            v = v_ref[0, pl.ds(ks, t), hh * HEAD_DIM:(hh + 1) * HEAD_DIM]
            acc = corr * acc + _dot(p.astype(BF16), v)
            out += [m_new, l, acc]
        return tuple(out)

    init = (jnp.full((t, 1), -1e30, F32), jnp.zeros((t, 1), F32), jnp.zeros((t, HEAD_DIM), F32)) * 2
    carry = block(pl.multiple_of(qi * t, t), init, True)
    carry = lax.fori_loop(0, qi, lambda kb, c: block(pl.multiple_of(kb * t, t), c, False), carry)
    o_ref[0] = jnp.concatenate([carry[2] / carry[1], carry[5] / carry[4]], axis=1).astype(o_ref.dtype)


def mla_attention(q, k, v, batch, seq):
    t = _pick(seq, (256, 128))
    hp = HEADS // 2
    out = pl.pallas_call(
        functools.partial(_mla_body, t=t),
        grid=(batch, hp, seq // t),
        in_specs=[pl.BlockSpec((1, t, 2 * LANES), lambda b, h, i: (b, i, h)),
                  pl.BlockSpec((1, seq, 2 * LANES), lambda b, h, i: (b, 0, h)),
                  pl.BlockSpec((1, seq, LANES), lambda b, h, i: (b, 0, h))],
        out_specs=pl.BlockSpec((1, t, LANES), lambda b, h, i: (b, i, h)),
        out_shape=jax.ShapeDtypeStruct((batch, seq, WIDTH), BF16),
        compiler_params=_params("parallel", "parallel", "parallel"),
        name="mla_attention",
    )(q.reshape(batch, seq, HEADS * LANES), k.reshape(batch, seq, HEADS * LANES),
      v.reshape(batch, seq, WIDTH))
    return out.reshape(batch * seq, WIDTH)


RWKV_IN = 3 * WIDTH + 2 * RWKV_LORA_PAD + RWKV_GATE_LORA


def _rwkv_prep_body(*refs, tiles_per_seq, has_vres):
    if has_vres:
        (cols_ref, prev_ref, mu_ref, w0_ref, w2_ref, a0_ref, a2_ref, g2_ref, kk_ref, ka_ref,
         v0_ref, v1_ref, v2_ref, vfirst_ref,
         r_out, lw_out, k_out, v_out, kk_out, a_out, g_out) = refs
    else:
        (cols_ref, prev_ref, mu_ref, w0_ref, w2_ref, a0_ref, a2_ref, g2_ref, kk_ref, ka_ref,
         r_out, lw_out, k_out, v_out, kk_out, a_out, g_out) = refs
    x = cols_ref[...]
    starts_seq = (pl.program_id(0) % tiles_per_seq) == 0
    last_prev = jnp.where(starts_seq, 0.0, prev_ref[7:8, :])
    first_row = lax.broadcasted_iota(jnp.int32, x.shape, 0) == 0
    prev = jnp.where(first_row, last_prev, pltpu.roll(x, 1, 0))
    x = x + (prev - x) * mu_ref[...]
    r = x[:, :WIDTH]
    k = x[:, WIDTH:2 * WIDTH]
    v = x[:, 2 * WIDTH:3 * WIDTH]
    o = 3 * WIDTH
    wd = x[:, o:o + RWKV_LORA_PAD]
    ad = x[:, o + RWKV_LORA_PAD:o + 2 * RWKV_LORA_PAD]
    gd = x[:, o + 2 * RWKV_LORA_PAD:]
    wpre = w0_ref[...] + _dot(jnp.tanh(wd).astype(BF16), w2_ref[...])
    w = -jax.nn.softplus(-wpre) - 0.5
    lw_out[...] = -jnp.exp(w)
    a = jax.nn.sigmoid(a0_ref[...] + _dot(ad.astype(BF16), a2_ref[...]))
    g_out[...] = _dot(jax.nn.sigmoid(gd).astype(BF16), g2_ref[...])
    if has_vres:
        low = _dot(v.astype(BF16), v1_ref[...])
        mix = jax.nn.sigmoid(v0_ref[...] + _dot(low.astype(BF16), v2_ref[...]))
        v = v + (vfirst_ref[...] - v) * mix
    r_out[...] = r
    v_out[...] = v
    kk_out[...] = k * kk_ref[...]
    k_out[...] = k * (1.0 + (a - 1.0) * ka_ref[...])
    a_out[...] = a


def rwkv_prep(cols, seq, mu, w0, w2, a0, a2, g2, k_k, k_a, vres):
    m = cols.shape[0]
    tm = _pick(seq, (256, 128, 64, 32, 16, 8))
    row = lambda t: t.reshape(1, -1).astype(F32)
    args = [cols, cols, row(mu), row(w0), w2, row(a0), a2, g2, row(k_k), row(k_a)]
    full = lambda arr: pl.BlockSpec(arr.shape, lambda i: (0, 0))
    tile = pl.BlockSpec((tm, WIDTH), lambda i: (i, 0))
    in_specs = [pl.BlockSpec((tm, RWKV_IN), lambda i: (i, 0)),
                pl.BlockSpec((8, RWKV_IN), lambda i: (jnp.maximum(i * (tm // 8) - 1, 0), 0))]
    in_specs += [full(t) for t in args[2:]]
    if vres is not None:
        v0, v1, v2, v_first = vres
        extra = [row(v0), v1, v2]
        args += extra + [v_first]
        in_specs += [full(t) for t in extra] + [tile]
    return pl.pallas_call(
        functools.partial(_rwkv_prep_body, tiles_per_seq=seq // tm, has_vres=vres is not None),
        grid=(m // tm,),
        in_specs=in_specs,
        out_specs=[tile] * 7,
        out_shape=[jax.ShapeDtypeStruct((m, WIDTH), F32)] * 7,
        compiler_params=_params("parallel"),
        name="rwkv_prep",
    )(*args)


def _halves(x):
    return x[:, :HEAD_DIM], x[:, HEAD_DIM:]


def _head_sum(x, left):
    s0 = jnp.sum(jnp.where(left, x, 0.0), axis=-1, keepdims=True)
    s1 = jnp.sum(jnp.where(left, 0.0, x), axis=-1, keepdims=True)
    return jnp.where(left, s0, s1)


def _dot3(x, y, dot):
    xh, xl = _split(x)
    yh, yl = _split(y)
    return dot(xh, yh) + (dot(xh, yl) + dot(xl, yh))


def _rwkv_chunk_terms(r, lw, k, v, kk_raw, a):
    c = r.shape[0]
    left = lax.broadcasted_iota(jnp.int32, (c, LANES), 1) < HEAD_DIM
    trow = lax.broadcasted_iota(jnp.int32, (c, LANES), 0)
    kk = kk_raw / jnp.maximum(jnp.sqrt(_head_sum(kk_raw * kk_raw, left)), 1e-12)
    beta = kk * a
    cl = lw
    shift = 1
    while shift < c:
        cl = cl + jnp.where(trow >= shift, pltpu.roll(cl, shift, 0), 0.0)
        shift *= 2
    a_t = -kk * jnp.exp(cl - lw)
    r_t = r * jnp.exp(cl)
    inv = jnp.exp(-cl)
    cl_end = cl[c - 1:c, :]
    tail = jnp.exp(cl_end - cl)
    full = (a_t, r_t, beta * inv, k * inv, v, beta * tail, k * tail, jnp.exp(cl_end))
    return [tuple(_halves(x)[hh] for x in full) for hh in range(2)]


def _rwkv_masks(c):
    ti = lax.broadcasted_iota(jnp.int32, (c, c), 0)
    tj = lax.broadcasted_iota(jnp.int32, (c, c), 1)
    merges = []
    half = 1
    while half < c:
        merges.append((ti // (2 * half) == tj // (2 * half)) & (ti % (2 * half) >= half)
                      & (tj % (2 * half) < half))
        half *= 2
    return ti > tj, ti >= tj, ti == tj, merges


def _rwkv_chain(out, head_terms, masks):
    at_h, rt_h, bt_h, kt_h, v_h, bbar_h, kbar_h, gend_h = head_terms
    strict, incl, eye, merges = masks
    c = at_h.shape[0]
    ar = jnp.concatenate([at_h, rt_h], axis=0).astype(BF16)
    bk = jnp.concatenate([bt_h, kt_h], axis=0).astype(BF16)
    pair = _dot_nt(ar, bk)
    yield
    lower = jnp.where(strict, pair[:c, :c], 0.0)
    akv = _dot(jnp.where(strict, pair[:c, c:], 0.0).astype(BF16), v_h.astype(BF16))
    tinv = eye.astype(F32) + jnp.where(merges[0], lower, 0.0)
    for off in merges[1:]:
        tb = tinv.astype(BF16)
        step = _dot(tb, jnp.where(off, lower, 0.0).astype(BF16))
        yield
        tinv = tinv + _dot(step.astype(BF16), tb)
        yield
    solved = _dot(tinv.astype(BF16), jnp.concatenate([at_h, akv], axis=1).astype(BF16))
    yield
    ta, w = _halves(solved)
    rb = jnp.where(incl, pair[c:, :c], 0.0).astype(BF16)
    rk = jnp.where(incl, pair[c:, c:], 0.0).astype(BF16)
    wv = jnp.concatenate([w, v_h], axis=0).astype(BF16)
    q = rt_h + _dot(rb, ta.astype(BF16))
    y0 = _dot(jnp.concatenate([rb, rk], axis=1), wv)
    g = jnp.where(eye, gend_h, 0.0) + _dot_tn(ta.astype(BF16), bbar_h.astype(BF16))
    d = _dot_tn(wv, jnp.concatenate([bbar_h, kbar_h], axis=0).astype(BF16))
    out.append((q, y0, g, d))


def _lockstep(chains):
    live = list(chains)
    while live:
        still = []
        for ch in live:
            try:
                next(ch)
                still.append(ch)
            except StopIteration:
                pass
        live = still


def _rwkv_scan_body(r_ref, lw_ref, k_ref, v_ref, kk_ref, a_ref, g_ref, lng_ref, lnb_ref, rk_ref,
                    o_ref, state_ref, *, chunk):
    @pl.when(pl.program_id(2) == 0)
    def _():
        state_ref[...] = jnp.zeros_like(state_ref)

    n_chunks = r_ref.shape[1] // chunk
    left = lax.broadcasted_iota(jnp.int32, (chunk, LANES), 1) < HEAD_DIM
    masks = _rwkv_masks(chunk)
    rows = [slice(ci * chunk, (ci + 1) * chunk) for ci in range(n_chunks)]
    results = [[[], []] for _ in range(n_chunks)]
    chains = []
    for ci in range(n_chunks):
        terms = _rwkv_chunk_terms(r_ref[0, rows[ci], :], lw_ref[0, rows[ci], :], k_ref[0, rows[ci], :],
                                  v_ref[0, rows[ci], :], kk_ref[0, rows[ci], :], a_ref[0, rows[ci], :])
        chains += [_rwkv_chain(results[ci][hh], terms[hh], masks) for hh in range(2)]
    _lockstep(chains)

    state = [state_ref[0], state_ref[1]]
    for ci in range(n_chunks):
        ys = []
        for hh in range(2):
            q, y0, g, d = results[ci][hh][0]
            ys.append(_dot3(q, state[hh], _dot_nt) + y0)
            state[hh] = _dot3(state[hh], g, _dot) + d
        y = jnp.concatenate(ys, axis=1)
        r, k, v = r_ref[0, rows[ci], :], k_ref[0, rows[ci], :], v_ref[0, rows[ci], :]
        mean = _head_sum(y, left) * (1.0 / HEAD_DIM)
        var = _head_sum(jnp.square(y - mean), left) * (1.0 / HEAD_DIM)
        y = (y - mean) * lax.rsqrt(var + RWKV_GN_EPS) * lng_ref[...] + lnb_ref[...]
        bonus = _head_sum(r * k * rk_ref[...], left) * v
        o_ref[0, rows[ci], :] = ((y + bonus) * g_ref[0, rows[ci], :]).astype(o_ref.dtype)
    state_ref[0] = state[0]
    state_ref[1] = state[1]


def rwkv_scan(r, lw, k, v, kk, a, g, ln_g, ln_b, r_k, batch, seq):
    chunk = min(RWKV_CHUNK, seq)
    rows = _pick(seq, (256, 128, 64, 32, 16, 8))
    hp = WIDTH // LANES
    seq_spec = pl.BlockSpec((1, rows, LANES), lambda b, h, c: (b, c, h))
    par_spec = pl.BlockSpec((1, LANES), lambda b, h, c: (0, h))
    shaped = [t.reshape(batch, seq, WIDTH) for t in (r, lw, k, v, kk, a, g)]
    pars = [t.reshape(1, WIDTH).astype(F32) for t in (ln_g, ln_b, r_k)]
    out = pl.pallas_call(
        functools.partial(_rwkv_scan_body, chunk=chunk),
        grid=(batch, hp, seq // rows),
        in_specs=[seq_spec] * 7 + [par_spec] * 3,
        out_specs=seq_spec,
        out_shape=jax.ShapeDtypeStruct((batch, seq, WIDTH), BF16),
        scratch_shapes=[pltpu.VMEM((2, HEAD_DIM, HEAD_DIM), F32)],
        compiler_params=_params("parallel", "parallel", "arbitrary"),
        name="rwkv_scan",
    )(*shaped, *pars)
    return out.reshape(batch * seq, WIDTH)


def _merge_body(a1, a2, a3, w1, w2, w3, g1, g2, g3, o_ref):
    acc = g1[...] * _dot(a1[...], w1[...])
    acc += g2[...] * _dot(a2[...], w2[...])
    acc += g3[...] * _dot(a3[...], w3[...])
    o_ref[...] = acc.astype(o_ref.dtype)


def merge_branches(outs, weights, gates):
    m = outs[0].shape[0]
    d = weights[0].shape[1]
    tm = _pick(m, (512, 256, 128, 64, 32, 16, 8))
    tn = _pick(d, (512, 256, 128))
    nb = d // tn
    a_spec = pl.BlockSpec((tm, WIDTH), lambda i, j: (i, 0))
    w_spec = pl.BlockSpec((WIDTH, tn), lambda i, j: (0, j))
    g_specs = [pl.BlockSpec((tm, tn), lambda i, j, b=b: (i, j + b * nb)) for b in range(3)]
    return pl.pallas_call(
        _merge_body,
        grid=(m // tm, nb),
        in_specs=[a_spec] * 3 + [w_spec] * 3 + g_specs,
        out_specs=pl.BlockSpec((tm, tn), lambda i, j: (i, j)),
        out_shape=jax.ShapeDtypeStruct((m, d), BF16),
        compiler_params=_params("parallel", "parallel"),
        name="merge_branches",
    )(*outs, *weights, gates, gates, gates)


ROUTER_PAD = 128


def _slab_cols(ref, n, slabs, first=0, pitch=None):
    pitch = slabs if pitch is None else pitch
    return jnp.concatenate([ref[pl.ds(first + s, n, stride=pitch), :] for s in range(slabs)], axis=1)


def _store_slabs(ref, value, slabs):
    n = value.shape[0]
    for s in range(slabs):
        ref[pl.ds(s, n, stride=slabs), :] = value[:, s * LANES:(s + 1) * LANES].astype(ref.dtype)


def _router_body(x_ref, g_ref, w_ref, b_ref, h_ref, logit_ref, *, slabs):
    x = x_ref[...]
    y = x * lax.rsqrt(jnp.mean(x * x, axis=-1, keepdims=True) + NORM_EPS)
    h = y * g_ref[...]
    _store_slabs(h_ref, h, slabs)
    logit_ref[...] = jnp.dot(h, w_ref[...], preferred_element_type=F32, precision=HIGHEST) + b_ref[...]


def router(x, g, w_pad, b_pad):
    m, d = x.shape
    slabs = d // LANES
    tm = _pick(m, (256, 128, 64, 32, 16, 8))
    return pl.pallas_call(
        functools.partial(_router_body, slabs=slabs),
        grid=(m // tm,),
        in_specs=[pl.BlockSpec((tm, d), lambda i: (i, 0)),
                  pl.BlockSpec((1, d), lambda i: (0, 0)),
                  pl.BlockSpec((d, ROUTER_PAD), lambda i: (0, 0)),
                  pl.BlockSpec((1, ROUTER_PAD), lambda i: (0, 0))],
        out_specs=[pl.BlockSpec((tm * slabs, LANES), lambda i: (i, 0)),
                   pl.BlockSpec((tm, ROUTER_PAD), lambda i: (i, 0))],
        out_shape=[jax.ShapeDtypeStruct((m * slabs, LANES), F32),
                   jax.ShapeDtypeStruct((m, ROUTER_PAD), F32)],
        compiler_params=_params("parallel"),
        name="ffn_norm_router",
    )(x, g.reshape(1, d).astype(F32), w_pad, b_pad)


def _row_gather(idx_ref, first, n, src_ref, buf_ref, sem, slabs):
    def copy(r):
        src = pl.multiple_of(idx_ref[first + r] * slabs, slabs)
        return pltpu.make_async_copy(src_ref.at[pl.ds(src, slabs)],
                                     buf_ref.at[pl.ds(r * slabs, slabs)], sem)

    def start():
        lax.fori_loop(0, n, lambda r, c: (copy(r).start(), c)[1], 0, unroll=8)

    def wait():
        lax.fori_loop(0, n, lambda r, c: (copy(r).wait(), c)[1], 0, unroll=8)

    return start, wait


def _expert_body(be_ref, nb_ref, tok_ref, h_ref, wg_ref, wu_ref, wd_ref, o_ref,
                 xbuf, wg_s, wu_s, wd_s, sem, *, slabs):
    i = pl.program_id(0)
    n_blocks = nb_ref[0]
    slot = i % 2

    def gather(block, slot):
        return _row_gather(tok_ref, block * MOE_ROWS, MOE_ROWS, h_ref, xbuf.at[slot], sem.at[slot], slabs)

    @pl.when(i == 0)
    def _():
        gather(0, 0)[0]()

    @pl.when(i + 1 < n_blocks)
    def _():
        gather(i + 1, 1 - slot)[0]()

    e = be_ref[i]

    @pl.when((i == 0) | (e != be_ref[jnp.maximum(i - 1, 0)]))
    def _():
        wg_s[...] = wg_ref[0].astype(BF16)
        wu_s[...] = wu_ref[0].astype(BF16)
        wd_s[...] = wd_ref[0].astype(BF16)

    @pl.when(i < n_blocks)
    def _():
        gather(i, slot)[1]()
        x = _slab_cols(xbuf.at[slot], MOE_ROWS, slabs).astype(BF16)
        gate = _dot(x, wg_s[...])
        up = _dot(x, wu_s[...])
        mid = (jax.nn.silu(gate) * up).astype(BF16)
        _store_slabs(o_ref, _dot(mid, wd_s[...]), slabs)

    @pl.when(i >= n_blocks)
    def _():
        o_ref[...] = jnp.zeros_like(o_ref)


def expert_ffn(h_rows, row_tok, block_e, n_blocks, w_gate, w_up, w_down):
    n_exp, d, ff = w_gate.shape
    slabs = d // LANES
    p = row_tok.shape[0]
    return pl.pallas_call(
        functools.partial(_expert_body, slabs=slabs),
        grid_spec=pltpu.PrefetchScalarGridSpec(
            num_scalar_prefetch=3,
            grid=(p // MOE_ROWS,),
            in_specs=[pl.BlockSpec(memory_space=pl.ANY),
                      pl.BlockSpec((1, d, ff), lambda i, be, nb, tok: (be[i], 0, 0)),
                      pl.BlockSpec((1, d, ff), lambda i, be, nb, tok: (be[i], 0, 0)),
                      pl.BlockSpec((1, ff, d), lambda i, be, nb, tok: (be[i], 0, 0))],
            out_specs=pl.BlockSpec((MOE_ROWS * slabs, LANES), lambda i, be, nb, tok: (i, 0)),
            scratch_shapes=[pltpu.VMEM((2, MOE_ROWS * slabs, LANES), F32),
                            pltpu.VMEM((d, ff), BF16), pltpu.VMEM((d, ff), BF16),
                            pltpu.VMEM((ff, d), BF16),
                            pltpu.SemaphoreType.DMA((2,))]),
        out_shape=jax.ShapeDtypeStruct((p * slabs, LANES), F32),
        compiler_params=_params("arbitrary"),
        name="expert_ffn",
    )(block_e, n_blocks, row_tok, h_rows, w_gate, w_up, w_down)


def _combine_body(dest_ref, x_ref, gate_ref, y_ref, o_ref, ybuf, sem, *, tm, slabs):
    i = pl.program_id(0)
    slot = i % 2

    def gather(step, slot):
        return _row_gather(dest_ref, step * tm * TOP_K, tm * TOP_K, y_ref, ybuf.at[slot], sem.at[slot], slabs)

    @pl.when(i == 0)
    def _():
        gather(0, 0)[0]()

    @pl.when(i + 1 < pl.num_programs(0))
    def _():
        gather(i + 1, 1 - slot)[0]()

    gather(i, slot)[1]()
    buf = ybuf.at[slot]
    gate = gate_ref[...]
    g0, g1 = gate[:, 0:1], gate[:, 1:2]
    for s in range(slabs):
        y0 = buf[pl.ds(s, tm, stride=TOP_K * slabs), :]
        y1 = buf[pl.ds(slabs + s, tm, stride=TOP_K * slabs), :]
        cols = slice(s * LANES, (s + 1) * LANES)
        o_ref[:, cols] = x_ref[:, cols] + (g0 * y0 + g1 * y1)


def moe_combine(x, y_rows, dest, gate):
    m, d = x.shape
    slabs = d // LANES
    tm = _pick(m, (256, 128, 64, 32, 16, 8))
    tile = lambda i, dest: (i, 0)
    return pl.pallas_call(
        functools.partial(_combine_body, tm=tm, slabs=slabs),
        grid_spec=pltpu.PrefetchScalarGridSpec(
            num_scalar_prefetch=1,
            grid=(m // tm,),
            in_specs=[pl.BlockSpec((tm, d), tile), pl.BlockSpec((tm, TOP_K), tile),
                      pl.BlockSpec(memory_space=pl.ANY)],
            out_specs=pl.BlockSpec((tm, d), tile),
            scratch_shapes=[pltpu.VMEM((2, tm * TOP_K * slabs, LANES), F32),
                            pltpu.SemaphoreType.DMA((2,))]),
        out_shape=jax.ShapeDtypeStruct((m, d), F32),
        compiler_params=_params("arbitrary"),
        name="moe_combine",
    )(dest.reshape(m * TOP_K), x, gate, y_rows)


def _routing_tables(logits):
    t = logits.shape[0]
    n_assign = t * TOP_K
    tok = jnp.arange(t)
    g_logits = logits[:, :N_GROUPS]
    grp = jnp.argmax(g_logits, axis=-1)
    p_grp = jax.nn.softmax(g_logits, axis=-1)[tok, grp]
    e_logits = logits[:, N_GROUPS:N_GROUPS + N_EXPERTS].reshape(t, N_GROUPS, EXPERTS_PER_GROUP)
    p_in, idx_in = lax.top_k(jax.nn.softmax(e_logits[tok, grp], axis=-1), TOP_K)
    gate = p_grp[:, None] * p_in / jnp.sum(p_in, axis=-1, keepdims=True)
    flat_e = (grp[:, None] * EXPERTS_PER_GROUP + idx_in).reshape(n_assign).astype(jnp.int32)
    onehot = (flat_e[:, None] == jnp.arange(N_EXPERTS, dtype=jnp.int32)[None, :]).astype(jnp.int32)
    rank = jnp.sum((jnp.cumsum(onehot, axis=0) - onehot) * onehot, axis=1)
    counts = jnp.sum(onehot, axis=0)
    padded = (counts + MOE_ROWS - 1) // MOE_ROWS * MOE_ROWS
    pend = jnp.cumsum(padded)
    dest = ((pend - padded)[flat_e] + rank).astype(jnp.int32)
    n_blocks_max = -(-n_assign // MOE_ROWS) + N_EXPERTS
    row_tok = jnp.zeros((n_blocks_max * MOE_ROWS,), jnp.int32).at[dest].set(
        jnp.repeat(tok, TOP_K).astype(jnp.int32))
    block_e = jnp.minimum(
        jnp.searchsorted(pend, jnp.arange(n_blocks_max) * MOE_ROWS, side="right"),
        N_EXPERTS - 1).astype(jnp.int32)
    n_blocks = (pend[-1:] // MOE_ROWS).astype(jnp.int32)
    return gate.astype(F32), dest.reshape(t, TOP_K), row_tok, block_e, n_blocks


def hier_moe_residual(x, norm_g, rg_w, rg_b, re_w, re_b, w_gate, w_up, w_down):
    t, d = x.shape
    w_pad = jnp.zeros((d, ROUTER_PAD), F32).at[:, :N_GROUPS].set(rg_w)
    w_pad = w_pad.at[:, N_GROUPS:N_GROUPS + N_EXPERTS].set(re_w)
    b_pad = jnp.zeros((1, ROUTER_PAD), F32).at[0, :N_GROUPS].set(rg_b)
    b_pad = b_pad.at[0, N_GROUPS:N_GROUPS + N_EXPERTS].set(re_b)
    h_rows, logits = router(x, norm_g, w_pad, b_pad)
    gate, dest, row_tok, block_e, n_blocks = _routing_tables(logits)
    y_rows = expert_ffn(h_rows, row_tok, block_e, n_blocks, w_gate, w_up, w_down)
    return moe_combine(x, y_rows, dest, gate)


def _mla_q_weight(w_uq):
    w = w_uq.reshape(MLA_Q_LORA, HEADS, HEAD_DIM + MLA_ROPE_DIM)
    w = jnp.pad(w, ((0, 0), (0, 0), (0, LANES - HEAD_DIM - MLA_ROPE_DIM)))
    return w.reshape(MLA_Q_LORA, HEADS * LANES).astype(BF16)


def _mla_kv_weights(w_ukv):
    w = w_ukv.reshape(MLA_KV_LORA, HEADS, 2 * HEAD_DIM)
    wk = jnp.pad(w[:, :, :HEAD_DIM], ((0, 0), (0, 0), (0, LANES - HEAD_DIM)))
    wv = w[:, :, HEAD_DIM:]
    return (wk.reshape(MLA_KV_LORA, HEADS * LANES).astype(BF16),
            wv.reshape(MLA_KV_LORA, WIDTH).astype(BF16))


def _rope_slot_tables(positions):
    half = MLA_ROPE_DIM // 2
    inv_freq = ROPE_THETA ** (-jnp.arange(0, MLA_ROPE_DIM, 2, dtype=F32) / MLA_ROPE_DIM)
    ang = positions.astype(F32).reshape(-1, 1) * inv_freq[None, :]
    cos, sin = jnp.cos(ang), jnp.sin(ang)
    t = ang.shape[0]
    pad = jnp.zeros((t, LANES - HEAD_DIM - 2 * half), F32)
    cos_t = jnp.concatenate([jnp.ones((t, HEAD_DIM), F32), cos, cos, pad], axis=1)
    sin_t = jnp.concatenate([jnp.zeros((t, HEAD_DIM), F32), -sin, sin, pad], axis=1)
    return cos_t, sin_t


def _mixer_block(x, l, batch, seq, cos_t, sin_t, v_first, p):
    t, d = x.shape
    h = rmsnorm(x, p["attn_norm_g"][l], BF16, name="attn_norm")
    w_in = p["w_in"][l]
    o_rwkv = 3 * WIDTH
    n_rwkv = 3 * WIDTH + RWKV_DECAY_LORA + RWKV_ICLR_LORA + RWKV_GATE_LORA
    o_mla = o_rwkv + n_rwkv
    n_mla = MLA_Q_LORA + MLA_KV_LORA + MLA_ROPE_DIM
    o_gate = o_mla + n_mla

    qscale = jnp.concatenate([jnp.full((1, WIDTH), HEAD_DIM ** -0.5, F32), jnp.ones((1, 2 * WIDTH), F32)], axis=1)
    (qkv,) = matmul(h, w_in[:, :o_rwkv].astype(BF16), [BF16],
                    epilogue=lambda acc, s: (acc * s,), extras=[(qscale, "col", 0)], name="proj_sb")
    o_sb = sb_attention(qkv, batch, seq)

    wr = w_in[:, o_rwkv:o_mla]
    zpad = jnp.zeros((d, RWKV_LORA_PAD - RWKV_DECAY_LORA), F32)
    c0 = 3 * WIDTH
    w_rwkv = jnp.concatenate([wr[:, :c0], wr[:, c0:c0 + RWKV_DECAY_LORA], zpad,
                              wr[:, c0 + RWKV_DECAY_LORA:c0 + 2 * RWKV_DECAY_LORA], zpad,
                              wr[:, c0 + 2 * RWKV_DECAY_LORA:]], axis=1).astype(BF16)
    (cols,) = matmul(h, w_rwkv, [F32], name="proj_rwkv")
    mu = p["rwkv_mu"][l]
    mpad = jnp.zeros((RWKV_LORA_PAD - RWKV_DECAY_LORA,), F32)
    mu_pad = jnp.concatenate([mu[:c0], mu[c0:c0 + RWKV_DECAY_LORA], mpad,
                              mu[c0 + RWKV_DECAY_LORA:c0 + 2 * RWKV_DECAY_LORA], mpad,
                              mu[c0 + 2 * RWKV_DECAY_LORA:]])
    rpad = ((0, RWKV_LORA_PAD - RWKV_DECAY_LORA), (0, 0))
    vres = None
    if l > 0:
        vres = (p["rwkv_v0"][l - 1], p["rwkv_v1"][l - 1].astype(BF16), p["rwkv_v2"][l - 1].astype(BF16), v_first)
    r, lw, k, v, kk, a, g = rwkv_prep(
        cols, seq, mu_pad, p["rwkv_w0"][l], jnp.pad(p["rwkv_w2"][l], rpad).astype(BF16),
        p["rwkv_a0"][l], jnp.pad(p["rwkv_a2"][l], rpad).astype(BF16), p["rwkv_g2"][l].astype(BF16),
        p["rwkv_k_k"][l], p["rwkv_k_a"][l], vres)
    if l == 0:
        v_first = v
    o_rw = rwkv_scan(r, lw, k, v, kk, a, g, p["rwkv_ln_g"][l], p["rwkv_ln_b"][l], p["rwkv_r_k"][l],
                     batch, seq)

    wm = w_in[:, o_mla:o_gate]
    w_mla = jnp.concatenate([wm[:, :MLA_Q_LORA + MLA_KV_LORA], jnp.zeros((d, HEAD_DIM), F32),
                             wm[:, MLA_Q_LORA + MLA_KV_LORA:],
                             jnp.zeros((d, LANES - HEAD_DIM - MLA_ROPE_DIM), F32)], axis=1).astype(BF16)
    (mcols,) = matmul(h, w_mla, [F32], name="proj_mla")
    cq = rmsnorm(mcols, p["mla_q_norm_g"][l], BF16, width=MLA_Q_LORA, col_block=0, name="mla_q_norm")
    ckv = rmsnorm(mcols, p["mla_kv_norm_g"][l], BF16, width=MLA_KV_LORA,
                  col_block=MLA_Q_LORA // MLA_KV_LORA, name="mla_kv_norm")
    mla_scale = (HEAD_DIM + MLA_ROPE_DIM) ** -0.5
    (q_mla,) = matmul(cq, _mla_q_weight(p["mla_w_uq"][l]), [BF16],
                      epilogue=lambda acc, c, s: (_rope_slots(acc, c, s) * mla_scale,),
                      extras=[(cos_t, "row", 0), (sin_t, "row", 0)], name="mla_q_up")
    wk, wv = _mla_kv_weights(p["mla_w_ukv"][l])
    kpe_block = (MLA_Q_LORA + MLA_KV_LORA) // LANES

    def k_epilogue(acc, kpe, c, s):
        kr = _rope_slots(kpe, c, s)
        return (acc + jnp.concatenate([kr] * (acc.shape[1] // LANES), axis=1),)

    kpe = mcols[:, kpe_block * LANES:(kpe_block + 1) * LANES]
    (k_mla,) = matmul(ckv, wk, [BF16], epilogue=k_epilogue,
                      extras=[(kpe, "row", 0), (cos_t, "row", 0), (sin_t, "row", 0)], name="mla_k_up")
    (v_mla,) = matmul(ckv, wv, [BF16], name="mla_v_up")
    o_mla_out = mla_attention(q_mla, k_mla, v_mla, batch, seq)

    (gates,) = matmul(h, w_in[:, o_gate:].astype(BF16), [F32],
                      epilogue=lambda acc: (jax.nn.sigmoid(acc),), name="proj_gates")
    merged = merge_branches([o_sb, o_rw, o_mla_out],
                            [p["w_br_sb"][l].astype(BF16), p["w_br_rwkv"][l].astype(BF16),
                             p["w_br_mla"][l].astype(BF16)], gates)
    (x,) = matmul(merged, p["w_out"][l].astype(BF16), [F32],
                  epilogue=lambda acc, res: (res + acc,), extras=[(x, "tile", 0)], name="out_proj")
    return x, v_first


def kernel(x, positions, attn_norm_g, w_in, rwkv_mu, rwkv_w0, rwkv_w2, rwkv_a0, rwkv_a2, rwkv_g2, rwkv_k_k, rwkv_k_a, rwkv_r_k, rwkv_ln_g, rwkv_ln_b, rwkv_v0, rwkv_v1, rwkv_v2, mla_q_norm_g, mla_w_uq, mla_kv_norm_g, mla_w_ukv, w_br_sb, w_br_rwkv, w_br_mla, w_out, ffn_norm_g, router_group_w, router_group_b, router_expert_w, router_expert_b, expert_w_gate, expert_w_up, expert_w_down, final_norm_g):
    p = dict(attn_norm_g=attn_norm_g, w_in=w_in, rwkv_mu=rwkv_mu, rwkv_w0=rwkv_w0, rwkv_w2=rwkv_w2,
             rwkv_a0=rwkv_a0, rwkv_a2=rwkv_a2, rwkv_g2=rwkv_g2, rwkv_k_k=rwkv_k_k, rwkv_k_a=rwkv_k_a,
             rwkv_r_k=rwkv_r_k, rwkv_ln_g=rwkv_ln_g, rwkv_ln_b=rwkv_ln_b, rwkv_v0=rwkv_v0,
             rwkv_v1=rwkv_v1, rwkv_v2=rwkv_v2, mla_q_norm_g=mla_q_norm_g, mla_w_uq=mla_w_uq,
             mla_kv_norm_g=mla_kv_norm_g, mla_w_ukv=mla_w_ukv, w_br_sb=w_br_sb, w_br_rwkv=w_br_rwkv,
             w_br_mla=w_br_mla, w_out=w_out)
    batch, seq, d = x.shape
    depth = w_in.shape[0]
    cos_t, sin_t = _rope_slot_tables(positions)
    xt = x.reshape(batch * seq, d)
    v_first = None
    for l in range(depth):
        xt, v_first = _mixer_block(xt, l, batch, seq, cos_t, sin_t, v_first, p)
        xt = hier_moe_residual(xt, ffn_norm_g[l], router_group_w[l], router_group_b[l],
                               router_expert_w[l], router_expert_b[l],
                               expert_w_gate[l], expert_w_up[l], expert_w_down[l])
    return rmsnorm(xt, final_norm_g, x.dtype, name="final_norm").reshape(batch, seq, d)
```

```python
import functools

import jax
import jax.numpy as jnp
from jax import lax
from jax.experimental import pallas as pl
from jax.experimental.pallas import tpu as pltpu

F32 = jnp.float32
BF16 = jnp.bfloat16
HIGHEST = lax.Precision.HIGHEST

NORM_EPS = 1e-6
HEADS = 16
HEAD_DIM = 64
WIDTH = HEADS * HEAD_DIM
LANES = 128
RWKV_DECAY_LORA = 96
RWKV_ICLR_LORA = 96
RWKV_GATE_LORA = 256
RWKV_LORA_PAD = 128
RWKV_GN_EPS = 64e-5
RWKV_CHUNK = 64
MLA_Q_LORA = 768
MLA_KV_LORA = 256
MLA_ROPE_DIM = 32
ROPE_THETA = 10000.0
N_GROUPS = 8
EXPERTS_PER_GROUP = 8
N_EXPERTS = N_GROUPS * EXPERTS_PER_GROUP
TOP_K = 2
MOE_ROWS = 256
VMEM_LIMIT = 48 * 1024 * 1024


def _params(*sem):
    return pltpu.CompilerParams(dimension_semantics=sem, vmem_limit_bytes=VMEM_LIMIT)


def _pick(n, cands):
    for c in cands:
        if n % c == 0:
            return c
    raise ValueError(f"no tile for {n} in {cands}")


def _dot(x, y):
    return jnp.dot(x, y, preferred_element_type=F32)


def _dot_nt(x, y):
    return lax.dot_general(x, y, (((1,), (1,)), ((), ())), preferred_element_type=F32)


def _dot_tn(x, y):
    return lax.dot_general(x, y, (((0,), (0,)), ((), ())), preferred_element_type=F32)


def _split(x):
    hi = x.astype(BF16)
    return hi, (x - hi.astype(F32)).astype(BF16)


def _rmsnorm_body(x_ref, g_ref, o_ref):
    x = x_ref[...].astype(F32)
    y = x * lax.rsqrt(jnp.mean(x * x, axis=-1, keepdims=True) + NORM_EPS)
    o_ref[...] = (y * g_ref[...]).astype(o_ref.dtype)


def rmsnorm(x, g, out_dtype, *, width=None, col_block=0, name="rmsnorm"):
    m = x.shape[0]
    width = x.shape[1] if width is None else width
    tm = _pick(m, (512, 256, 128, 64, 32, 16, 8))
    return pl.pallas_call(
        _rmsnorm_body,
        grid=(m // tm,),
        in_specs=[pl.BlockSpec((tm, width), lambda i: (i, col_block)),
                  pl.BlockSpec((1, width), lambda i: (0, 0))],
        out_specs=pl.BlockSpec((tm, width), lambda i: (i, 0)),
        out_shape=jax.ShapeDtypeStruct((m, width), out_dtype),
        compiler_params=_params("parallel"),
        name=name,
    )(x, g.reshape(1, width).astype(F32))


def _mm_body(*refs, n_extra, epilogue):
    a_ref, w_ref = refs[:2]
    extra = [r[...] for r in refs[2:2 + n_extra]]
    outs = refs[2 + n_extra:]
    acc = _dot(a_ref[...], w_ref[...])
    res = epilogue(acc, *extra)
    for o, r in zip(outs, res):
        o[...] = r.astype(o.dtype)


def matmul(a, w, out_dtypes, *, epilogue=None, extras=(), a_col_block=0, name="matmul"):
    m = a.shape[0]
    k, n = w.shape
    tm = _pick(m, (512, 256, 128, 64, 32, 16, 8))
    tn = n if n <= 1280 else _pick(n, (512, 256, 128))
    if epilogue is None:
        epilogue = lambda acc: (acc,)
    in_specs = [pl.BlockSpec((tm, k), lambda i, j: (i, a_col_block)),
                pl.BlockSpec((k, tn), lambda i, j: (0, j))]
    args = [a, w]
    for arr, kind, off in extras:
        if kind == "row":
            in_specs.append(pl.BlockSpec((tm, arr.shape[1]), lambda i, j: (i, 0)))
        elif kind == "col":
            in_specs.append(pl.BlockSpec((1, tn), lambda i, j: (0, j)))
        else:
            in_specs.append(pl.BlockSpec((tm, tn), lambda i, j, off=off: (i, j + off)))
        args.append(arr)
    return pl.pallas_call(
        functools.partial(_mm_body, n_extra=len(extras), epilogue=epilogue),
        grid=(m // tm, n // tn),
        in_specs=in_specs,
        out_specs=[pl.BlockSpec((tm, tn), lambda i, j: (i, j)) for _ in out_dtypes],
        out_shape=[jax.ShapeDtypeStruct((m, n), dt) for dt in out_dtypes],
        compiler_params=_params("parallel", "parallel"),
        name=name,
    )(*args)


def _rope_slots(x, cos_t, sin_t):
    n = x.shape[-1]
    reps = n // LANES
    lane = lax.broadcasted_iota(jnp.int32, x.shape, 1) % LANES
    from_hi = pltpu.roll(x, n - MLA_ROPE_DIM // 2, 1)
    from_lo = pltpu.roll(x, MLA_ROPE_DIM // 2, 1)
    swapped = jnp.where(lane < HEAD_DIM + MLA_ROPE_DIM // 2, from_hi, from_lo)
    if reps > 1:
        cos_t = jnp.concatenate([cos_t] * reps, axis=1)
        sin_t = jnp.concatenate([sin_t] * reps, axis=1)
    return x * cos_t + swapped * sin_t


def _sb_body(q_ref, k_ref, v_ref, o_ref, *, t, cw):
    qi = pl.program_id(2)
    n_sub = t // cw
    rows = lax.broadcasted_iota(jnp.int32, (t, cw), 0)
    cols = lax.broadcasted_iota(jnp.int32, (t, cw), 1)
    later = (lax.broadcasted_iota(jnp.int32, (cw, cw), 0)
             > lax.broadcasted_iota(jnp.int32, (cw, cw), 1)).astype(BF16)
    later2 = jnp.concatenate([later, later], axis=0)
    heads = [slice(hh * HEAD_DIM, (hh + 1) * HEAD_DIM) for hh in range(2)]
    q = [q_ref[0, :, h] for h in heads]

    def block(ks, carry, diagonal):
        z = [_dot_nt(q[hh], k_ref[0, pl.ds(ks, t), h]) for hh, h in enumerate(heads)]
        acc = [carry[0], carry[2]]
        run = [carry[1], carry[3]]
        weights = [[None] * n_sub for _ in heads]
        for sub in reversed(range(n_sub)):
            for hh in range(2):
                zz = z[hh][:, sub * cw:(sub + 1) * cw]
                sp = jnp.maximum(zz, 0.0) + jnp.log(1.0 + jnp.exp(-jnp.abs(zz)))
                if diagonal:
                    before = (cols + sub * cw) < rows
                    mass = jnp.where(before, sp, 0.0)
                else:
                    mass = sp
                hi, lo = _split(mass)
                after = _dot(jnp.concatenate([hi, lo], axis=1), later2)
                wgt = jnp.exp(((zz - sp) - after) - run[hh])
                if diagonal:
                    wgt = jnp.where(before, wgt, 0.0)
                weights[hh][sub] = wgt.astype(BF16)
                run[hh] = run[hh] + (after[:, :1] + mass[:, :1])
        for hh, h in enumerate(heads):
            acc[hh] = acc[hh] + _dot(jnp.concatenate(weights[hh], axis=1), v_ref[0, pl.ds(ks, t), h])
        return acc[0], run[0], acc[1], run[1]

    zeros = (jnp.zeros((t, HEAD_DIM), F32), jnp.zeros((t, 1), F32)) * 2
    carry = block(pl.multiple_of(qi * t, t), zeros, True)
    carry = lax.fori_loop(
        0, qi, lambda it, c: block(pl.multiple_of((qi - 1 - it) * t, t), c, False), carry)
    o_ref[0] = jnp.concatenate([carry[0], carry[2]], axis=1).astype(o_ref.dtype)


def sb_attention(qkv, batch, seq):
    t = _pick(seq, (512, 256, 128))
    x = qkv.reshape(batch, seq, 3 * WIDTH)
    hp = WIDTH // LANES
    out = pl.pallas_call(
        functools.partial(_sb_body, t=t, cw=min(t, 256)),
        grid=(batch, hp, seq // t),
        in_specs=[pl.BlockSpec((1, t, LANES), lambda b, h, i: (b, i, h)),
                  pl.BlockSpec((1, seq, LANES), lambda b, h, i: (b, 0, hp + h)),
                  pl.BlockSpec((1, seq, LANES), lambda b, h, i: (b, 0, 2 * hp + h))],
        out_specs=pl.BlockSpec((1, t, LANES), lambda b, h, i: (b, i, h)),
        out_shape=jax.ShapeDtypeStruct((batch, seq, WIDTH), BF16),
        compiler_params=_params("parallel", "parallel", "parallel"),
        name="sb_attention",
    )(x, x, x)
    return out.reshape(batch * seq, WIDTH)


def _mla_body(q_ref, k_ref, v_ref, o_ref, *, t):
    qi = pl.program_id(2)
    rows = lax.broadcasted_iota(jnp.int32, (t, t), 0)
    cols = lax.broadcasted_iota(jnp.int32, (t, t), 1)
    q = [q_ref[0, :, hh * LANES:(hh + 1) * LANES] for hh in range(2)]

    def block(ks, carry, diagonal):
        out = []
        for hh in range(2):
            m, l, acc = carry[3 * hh:3 * hh + 3]
            s = _dot_nt(q[hh], k_ref[0, pl.ds(ks, t), hh * LANES:(hh + 1) * LANES])
            if diagonal:
                s = jnp.where(cols <= rows, s, -1e30)
            m_new = jnp.maximum(m, jnp.max(s, axis=-1, keepdims=True))
            p = jnp.exp(s - m_new)
            corr = jnp.exp(m - m_new)
            l = corr * l + jnp.sum(p, axis=-1, keepdims=True)
            v = v_ref[0, pl.ds(ks, t), hh * HEAD_DIM:(hh + 1) * HEAD_DIM]
            acc = corr * acc + _dot(p.astype(BF16), v)
            out += [m_new, l, acc]
        return tuple(out)

    init = (jnp.full((t, 1), -1e30, F32), jnp.zeros((t, 1), F32), jnp.zeros((t, HEAD_DIM), F32)) * 2
    carry = block(pl.multiple_of(qi * t, t), init, True)
    carry = lax.fori_loop(0, qi, lambda kb, c: block(pl.multiple_of(kb * t, t), c, False), carry)
    o_ref[0] = jnp.concatenate([carry[2] / carry[1], carry[5] / carry[4]], axis=1).astype(o_ref.dtype)


def mla_attention(q, k, v, batch, seq):
    t = _pick(seq, (512, 256, 128))
    hp = HEADS // 2
    out = pl.pallas_call(
        functools.partial(_mla_body, t=t),
        grid=(batch, hp, seq // t),
        in_specs=[pl.BlockSpec((1, t, 2 * LANES), lambda b, h, i: (b, i, h)),
                  pl.BlockSpec((1, seq, 2 * LANES), lambda b, h, i: (b, 0, h)),
                  pl.BlockSpec((1, seq, LANES), lambda b, h, i: (b, 0, h))],
        out_specs=pl.BlockSpec((1, t, LANES), lambda b, h, i: (b, i, h)),
        out_shape=jax.ShapeDtypeStruct((batch, seq, WIDTH), BF16),
        compiler_params=_params("parallel", "parallel", "parallel"),
        name="mla_attention",
    )(q.reshape(batch, seq, HEADS * LANES), k.reshape(batch, seq, HEADS * LANES),
      v.reshape(batch, seq, WIDTH))
    return out.reshape(batch * seq, WIDTH)


RWKV_IN = 3 * WIDTH + 2 * RWKV_LORA_PAD + RWKV_GATE_LORA


def _rwkv_prep_body(*refs, tiles_per_seq, has_vres):
    if has_vres:
        (cols_ref, prev_ref, mu_ref, w0_ref, w2_ref, a0_ref, a2_ref, g2_ref, kk_ref, ka_ref,
         v0_ref, v1_ref, v2_ref, vfirst_ref,
         r_out, lw_out, k_out, v_out, kk_out, a_out, g_out) = refs
    else:
        (cols_ref, prev_ref, mu_ref, w0_ref, w2_ref, a0_ref, a2_ref, g2_ref, kk_ref, ka_ref,
         r_out, lw_out, k_out, v_out, kk_out, a_out, g_out) = refs
    x = cols_ref[...]
    starts_seq = (pl.program_id(0) % tiles_per_seq) == 0
    last_prev = jnp.where(starts_seq, 0.0, prev_ref[7:8, :])
    first_row = lax.broadcasted_iota(jnp.int32, x.shape, 0) == 0
    prev = jnp.where(first_row, last_prev, pltpu.roll(x, 1, 0))
    x = x + (prev - x) * mu_ref[...]
    r = x[:, :WIDTH]
    k = x[:, WIDTH:2 * WIDTH]
    v = x[:, 2 * WIDTH:3 * WIDTH]
    o = 3 * WIDTH
    wd = x[:, o:o + RWKV_LORA_PAD]
    ad = x[:, o + RWKV_LORA_PAD:o + 2 * RWKV_LORA_PAD]
    gd = x[:, o + 2 * RWKV_LORA_PAD:]
    wpre = w0_ref[...] + _dot(jnp.tanh(wd).astype(BF16), w2_ref[...])
    w = -jax.nn.softplus(-wpre) - 0.5
    lw_out[...] = -jnp.exp(w)
    a = jax.nn.sigmoid(a0_ref[...] + _dot(ad.astype(BF16), a2_ref[...]))
    g_out[...] = _dot(jax.nn.sigmoid(gd).astype(BF16), g2_ref[...])
    if has_vres:
        low = _dot(v.astype(BF16), v1_ref[...])
        mix = jax.nn.sigmoid(v0_ref[...] + _dot(low.astype(BF16), v2_ref[...]))
        v = v + (vfirst_ref[...] - v) * mix
    r_out[...] = r
    v_out[...] = v
    kk_out[...] = k * kk_ref[...]
    k_out[...] = k * (1.0 + (a - 1.0) * ka_ref[...])
    a_out[...] = a


def rwkv_prep(cols, seq, mu, w0, w2, a0, a2, g2, k_k, k_a, vres):
    m = cols.shape[0]
    tm = _pick(seq, (256, 128, 64, 32, 16, 8))
    row = lambda t: t.reshape(1, -1).astype(F32)
    args = [cols, cols, row(mu), row(w0), w2, row(a0), a2, g2, row(k_k), row(k_a)]
    full = lambda arr: pl.BlockSpec(arr.shape, lambda i: (0, 0))
    tile = pl.BlockSpec((tm, WIDTH), lambda i: (i, 0))
    in_specs = [pl.BlockSpec((tm, RWKV_IN), lambda i: (i, 0)),
                pl.BlockSpec((8, RWKV_IN), lambda i: (jnp.maximum(i * (tm // 8) - 1, 0), 0))]
    in_specs += [full(t) for t in args[2:]]
    if vres is not None:
        v0, v1, v2, v_first = vres
        extra = [row(v0), v1, v2]
        args += extra + [v_first]
        in_specs += [full(t) for t in extra] + [tile]
    return pl.pallas_call(
        functools.partial(_rwkv_prep_body, tiles_per_seq=seq // tm, has_vres=vres is not None),
        grid=(m // tm,),
        in_specs=in_specs,
        out_specs=[tile] * 7,
        out_shape=[jax.ShapeDtypeStruct((m, WIDTH), F32)] * 7,
        compiler_params=_params("parallel"),
        name="rwkv_prep",
    )(*args)


def _halves(x):
    return x[:, :HEAD_DIM], x[:, HEAD_DIM:]


def _head_sum(x, left):
    s0 = jnp.sum(jnp.where(left, x, 0.0), axis=-1, keepdims=True)
    s1 = jnp.sum(jnp.where(left, 0.0, x), axis=-1, keepdims=True)
    return jnp.where(left, s0, s1)


def _dot3(x, y, dot):
    xh, xl = _split(x)
    yh, yl = _split(y)
    return dot(xh, yh) + (dot(xh, yl) + dot(xl, yh))


def _rwkv_chunk_terms(r, lw, k, v, kk_raw, a):
    c = r.shape[0]
    left = lax.broadcasted_iota(jnp.int32, (c, LANES), 1) < HEAD_DIM
    trow = lax.broadcasted_iota(jnp.int32, (c, LANES), 0)
    kk = kk_raw / jnp.maximum(jnp.sqrt(_head_sum(kk_raw * kk_raw, left)), 1e-12)
    beta = kk * a
    cl = lw
    shift = 1
    while shift < c:
        cl = cl + jnp.where(trow >= shift, pltpu.roll(cl, shift, 0), 0.0)
        shift *= 2
    a_t = -kk * jnp.exp(cl - lw)
    r_t = r * jnp.exp(cl)
    inv = jnp.exp(-cl)
    cl_end = cl[c - 1:c, :]
    tail = jnp.exp(cl_end - cl)
    full = (a_t, r_t, beta * inv, k * inv, v, beta * tail, k * tail, jnp.exp(cl_end))
    return [tuple(_halves(x)[hh] for x in full) for hh in range(2)]


def _rwkv_masks(c):
    ti = lax.broadcasted_iota(jnp.int32, (c, c), 0)
    tj = lax.broadcasted_iota(jnp.int32, (c, c), 1)
    merges = []
    half = 1
    while half < c:
        merges.append((ti // (2 * half) == tj // (2 * half)) & (ti % (2 * half) >= half)
                      & (tj % (2 * half) < half))
        half *= 2
    return ti > tj, ti >= tj, ti == tj, merges


def _rwkv_chain(out, head_terms, masks):
    at_h, rt_h, bt_h, kt_h, v_h, bbar_h, kbar_h, gend_h = head_terms
    strict, incl, eye, merges = masks
    c = at_h.shape[0]
    ar = jnp.concatenate([at_h, rt_h], axis=0).astype(BF16)
    bk = jnp.concatenate([bt_h, kt_h], axis=0).astype(BF16)
    pair = _dot_nt(ar, bk)
    yield
    lower = jnp.where(strict, pair[:c, :c], 0.0)
    akv = _dot(jnp.where(strict, pair[:c, c:], 0.0).astype(BF16), v_h.astype(BF16))
    tinv = eye.astype(F32) + jnp.where(merges[0], lower, 0.0)
    for off in merges[1:]:
        tb = tinv.astype(BF16)
        step = _dot(tb, jnp.where(off, lower, 0.0).astype(BF16))
        yield
        tinv = tinv + _dot(step.astype(BF16), tb)
        yield
    solved = _dot(tinv.astype(BF16), jnp.concatenate([at_h, akv], axis=1).astype(BF16))
    yield
    ta, w = _halves(solved)
    rb = jnp.where(incl, pair[c:, :c], 0.0).astype(BF16)
    rk = jnp.where(incl, pair[c:, c:], 0.0).astype(BF16)
    wv = jnp.concatenate([w, v_h], axis=0).astype(BF16)
    q = rt_h + _dot(rb, ta.astype(BF16))
    y0 = _dot(jnp.concatenate([rb, rk], axis=1), wv)
    g = jnp.where(eye, gend_h, 0.0) + _dot_tn(ta.astype(BF16), bbar_h.astype(BF16))
    d = _dot_tn(wv, jnp.concatenate([bbar_h, kbar_h], axis=0).astype(BF16))
    out.append((q, y0, g, d))


def _lockstep(chains):
    live = list(chains)
    while live:
        still = []
        for ch in live:
            try:
                next(ch)
                still.append(ch)
            except StopIteration:
                pass
        live = still


def _rwkv_scan_body(r_ref, lw_ref, k_ref, v_ref, kk_ref, a_ref, g_ref, lng_ref, lnb_ref, rk_ref,
                    o_ref, state_ref, *, chunk):
    @pl.when(pl.program_id(2) == 0)
    def _():
        state_ref[...] = jnp.zeros_like(state_ref)

    n_chunks = r_ref.shape[1] // chunk
    left = lax.broadcasted_iota(jnp.int32, (chunk, LANES), 1) < HEAD_DIM
    masks = _rwkv_masks(chunk)
    rows = [slice(ci * chunk, (ci + 1) * chunk) for ci in range(n_chunks)]
    results = [[[], []] for _ in range(n_chunks)]
    chains = []
    for ci in range(n_chunks):
        terms = _rwkv_chunk_terms(r_ref[0, rows[ci], :], lw_ref[0, rows[ci], :], k_ref[0, rows[ci], :],
                                  v_ref[0, rows[ci], :], kk_ref[0, rows[ci], :], a_ref[0, rows[ci], :])
        chains += [_rwkv_chain(results[ci][hh], terms[hh], masks) for hh in range(2)]
    _lockstep(chains)

    state = [state_ref[0], state_ref[1]]
    for ci in range(n_chunks):
        ys = []
        for hh in range(2):
            q, y0, g, d = results[ci][hh][0]
            ys.append(_dot3(q, state[hh], _dot_nt) + y0)
            state[hh] = _dot3(state[hh], g, _dot) + d
        y = jnp.concatenate(ys, axis=1)
        r, k, v = r_ref[0, rows[ci], :], k_ref[0, rows[ci], :], v_ref[0, rows[ci], :]
        mean = _head_sum(y, left) * (1.0 / HEAD_DIM)
        var = _head_sum(jnp.square(y - mean), left) * (1.0 / HEAD_DIM)
        y = (y - mean) * lax.rsqrt(var + RWKV_GN_EPS) * lng_ref[...] + lnb_ref[...]
        bonus = _head_sum(r * k * rk_ref[...], left) * v
        o_ref[0, rows[ci], :] = ((y + bonus) * g_ref[0, rows[ci], :]).astype(o_ref.dtype)
    state_ref[0] = state[0]
    state_ref[1] = state[1]


def rwkv_scan(r, lw, k, v, kk, a, g, ln_g, ln_b, r_k, batch, seq):
    chunk = min(RWKV_CHUNK, seq)
    rows = _pick(seq, (512, 256, 128, 64, 32, 16, 8))
    hp = WIDTH // LANES
    seq_spec = pl.BlockSpec((1, rows, LANES), lambda b, h, c: (b, c, h))
    par_spec = pl.BlockSpec((1, LANES), lambda b, h, c: (0, h))
    shaped = [t.reshape(batch, seq, WIDTH) for t in (r, lw, k, v, kk, a, g)]
    pars = [t.reshape(1, WIDTH).astype(F32) for t in (ln_g, ln_b, r_k)]
    out = pl.pallas_call(
        functools.partial(_rwkv_scan_body, chunk=chunk),
        grid=(batch, hp, seq // rows),
        in_specs=[seq_spec] * 7 + [par_spec] * 3,
        out_specs=seq_spec,
        out_shape=jax.ShapeDtypeStruct((batch, seq, WIDTH), BF16),
        scratch_shapes=[pltpu.VMEM((2, HEAD_DIM, HEAD_DIM), F32)],
        compiler_params=_params("parallel", "parallel", "arbitrary"),
        name="rwkv_scan",
    )(*shaped, *pars)
    return out.reshape(batch * seq, WIDTH)


def _merge_body(h, a1, a2, a3, g1, g2, g3, w1, w2, w3, o_ref):
    hh = h[...]
    acc = jax.nn.sigmoid(_dot(hh, g1[...])) * _dot(a1[...], w1[...])
    acc += jax.nn.sigmoid(_dot(hh, g2[...])) * _dot(a2[...], w2[...])
    acc += jax.nn.sigmoid(_dot(hh, g3[...])) * _dot(a3[...], w3[...])
    o_ref[...] = acc.astype(o_ref.dtype)


def merge_branches(h, outs, w_gates, weights):
    m, d = h.shape
    tm = _pick(m, (512, 256, 128, 64, 32, 16, 8))
    tn = _pick(d, (512, 256, 128))
    nb = d // tn
    h_spec = pl.BlockSpec((tm, d), lambda i, j: (i, 0))
    a_spec = pl.BlockSpec((tm, WIDTH), lambda i, j: (i, 0))
    w_spec = pl.BlockSpec((WIDTH, tn), lambda i, j: (0, j))
    g_specs = [pl.BlockSpec((d, tn), lambda i, j, b=b: (0, j + b * nb)) for b in range(3)]
    return pl.pallas_call(
        _merge_body,
        grid=(m // tm, nb),
        in_specs=[h_spec] + [a_spec] * 3 + g_specs + [w_spec] * 3,
        out_specs=pl.BlockSpec((tm, tn), lambda i, j: (i, j)),
        out_shape=jax.ShapeDtypeStruct((m, d), BF16),
        compiler_params=_params("parallel", "parallel"),
        name="merge_branches",
    )(h, *outs, w_gates, w_gates, w_gates, *weights)


ROUTER_PAD = 128


def _slab_cols(ref, n, slabs, first=0, pitch=None):
    pitch = slabs if pitch is None else pitch
    return jnp.concatenate([ref[pl.ds(first + s, n, stride=pitch), :] for s in range(slabs)], axis=1)


def _store_slabs(ref, value, slabs):
    n = value.shape[0]
    for s in range(slabs):
        ref[pl.ds(s, n, stride=slabs), :] = value[:, s * LANES:(s + 1) * LANES].astype(ref.dtype)


def _router_body(x_ref, g_ref, w_ref, b_ref, h_ref, logit_ref, *, slabs):
    x = x_ref[...]
    y = x * lax.rsqrt(jnp.mean(x * x, axis=-1, keepdims=True) + NORM_EPS)
    h = y * g_ref[...]
    _store_slabs(h_ref, h, slabs)
    logit_ref[...] = jnp.dot(h, w_ref[...], preferred_element_type=F32, precision=HIGHEST) + b_ref[...]


def router(x, g, w_pad, b_pad):
    m, d = x.shape
    slabs = d // LANES
    tm = _pick(m, (256, 128, 64, 32, 16, 8))
    return pl.pallas_call(
        functools.partial(_router_body, slabs=slabs),
        grid=(m // tm,),
        in_specs=[pl.BlockSpec((tm, d), lambda i: (i, 0)),
                  pl.BlockSpec((1, d), lambda i: (0, 0)),
                  pl.BlockSpec((d, ROUTER_PAD), lambda i: (0, 0)),
                  pl.BlockSpec((1, ROUTER_PAD), lambda i: (0, 0))],
        out_specs=[pl.BlockSpec((tm * slabs, LANES), lambda i: (i, 0)),
                   pl.BlockSpec((tm, ROUTER_PAD), lambda i: (i, 0))],
        out_shape=[jax.ShapeDtypeStruct((m * slabs, LANES), F32),
                   jax.ShapeDtypeStruct((m, ROUTER_PAD), F32)],
        compiler_params=_params("parallel"),
        name="ffn_norm_router",
    )(x, g.reshape(1, d).astype(F32), w_pad, b_pad)


def _row_gather(idx_ref, first, n, src_ref, buf_ref, sem, slabs):
    def copy(r):
        src = pl.multiple_of(idx_ref[first + r] * slabs, slabs)
        return pltpu.make_async_copy(src_ref.at[pl.ds(src, slabs)],
                                     buf_ref.at[pl.ds(r * slabs, slabs)], sem)

    def start():
        lax.fori_loop(0, n, lambda r, c: (copy(r).start(), c)[1], 0, unroll=8)

    def wait():
        lax.fori_loop(0, n, lambda r, c: (copy(r).wait(), c)[1], 0, unroll=8)

    return start, wait


def _expert_body(be_ref, nb_ref, tok_ref, h_ref, wg_ref, wu_ref, wd_ref, o_ref,
                 xbuf, wg_s, wu_s, wd_s, sem, *, slabs):
    i = pl.program_id(0)
    n_blocks = nb_ref[0]
    slot = i % 2

    def gather(block, slot):
        return _row_gather(tok_ref, block * MOE_ROWS, MOE_ROWS, h_ref, xbuf.at[slot], sem.at[slot], slabs)

    @pl.when(i == 0)
    def _():
        gather(0, 0)[0]()

    @pl.when(i + 1 < n_blocks)
    def _():
        gather(i + 1, 1 - slot)[0]()

    e = be_ref[i]

    @pl.when((i == 0) | (e != be_ref[jnp.maximum(i - 1, 0)]))
    def _():
        wg_s[...] = wg_ref[0, 0].astype(BF16)
        wu_s[...] = wu_ref[0, 0].astype(BF16)
        wd_s[...] = wd_ref[0, 0].astype(BF16)

    @pl.when(i < n_blocks)
    def _():
        gather(i, slot)[1]()
        x = _slab_cols(xbuf.at[slot], MOE_ROWS, slabs).astype(BF16)
        gate = _dot(x, wg_s[...])
        up = _dot(x, wu_s[...])
        mid = (jax.nn.silu(gate) * up).astype(BF16)
        _store_slabs(o_ref, _dot(mid, wd_s[...]), slabs)

    @pl.when(i >= n_blocks)
    def _():
        o_ref[...] = jnp.zeros_like(o_ref)


def expert_ffn(h_rows, row_tok, block_e, n_blocks, layer, w_gate, w_up, w_down):
    _, n_exp, d, ff = w_gate.shape
    slabs = d // LANES
    p = row_tok.shape[0]
    return pl.pallas_call(
        functools.partial(_expert_body, slabs=slabs),
        grid_spec=pltpu.PrefetchScalarGridSpec(
            num_scalar_prefetch=3,
            grid=(p // MOE_ROWS,),
            in_specs=[pl.BlockSpec(memory_space=pl.ANY),
                      pl.BlockSpec((1, 1, d, ff), lambda i, be, nb, tok: (layer, be[i], 0, 0)),
                      pl.BlockSpec((1, 1, d, ff), lambda i, be, nb, tok: (layer, be[i], 0, 0)),
                      pl.BlockSpec((1, 1, ff, d), lambda i, be, nb, tok: (layer, be[i], 0, 0))],
            out_specs=pl.BlockSpec((MOE_ROWS * slabs, LANES), lambda i, be, nb, tok: (i, 0)),
            scratch_shapes=[pltpu.VMEM((2, MOE_ROWS * slabs, LANES), F32),
                            pltpu.VMEM((d, ff), BF16), pltpu.VMEM((d, ff), BF16),
                            pltpu.VMEM((ff, d), BF16),
                            pltpu.SemaphoreType.DMA((2,))]),
        out_shape=jax.ShapeDtypeStruct((p * slabs, LANES), F32),
        compiler_params=_params("arbitrary"),
        name="expert_ffn",
    )(block_e, n_blocks, row_tok, h_rows, w_gate, w_up, w_down)


def _combine_body(dest_ref, x_ref, gate_ref, y_ref, o_ref, ybuf, sem, *, tm, slabs):
    i = pl.program_id(0)
    slot = i % 2

    def gather(step, slot):
        return _row_gather(dest_ref, step * tm * TOP_K, tm * TOP_K, y_ref, ybuf.at[slot], sem.at[slot], slabs)

    @pl.when(i == 0)
    def _():
        gather(0, 0)[0]()

    @pl.when(i + 1 < pl.num_programs(0))
    def _():
        gather(i + 1, 1 - slot)[0]()

    gather(i, slot)[1]()
    buf = ybuf.at[slot]
    gate = gate_ref[...]
    g0, g1 = gate[:, 0:1], gate[:, 1:2]
    for s in range(slabs):
        y0 = buf[pl.ds(s, tm, stride=TOP_K * slabs), :]
        y1 = buf[pl.ds(slabs + s, tm, stride=TOP_K * slabs), :]
        cols = slice(s * LANES, (s + 1) * LANES)
        o_ref[:, cols] = x_ref[:, cols] + (g0 * y0 + g1 * y1)


def moe_combine(x, y_rows, dest, gate):
    m, d = x.shape
    slabs = d // LANES
    tm = _pick(m, (256, 128, 64, 32, 16, 8))
    tile = lambda i, dest: (i, 0)
    return pl.pallas_call(
        functools.partial(_combine_body, tm=tm, slabs=slabs),
        grid_spec=pltpu.PrefetchScalarGridSpec(
            num_scalar_prefetch=1,
            grid=(m // tm,),
            in_specs=[pl.BlockSpec((tm, d), tile), pl.BlockSpec((tm, TOP_K), tile),
                      pl.BlockSpec(memory_space=pl.ANY)],
            out_specs=pl.BlockSpec((tm, d), tile),
            scratch_shapes=[pltpu.VMEM((2, tm * TOP_K * slabs, LANES), F32),
                            pltpu.SemaphoreType.DMA((2,))]),
        out_shape=jax.ShapeDtypeStruct((m, d), F32),
        compiler_params=_params("arbitrary"),
        name="moe_combine",
    )(dest.reshape(m * TOP_K), x, gate, y_rows)


def _routing_tables(logits):
    t = logits.shape[0]
    n_assign = t * TOP_K
    tok = jnp.arange(t)
    g_logits = logits[:, :N_GROUPS]
    grp = jnp.argmax(g_logits, axis=-1)
    p_grp = jax.nn.softmax(g_logits, axis=-1)[tok, grp]
    e_logits = logits[:, N_GROUPS:N_GROUPS + N_EXPERTS].reshape(t, N_GROUPS, EXPERTS_PER_GROUP)
    p_in, idx_in = lax.top_k(jax.nn.softmax(e_logits[tok, grp], axis=-1), TOP_K)
    gate = p_grp[:, None] * p_in / jnp.sum(p_in, axis=-1, keepdims=True)
    flat_e = (grp[:, None] * EXPERTS_PER_GROUP + idx_in).reshape(n_assign).astype(jnp.int32)
    onehot = (flat_e[:, None] == jnp.arange(N_EXPERTS, dtype=jnp.int32)[None, :]).astype(jnp.int32)
    rank = jnp.sum((jnp.cumsum(onehot, axis=0) - onehot) * onehot, axis=1)
    counts = jnp.sum(onehot, axis=0)
    padded = (counts + MOE_ROWS - 1) // MOE_ROWS * MOE_ROWS
    pend = jnp.cumsum(padded)
    dest = ((pend - padded)[flat_e] + rank).astype(jnp.int32)
    n_blocks_max = -(-n_assign // MOE_ROWS) + N_EXPERTS
    row_tok = jnp.zeros((n_blocks_max * MOE_ROWS,), jnp.int32).at[dest].set(
        jnp.repeat(tok, TOP_K).astype(jnp.int32))
    block_e = jnp.minimum(
        jnp.searchsorted(pend, jnp.arange(n_blocks_max) * MOE_ROWS, side="right"),
        N_EXPERTS - 1).astype(jnp.int32)
    n_blocks = (pend[-1:] // MOE_ROWS).astype(jnp.int32)
    return gate.astype(F32), dest.reshape(t, TOP_K), row_tok, block_e, n_blocks


def hier_moe_residual(x, norm_g, rg_w, rg_b, re_w, re_b, layer, w_gate, w_up, w_down):
    t, d = x.shape
    w_pad = jnp.zeros((d, ROUTER_PAD), F32).at[:, :N_GROUPS].set(rg_w)
    w_pad = w_pad.at[:, N_GROUPS:N_GROUPS + N_EXPERTS].set(re_w)
    b_pad = jnp.zeros((1, ROUTER_PAD), F32).at[0, :N_GROUPS].set(rg_b)
    b_pad = b_pad.at[0, N_GROUPS:N_GROUPS + N_EXPERTS].set(re_b)
    h_rows, logits = router(x, norm_g, w_pad, b_pad)
    gate, dest, row_tok, block_e, n_blocks = _routing_tables(logits)
    y_rows = expert_ffn(h_rows, row_tok, block_e, n_blocks, layer, w_gate, w_up, w_down)
    return moe_combine(x, y_rows, dest, gate)


def _mla_q_weight(w_uq):
    w = w_uq.reshape(MLA_Q_LORA, HEADS, HEAD_DIM + MLA_ROPE_DIM)
    w = jnp.pad(w, ((0, 0), (0, 0), (0, LANES - HEAD_DIM - MLA_ROPE_DIM)))
    return w.reshape(MLA_Q_LORA, HEADS * LANES).astype(BF16)


def _mla_kv_weights(w_ukv):
    w = w_ukv.reshape(MLA_KV_LORA, HEADS, 2 * HEAD_DIM)
    wk = jnp.pad(w[:, :, :HEAD_DIM], ((0, 0), (0, 0), (0, LANES - HEAD_DIM)))
    wv = w[:, :, HEAD_DIM:]
    return (wk.reshape(MLA_KV_LORA, HEADS * LANES).astype(BF16),
            wv.reshape(MLA_KV_LORA, WIDTH).astype(BF16))


def _rope_slot_tables(positions):
    half = MLA_ROPE_DIM // 2
    inv_freq = ROPE_THETA ** (-jnp.arange(0, MLA_ROPE_DIM, 2, dtype=F32) / MLA_ROPE_DIM)
    ang = positions.astype(F32).reshape(-1, 1) * inv_freq[None, :]
    cos, sin = jnp.cos(ang), jnp.sin(ang)
    t = ang.shape[0]
    pad = jnp.zeros((t, LANES - HEAD_DIM - 2 * half), F32)
    cos_t = jnp.concatenate([jnp.ones((t, HEAD_DIM), F32), cos, cos, pad], axis=1)
    sin_t = jnp.concatenate([jnp.zeros((t, HEAD_DIM), F32), -sin, sin, pad], axis=1)
    return cos_t, sin_t


def _mixer_block(x, l, batch, seq, cos_t, sin_t, v_first, p):
    t, d = x.shape
    h = rmsnorm(x, p["attn_norm_g"][l], BF16, name="attn_norm")
    w_in = p["w_in"][l]
    o_rwkv = 3 * WIDTH
    n_rwkv = 3 * WIDTH + RWKV_DECAY_LORA + RWKV_ICLR_LORA + RWKV_GATE_LORA
    o_mla = o_rwkv + n_rwkv
    n_mla = MLA_Q_LORA + MLA_KV_LORA + MLA_ROPE_DIM
    o_gate = o_mla + n_mla

    qscale = jnp.concatenate([jnp.full((1, WIDTH), HEAD_DIM ** -0.5, F32), jnp.ones((1, 2 * WIDTH), F32)], axis=1)
    (qkv,) = matmul(h, w_in[:, :o_rwkv].astype(BF16), [BF16],
                    epilogue=lambda acc, s: (acc * s,), extras=[(qscale, "col", 0)], name="proj_sb")
    o_sb = sb_attention(qkv, batch, seq)

    wr = w_in[:, o_rwkv:o_mla]
    zpad = jnp.zeros((d, RWKV_LORA_PAD - RWKV_DECAY_LORA), F32)
    c0 = 3 * WIDTH
    w_rwkv = jnp.concatenate([wr[:, :c0], wr[:, c0:c0 + RWKV_DECAY_LORA], zpad,
                              wr[:, c0 + RWKV_DECAY_LORA:c0 + 2 * RWKV_DECAY_LORA], zpad,
                              wr[:, c0 + 2 * RWKV_DECAY_LORA:]], axis=1).astype(BF16)
    (cols,) = matmul(h, w_rwkv, [F32], name="proj_rwkv")
    mu = p["rwkv_mu"][l]
    mpad = jnp.zeros((RWKV_LORA_PAD - RWKV_DECAY_LORA,), F32)
    mu_pad = jnp.concatenate([mu[:c0], mu[c0:c0 + RWKV_DECAY_LORA], mpad,
                              mu[c0 + RWKV_DECAY_LORA:c0 + 2 * RWKV_DECAY_LORA], mpad,
                              mu[c0 + 2 * RWKV_DECAY_LORA:]])
    rpad = ((0, RWKV_LORA_PAD - RWKV_DECAY_LORA), (0, 0))
    vres = None
    if l > 0:
        vres = (p["rwkv_v0"][l - 1], p["rwkv_v1"][l - 1].astype(BF16), p["rwkv_v2"][l - 1].astype(BF16), v_first)
    r, lw, k, v, kk, a, g = rwkv_prep(
        cols, seq, mu_pad, p["rwkv_w0"][l], jnp.pad(p["rwkv_w2"][l], rpad).astype(BF16),
        p["rwkv_a0"][l], jnp.pad(p["rwkv_a2"][l], rpad).astype(BF16), p["rwkv_g2"][l].astype(BF16),
        p["rwkv_k_k"][l], p["rwkv_k_a"][l], vres)
    if l == 0:
        v_first = v
    o_rw = rwkv_scan(r, lw, k, v, kk, a, g, p["rwkv_ln_g"][l], p["rwkv_ln_b"][l], p["rwkv_r_k"][l],
                     batch, seq)

    wm = w_in[:, o_mla:o_gate]
    w_mla = jnp.concatenate([wm[:, :MLA_Q_LORA + MLA_KV_LORA], jnp.zeros((d, HEAD_DIM), F32),
                             wm[:, MLA_Q_LORA + MLA_KV_LORA:],
                             jnp.zeros((d, LANES - HEAD_DIM - MLA_ROPE_DIM), F32)], axis=1).astype(BF16)
    (mcols,) = matmul(h, w_mla, [F32], name="proj_mla")
    cq = rmsnorm(mcols, p["mla_q_norm_g"][l], BF16, width=MLA_Q_LORA, col_block=0, name="mla_q_norm")
    ckv = rmsnorm(mcols, p["mla_kv_norm_g"][l], BF16, width=MLA_KV_LORA,
                  col_block=MLA_Q_LORA // MLA_KV_LORA, name="mla_kv_norm")
    mla_scale = (HEAD_DIM + MLA_ROPE_DIM) ** -0.5
    (q_mla,) = matmul(cq, _mla_q_weight(p["mla_w_uq"][l]), [BF16],
                      epilogue=lambda acc, c, s: (_rope_slots(acc, c, s) * mla_scale,),
                      extras=[(cos_t, "row", 0), (sin_t, "row", 0)], name="mla_q_up")
    wk, wv = _mla_kv_weights(p["mla_w_ukv"][l])
    kpe_block = (MLA_Q_LORA + MLA_KV_LORA) // LANES

    def k_epilogue(acc, kpe, c, s):
        kr = _rope_slots(kpe, c, s)
        return (acc + jnp.concatenate([kr] * (acc.shape[1] // LANES), axis=1),)

    kpe = mcols[:, kpe_block * LANES:(kpe_block + 1) * LANES]
    (k_mla,) = matmul(ckv, wk, [BF16], epilogue=k_epilogue,
                      extras=[(kpe, "row", 0), (cos_t, "row", 0), (sin_t, "row", 0)], name="mla_k_up")
    (v_mla,) = matmul(ckv, wv, [BF16], name="mla_v_up")
    o_mla_out = mla_attention(q_mla, k_mla, v_mla, batch, seq)

    merged = merge_branches(h, [o_sb, o_rw, o_mla_out], w_in[:, o_gate:].astype(BF16),
                            [p["w_br_sb"][l].astype(BF16), p["w_br_rwkv"][l].astype(BF16),
                             p["w_br_mla"][l].astype(BF16)])
    (x,) = matmul(merged, p["w_out"][l].astype(BF16), [F32],
                  epilogue=lambda acc, res: (res + acc,), extras=[(x, "tile", 0)], name="out_proj")
    return x, v_first


def kernel(x, positions, attn_norm_g, w_in, rwkv_mu, rwkv_w0, rwkv_w2, rwkv_a0, rwkv_a2, rwkv_g2, rwkv_k_k, rwkv_k_a, rwkv_r_k, rwkv_ln_g, rwkv_ln_b, rwkv_v0, rwkv_v1, rwkv_v2, mla_q_norm_g, mla_w_uq, mla_kv_norm_g, mla_w_ukv, w_br_sb, w_br_rwkv, w_br_mla, w_out, ffn_norm_g, router_group_w, router_group_b, router_expert_w, router_expert_b, expert_w_gate, expert_w_up, expert_w_down, final_norm_g):
    p = dict(attn_norm_g=attn_norm_g, w_in=w_in, rwkv_mu=rwkv_mu, rwkv_w0=rwkv_w0, rwkv_w2=rwkv_w2,
             rwkv_a0=rwkv_a0, rwkv_a2=rwkv_a2, rwkv_g2=rwkv_g2, rwkv_k_k=rwkv_k_k, rwkv_k_a=rwkv_k_a,
             rwkv_r_k=rwkv_r_k, rwkv_ln_g=rwkv_ln_g, rwkv_ln_b=rwkv_ln_b, rwkv_v0=rwkv_v0,
             rwkv_v1=rwkv_v1, rwkv_v2=rwkv_v2, mla_q_norm_g=mla_q_norm_g, mla_w_uq=mla_w_uq,
             mla_kv_norm_g=mla_kv_norm_g, mla_w_ukv=mla_w_ukv, w_br_sb=w_br_sb, w_br_rwkv=w_br_rwkv,
             w_br_mla=w_br_mla, w_out=w_out)
    batch, seq, d = x.shape
    depth = w_in.shape[0]
    cos_t, sin_t = _rope_slot_tables(positions)
    xt = x.reshape(batch * seq, d)
    v_first = None
    for l in range(depth):
        xt, v_first = _mixer_block(xt, l, batch, seq, cos_t, sin_t, v_first, p)
        xt = hier_moe_residual(xt, ffn_norm_g[l], router_group_w[l], router_group_b[l],
                               router_expert_w[l], router_expert_b[l],
                               l, expert_w_gate, expert_w_up, expert_w_down)
    return rmsnorm(xt, final_norm_g, x.dtype, name="final_norm").reshape(batch, seq, d)
```

```python
import functools

import jax
import jax.numpy as jnp
from jax import lax
from jax.experimental import pallas as pl
from jax.experimental.pallas import tpu as pltpu

F32 = jnp.float32
BF16 = jnp.bfloat16
HIGHEST = lax.Precision.HIGHEST

NORM_EPS = 1e-6
HEADS = 16
HEAD_DIM = 64
WIDTH = HEADS * HEAD_DIM
LANES = 128
RWKV_DECAY_LORA = 96
RWKV_ICLR_LORA = 96
RWKV_GATE_LORA = 256
RWKV_LORA_PAD = 128
RWKV_GN_EPS = 64e-5
RWKV_CHUNK = 64
MLA_Q_LORA = 768
MLA_KV_LORA = 256
MLA_ROPE_DIM = 32
ROPE_THETA = 10000.0
N_GROUPS = 8
EXPERTS_PER_GROUP = 8
N_EXPERTS = N_GROUPS * EXPERTS_PER_GROUP
TOP_K = 2
MOE_ROWS = 256
VMEM_LIMIT = 48 * 1024 * 1024


def _params(*sem):
    return pltpu.CompilerParams(dimension_semantics=sem, vmem_limit_bytes=VMEM_LIMIT)


def _pick(n, cands):
    for c in cands:
        if n % c == 0:
            return c
    raise ValueError(f"no tile for {n} in {cands}")


def _dot(x, y):
    return jnp.dot(x, y, preferred_element_type=F32)


def _dot_nt(x, y):
    return lax.dot_general(x, y, (((1,), (1,)), ((), ())), preferred_element_type=F32)


def _dot_tn(x, y):
    return lax.dot_general(x, y, (((0,), (0,)), ((), ())), preferred_element_type=F32)


def _split(x):
    hi = x.astype(BF16)
    return hi, (x - hi.astype(F32)).astype(BF16)


def _rmsnorm_body(x_ref, g_ref, o_ref):
    x = x_ref[...].astype(F32)
    y = x * lax.rsqrt(jnp.mean(x * x, axis=-1, keepdims=True) + NORM_EPS)
    o_ref[...] = (y * g_ref[...]).astype(o_ref.dtype)


def rmsnorm(x, g, out_dtype, *, width=None, col_block=0, name="rmsnorm"):
    m = x.shape[0]
    width = x.shape[1] if width is None else width
    tm = _pick(m, (512, 256, 128, 64, 32, 16, 8))
    return pl.pallas_call(
        _rmsnorm_body,
        grid=(m // tm,),
        in_specs=[pl.BlockSpec((tm, width), lambda i: (i, col_block)),
                  pl.BlockSpec((1, width), lambda i: (0, 0))],
        out_specs=pl.BlockSpec((tm, width), lambda i: (i, 0)),
        out_shape=jax.ShapeDtypeStruct((m, width), out_dtype),
        compiler_params=_params("parallel"),
        name=name,
    )(x, g.reshape(1, width).astype(F32))


def _mm_body(*refs, n_extra, epilogue):
    a_ref, w_ref = refs[:2]
    extra = [r[...] for r in refs[2:2 + n_extra]]
    outs = refs[2 + n_extra:]
    acc = _dot(a_ref[...], w_ref[...])
    res = epilogue(acc, *extra)
    for o, r in zip(outs, res):
        o[...] = r.astype(o.dtype)


def matmul(a, w, out_dtypes, *, epilogue=None, extras=(), a_col_block=0, name="matmul"):
    m = a.shape[0]
    k, n = w.shape
    tm = _pick(m, (1024, 512, 256, 128, 64, 32, 16, 8))
    tn = n if n <= 1280 else _pick(n, (512, 256, 128))
    if epilogue is None:
        epilogue = lambda acc: (acc,)
    in_specs = [pl.BlockSpec((tm, k), lambda i, j: (i, a_col_block)),
                pl.BlockSpec((k, tn), lambda i, j: (0, j))]
    args = [a, w]
    for arr, kind, off in extras:
        if kind == "row":
            in_specs.append(pl.BlockSpec((tm, arr.shape[1]), lambda i, j: (i, 0)))
        elif kind == "col":
            in_specs.append(pl.BlockSpec((1, tn), lambda i, j: (0, j)))
        else:
            in_specs.append(pl.BlockSpec((tm, tn), lambda i, j, off=off: (i, j + off)))
        args.append(arr)
    return pl.pallas_call(
        functools.partial(_mm_body, n_extra=len(extras), epilogue=epilogue),
        grid=(m // tm, n // tn),
        in_specs=in_specs,
        out_specs=[pl.BlockSpec((tm, tn), lambda i, j: (i, j)) for _ in out_dtypes],
        out_shape=[jax.ShapeDtypeStruct((m, n), dt) for dt in out_dtypes],
        compiler_params=_params("parallel", "parallel"),
        name=name,
    )(*args)


def _rope_slots(x, cos_t, sin_t):
    n = x.shape[-1]
    reps = n // LANES
    lane = lax.broadcasted_iota(jnp.int32, x.shape, 1) % LANES
    from_hi = pltpu.roll(x, n - MLA_ROPE_DIM // 2, 1)
    from_lo = pltpu.roll(x, MLA_ROPE_DIM // 2, 1)
    swapped = jnp.where(lane < HEAD_DIM + MLA_ROPE_DIM // 2, from_hi, from_lo)
    if reps > 1:
        cos_t = jnp.concatenate([cos_t] * reps, axis=1)
        sin_t = jnp.concatenate([sin_t] * reps, axis=1)
    return x * cos_t + swapped * sin_t


def _sb_body(q_ref, k_ref, v_ref, o_ref, *, t, cw):
    qi = pl.program_id(2)
    n_sub = t // cw
    rows = lax.broadcasted_iota(jnp.int32, (t, cw), 0)
    cols = lax.broadcasted_iota(jnp.int32, (t, cw), 1)
    later = (lax.broadcasted_iota(jnp.int32, (cw, cw), 0)
             > lax.broadcasted_iota(jnp.int32, (cw, cw), 1)).astype(BF16)
    later2 = jnp.concatenate([later, later], axis=0)
    heads = [slice(hh * HEAD_DIM, (hh + 1) * HEAD_DIM) for hh in range(2)]
    q = [q_ref[0, :, h] for h in heads]

    def block(ks, carry, diagonal):
        z = [_dot_nt(q[hh], k_ref[0, pl.ds(ks, t), h]) for hh, h in enumerate(heads)]
        acc = [carry[0], carry[2]]
        run = [carry[1], carry[3]]
        weights = [[None] * n_sub for _ in heads]
        for sub in reversed(range(n_sub)):
            for hh in range(2):
                zz = z[hh][:, sub * cw:(sub + 1) * cw]
                neg_abs = pltpu.bitcast(pltpu.bitcast(zz, jnp.uint32) | jnp.uint32(0x80000000), F32)
                sp = jnp.maximum(zz, 0.0) + jnp.log(1.0 + jnp.exp(neg_abs))
                if diagonal:
                    before = (cols + sub * cw) < rows
                    mass = jnp.where(before, sp, 0.0)
                else:
                    mass = sp
                hi, lo = _split(mass)
                after = _dot(jnp.concatenate([hi, lo], axis=1), later2)
                wgt = jnp.exp(((zz - sp) - after) - run[hh])
                if diagonal:
                    wgt = jnp.where(before, wgt, 0.0)
                weights[hh][sub] = wgt.astype(BF16)
                run[hh] = run[hh] + (after[:, :1] + mass[:, :1])
        for hh, h in enumerate(heads):
            acc[hh] = acc[hh] + _dot(jnp.concatenate(weights[hh], axis=1), v_ref[0, pl.ds(ks, t), h])
        return acc[0], run[0], acc[1], run[1]

    zeros = (jnp.zeros((t, HEAD_DIM), F32), jnp.zeros((t, 1), F32)) * 2
    carry = block(pl.multiple_of(qi * t, t), zeros, True)
    carry = lax.fori_loop(
        0, qi, lambda it, c: block(pl.multiple_of((qi - 1 - it) * t, t), c, False), carry)
    o_ref[0] = jnp.concatenate([carry[0], carry[2]], axis=1).astype(o_ref.dtype)


def sb_attention(qkv, batch, seq):
    t = _pick(seq, (512, 256, 128))
    x = qkv.reshape(batch, seq, 3 * WIDTH)
    hp = WIDTH // LANES
    out = pl.pallas_call(
        functools.partial(_sb_body, t=t, cw=min(t, 256)),
        grid=(batch, hp, seq // t),
        in_specs=[pl.BlockSpec((1, t, LANES), lambda b, h, i: (b, i, h)),
                  pl.BlockSpec((1, seq, LANES), lambda b, h, i: (b, 0, hp + h)),
                  pl.BlockSpec((1, seq, LANES), lambda b, h, i: (b, 0, 2 * hp + h))],
        out_specs=pl.BlockSpec((1, t, LANES), lambda b, h, i: (b, i, h)),
        out_shape=jax.ShapeDtypeStruct((batch, seq, WIDTH), BF16),
        compiler_params=_params("parallel", "parallel", "parallel"),
        name="sb_attention",
    )(x, x, x)
    return out.reshape(batch * seq, WIDTH)


def _mla_body(q_ref, k_ref, v_ref, o_ref, *, t):
    qi = pl.program_id(2)
    rows = lax.broadcasted_iota(jnp.int32, (t, t), 0)
    cols = lax.broadcasted_iota(jnp.int32, (t, t), 1)
    q = [q_ref[0, :, hh * LANES:(hh + 1) * LANES] for hh in range(2)]

    def block(ks, carry, diagonal):
        s = [_dot_nt(q[hh], k_ref[0, pl.ds(ks, t), hh * LANES:(hh + 1) * LANES]) for hh in range(2)]
        out = []
        for hh in range(2):
            m, l, acc = carry[3 * hh:3 * hh + 3]
            ss = jnp.where(cols <= rows, s[hh], -1e30) if diagonal else s[hh]
            m_new = jnp.maximum(m, jnp.max(ss, axis=-1, keepdims=True))
            p = jnp.exp(ss - m_new)
            corr = jnp.exp(m - m_new)
            l = corr * l + jnp.sum(p, axis=-1, keepdims=True)
            v = v_ref[0, pl.ds(ks, t), hh * HEAD_DIM:(hh + 1) * HEAD_DIM]
            acc = corr * acc + _dot(p.astype(BF16), v)
            out += [m_new, l, acc]
        return tuple(out)

    init = (jnp.full((t, 1), -1e30, F32), jnp.zeros((t, 1), F32), jnp.zeros((t, HEAD_DIM), F32)) * 2
    carry = block(pl.multiple_of(qi * t, t), init, True)
    carry = lax.fori_loop(0, qi, lambda kb, c: block(pl.multiple_of(kb * t, t), c, False), carry)
    o_ref[0] = jnp.concatenate([carry[2] / carry[1], carry[5] / carry[4]], axis=1).astype(o_ref.dtype)


def mla_attention(q, k, v, batch, seq):
    t = _pick(seq, (512, 256, 128))
    hp = HEADS // 2
    out = pl.pallas_call(
        functools.partial(_mla_body, t=t),
        grid=(batch, hp, seq // t),
        in_specs=[pl.BlockSpec((1, t, 2 * LANES), lambda b, h, i: (b, i, h)),
                  pl.BlockSpec((1, seq, 2 * LANES), lambda b, h, i: (b, 0, h)),
                  pl.BlockSpec((1, seq, LANES), lambda b, h, i: (b, 0, h))],
        out_specs=pl.BlockSpec((1, t, LANES), lambda b, h, i: (b, i, h)),
        out_shape=jax.ShapeDtypeStruct((batch, seq, WIDTH), BF16),
        compiler_params=_params("parallel", "parallel", "parallel"),
        name="mla_attention",
    )(q.reshape(batch, seq, HEADS * LANES), k.reshape(batch, seq, HEADS * LANES),
      v.reshape(batch, seq, WIDTH))
    return out.reshape(batch * seq, WIDTH)


RWKV_IN = 3 * WIDTH + 2 * RWKV_LORA_PAD + RWKV_GATE_LORA


def _rwkv_prep_body(*refs, tiles_per_seq, has_vres):
    if has_vres:
        (cols_ref, prev_ref, mu_ref, w0_ref, w2_ref, a0_ref, a2_ref, g2_ref, kk_ref, ka_ref,
         v0_ref, v1_ref, v2_ref, vfirst_ref,
         r_out, lw_out, k_out, v_out, kk_out, a_out, g_out) = refs
    else:
        (cols_ref, prev_ref, mu_ref, w0_ref, w2_ref, a0_ref, a2_ref, g2_ref, kk_ref, ka_ref,
         r_out, lw_out, k_out, v_out, kk_out, a_out, g_out) = refs
    x = cols_ref[...]
    starts_seq = (pl.program_id(0) % tiles_per_seq) == 0
    last_prev = jnp.where(starts_seq, 0.0, prev_ref[7:8, :])
    first_row = lax.broadcasted_iota(jnp.int32, x.shape, 0) == 0
    prev = jnp.where(first_row, last_prev, pltpu.roll(x, 1, 0))
    x = x + (prev - x) * mu_ref[...]
    r = x[:, :WIDTH]
    k = x[:, WIDTH:2 * WIDTH]
    v = x[:, 2 * WIDTH:3 * WIDTH]
    o = 3 * WIDTH
    wd = x[:, o:o + RWKV_LORA_PAD]
    ad = x[:, o + RWKV_LORA_PAD:o + 2 * RWKV_LORA_PAD]
    gd = x[:, o + 2 * RWKV_LORA_PAD:]
    wpre = w0_ref[...] + _dot(jnp.tanh(wd).astype(BF16), w2_ref[...])
    w = -jax.nn.softplus(-wpre) - 0.5
    lw_out[...] = -jnp.exp(w)
    a = jax.nn.sigmoid(a0_ref[...] + _dot(ad.astype(BF16), a2_ref[...]))
    g_out[...] = _dot(jax.nn.sigmoid(gd).astype(BF16), g2_ref[...])
    if has_vres:
        low = _dot(v.astype(BF16), v1_ref[...])
        mix = jax.nn.sigmoid(v0_ref[...] + _dot(low.astype(BF16), v2_ref[...]))
        v = v + (vfirst_ref[...] - v) * mix
    r_out[...] = r
    v_out[...] = v
    kk_out[...] = k * kk_ref[...]
    k_out[...] = k * (1.0 + (a - 1.0) * ka_ref[...])
    a_out[...] = a


def rwkv_prep(cols, seq, mu, w0, w2, a0, a2, g2, k_k, k_a, vres):
    m = cols.shape[0]
    tm = _pick(seq, (256, 128, 64, 32, 16, 8))
    row = lambda t: t.reshape(1, -1).astype(F32)
    args = [cols, cols, row(mu), row(w0), w2, row(a0), a2, g2, row(k_k), row(k_a)]
    full = lambda arr: pl.BlockSpec(arr.shape, lambda i: (0, 0))
    tile = pl.BlockSpec((tm, WIDTH), lambda i: (i, 0))
    in_specs = [pl.BlockSpec((tm, RWKV_IN), lambda i: (i, 0)),
                pl.BlockSpec((8, RWKV_IN), lambda i: (jnp.maximum(i * (tm // 8) - 1, 0), 0))]
    in_specs += [full(t) for t in args[2:]]
    if vres is not None:
        v0, v1, v2, v_first = vres
        extra = [row(v0), v1, v2]
        args += extra + [v_first]
        in_specs += [full(t) for t in extra] + [tile]
    return pl.pallas_call(
        functools.partial(_rwkv_prep_body, tiles_per_seq=seq // tm, has_vres=vres is not None),
        grid=(m // tm,),
        in_specs=in_specs,
        out_specs=[tile] * 7,
        out_shape=[jax.ShapeDtypeStruct((m, WIDTH), F32)] * 7,
        compiler_params=_params("parallel"),
        name="rwkv_prep",
    )(*args)


def _halves(x):
    return x[:, :HEAD_DIM], x[:, HEAD_DIM:]


def _head_sum(x, left):
    s0 = jnp.sum(jnp.where(left, x, 0.0), axis=-1, keepdims=True)
    s1 = jnp.sum(jnp.where(left, 0.0, x), axis=-1, keepdims=True)
    return jnp.where(left, s0, s1)


def _dot3(x, y, dot):
    xh, xl = _split(x)
    yh, yl = _split(y)
    return dot(xh, yh) + (dot(xh, yl) + dot(xl, yh))


def _rwkv_chunk_terms(r, lw, k, v, kk_raw, a):
    c = r.shape[0]
    left = lax.broadcasted_iota(jnp.int32, (c, LANES), 1) < HEAD_DIM
    trow = lax.broadcasted_iota(jnp.int32, (c, LANES), 0)
    kk = kk_raw / jnp.maximum(jnp.sqrt(_head_sum(kk_raw * kk_raw, left)), 1e-12)
    beta = kk * a
    cl = lw
    shift = 1
    while shift < c:
        cl = cl + jnp.where(trow >= shift, pltpu.roll(cl, shift, 0), 0.0)
        shift *= 2
    a_t = -kk * jnp.exp(cl - lw)
    r_t = r * jnp.exp(cl)
    inv = jnp.exp(-cl)
    cl_end = cl[c - 1:c, :]
    tail = jnp.exp(cl_end - cl)
    full = (a_t, r_t, beta * inv, k * inv, v, beta * tail, k * tail, jnp.exp(cl_end))
    return [tuple(_halves(x)[hh] for x in full) for hh in range(2)]


def _rwkv_masks(c):
    ti = lax.broadcasted_iota(jnp.int32, (c, c), 0)
    tj = lax.broadcasted_iota(jnp.int32, (c, c), 1)
    merges = []
    half = 1
    while half < c:
        merges.append((ti // (2 * half) == tj // (2 * half)) & (ti % (2 * half) >= half)
                      & (tj % (2 * half) < half))
        half *= 2
    return ti > tj, ti >= tj, ti == tj, merges


def _rwkv_chain(out, head_terms, masks):
    at_h, rt_h, bt_h, kt_h, v_h, bbar_h, kbar_h, gend_h = head_terms
    strict, incl, eye, merges = masks
    c = at_h.shape[0]
    ar = jnp.concatenate([at_h, rt_h], axis=0).astype(BF16)
    bk = jnp.concatenate([bt_h, kt_h], axis=0).astype(BF16)
    pair = _dot_nt(ar, bk)
    yield
    lower = jnp.where(strict, pair[:c, :c], 0.0)
    akv = _dot(jnp.where(strict, pair[:c, c:], 0.0).astype(BF16), v_h.astype(BF16))
    tinv = eye.astype(F32) + jnp.where(merges[0], lower, 0.0)
    for off in merges[1:]:
        tb = tinv.astype(BF16)
        step = _dot(tb, jnp.where(off, lower, 0.0).astype(BF16))
        yield
        tinv = tinv + _dot(step.astype(BF16), tb)
        yield
    solved = _dot(tinv.astype(BF16), jnp.concatenate([at_h, akv], axis=1).astype(BF16))
    yield
    ta, w = _halves(solved)
    rb = jnp.where(incl, pair[c:, :c], 0.0).astype(BF16)
    rk = jnp.where(incl, pair[c:, c:], 0.0).astype(BF16)
    wv = jnp.concatenate([w, v_h], axis=0).astype(BF16)
    q = rt_h + _dot(rb, ta.astype(BF16))
    y0 = _dot(jnp.concatenate([rb, rk], axis=1), wv)
    g = jnp.where(eye, gend_h, 0.0) + _dot_tn(ta.astype(BF16), bbar_h.astype(BF16))
    d = _dot_tn(wv, jnp.concatenate([bbar_h, kbar_h], axis=0).astype(BF16))
    out.append((q, y0, g, d))


def _lockstep(chains):
    live = list(chains)
    while live:
        still = []
        for ch in live:
            try:
                next(ch)
                still.append(ch)
            except StopIteration:
                pass
        live = still


def _rwkv_scan_body(r_ref, lw_ref, k_ref, v_ref, kk_ref, a_ref, g_ref, lng_ref, lnb_ref, rk_ref,
                    o_ref, state_ref, *, chunk):
    @pl.when(pl.program_id(2) == 0)
    def _():
        state_ref[...] = jnp.zeros_like(state_ref)

    n_chunks = r_ref.shape[1] // chunk
    left = lax.broadcasted_iota(jnp.int32, (chunk, LANES), 1) < HEAD_DIM
    masks = _rwkv_masks(chunk)
    rows = [slice(ci * chunk, (ci + 1) * chunk) for ci in range(n_chunks)]
    results = [[[], []] for _ in range(n_chunks)]
    chains = []
    for ci in range(n_chunks):
        terms = _rwkv_chunk_terms(r_ref[0, rows[ci], :], lw_ref[0, rows[ci], :], k_ref[0, rows[ci], :],
                                  v_ref[0, rows[ci], :], kk_ref[0, rows[ci], :], a_ref[0, rows[ci], :])
        chains += [_rwkv_chain(results[ci][hh], terms[hh], masks) for hh in range(2)]
    _lockstep(chains)

    state = [state_ref[0], state_ref[1]]
    for ci in range(n_chunks):
        ys = []
        for hh in range(2):
            q, y0, g, d = results[ci][hh][0]
            ys.append(_dot3(q, state[hh], _dot_nt) + y0)
            state[hh] = _dot3(state[hh], g, _dot) + d
        y = jnp.concatenate(ys, axis=1)
        r, k, v = r_ref[0, rows[ci], :], k_ref[0, rows[ci], :], v_ref[0, rows[ci], :]
        mean = _head_sum(y, left) * (1.0 / HEAD_DIM)
        var = _head_sum(jnp.square(y - mean), left) * (1.0 / HEAD_DIM)
        y = (y - mean) * lax.rsqrt(var + RWKV_GN_EPS) * lng_ref[...] + lnb_ref[...]
        bonus = _head_sum(r * k * rk_ref[...], left) * v
        o_ref[0, rows[ci], :] = ((y + bonus) * g_ref[0, rows[ci], :]).astype(o_ref.dtype)
    state_ref[0] = state[0]
    state_ref[1] = state[1]


def rwkv_scan(r, lw, k, v, kk, a, g, ln_g, ln_b, r_k, batch, seq):
    chunk = min(RWKV_CHUNK, seq)
    rows = _pick(seq, (512, 256, 128, 64, 32, 16, 8))
    hp = WIDTH // LANES
    seq_spec = pl.BlockSpec((1, rows, LANES), lambda b, h, c: (b, c, h))
    par_spec = pl.BlockSpec((1, LANES), lambda b, h, c: (0, h))
    shaped = [t.reshape(batch, seq, WIDTH) for t in (r, lw, k, v, kk, a, g)]
    pars = [t.reshape(1, WIDTH).astype(F32) for t in (ln_g, ln_b, r_k)]
    out = pl.pallas_call(
        functools.partial(_rwkv_scan_body, chunk=chunk),
        grid=(batch, hp, seq // rows),
        in_specs=[seq_spec] * 7 + [par_spec] * 3,
        out_specs=seq_spec,
        out_shape=jax.ShapeDtypeStruct((batch, seq, WIDTH), BF16),
        scratch_shapes=[pltpu.VMEM((2, HEAD_DIM, HEAD_DIM), F32)],
        compiler_params=_params("parallel", "parallel", "arbitrary"),
        name="rwkv_scan",
    )(*shaped, *pars)
    return out.reshape(batch * seq, WIDTH)


def _merge_body(h, a1, a2, a3, g1, g2, g3, w1, w2, w3, o_ref):
    hh = h[...]
    acc = jax.nn.sigmoid(_dot(hh, g1[...])) * _dot(a1[...], w1[...])
    acc += jax.nn.sigmoid(_dot(hh, g2[...])) * _dot(a2[...], w2[...])
    acc += jax.nn.sigmoid(_dot(hh, g3[...])) * _dot(a3[...], w3[...])
    o_ref[...] = acc.astype(o_ref.dtype)


def merge_branches(h, outs, w_gates, weights):
    m, d = h.shape
    tm = _pick(m, (512, 256, 128, 64, 32, 16, 8))
    tn = _pick(d, (512, 256, 128))
    nb = d // tn
    h_spec = pl.BlockSpec((tm, d), lambda i, j: (i, 0))
    a_spec = pl.BlockSpec((tm, WIDTH), lambda i, j: (i, 0))
    w_spec = pl.BlockSpec((WIDTH, tn), lambda i, j: (0, j))
    g_specs = [pl.BlockSpec((d, tn), lambda i, j, b=b: (0, j + b * nb)) for b in range(3)]
    return pl.pallas_call(
        _merge_body,
        grid=(m // tm, nb),
        in_specs=[h_spec] + [a_spec] * 3 + g_specs + [w_spec] * 3,
        out_specs=pl.BlockSpec((tm, tn), lambda i, j: (i, j)),
        out_shape=jax.ShapeDtypeStruct((m, d), BF16),
        compiler_params=_params("parallel", "parallel"),
        name="merge_branches",
    )(h, *outs, w_gates, w_gates, w_gates, *weights)


ROUTER_PAD = 128


def _slab_cols(ref, n, slabs, first=0, pitch=None):
    pitch = slabs if pitch is None else pitch
    return jnp.concatenate([ref[pl.ds(first + s, n, stride=pitch), :] for s in range(slabs)], axis=1)


def _store_slabs(ref, value, slabs):
    n = value.shape[0]
    for s in range(slabs):
        ref[pl.ds(s, n, stride=slabs), :] = value[:, s * LANES:(s + 1) * LANES].astype(ref.dtype)


def _pack_bf16_pairs(h):
    half = h.shape[1] // 2
    bits = pltpu.bitcast(h.astype(BF16).astype(F32), jnp.uint32)
    return (bits[:, :half] >> 16) | bits[:, half:]


def _unpack_bf16_pairs(words):
    lo = pltpu.bitcast(words << 16, F32)
    hi = pltpu.bitcast(words & jnp.uint32(0xFFFF0000), F32)
    return jnp.concatenate([lo, hi], axis=1).astype(BF16)


def _router_body(x_ref, g_ref, w_ref, b_ref, h_ref, logit_ref, *, slabs):
    x = x_ref[...]
    y = x * lax.rsqrt(jnp.mean(x * x, axis=-1, keepdims=True) + NORM_EPS)
    h = y * g_ref[...]
    _store_slabs(h_ref, _pack_bf16_pairs(h), slabs)
    logit_ref[...] = jnp.dot(h, w_ref[...], preferred_element_type=F32, precision=HIGHEST) + b_ref[...]


def router(x, g, w_pad, b_pad):
    m, d = x.shape
    slabs = d // (2 * LANES)
    tm = _pick(m, (256, 128, 64, 32, 16, 8))
    return pl.pallas_call(
        functools.partial(_router_body, slabs=slabs),
        grid=(m // tm,),
        in_specs=[pl.BlockSpec((tm, d), lambda i: (i, 0)),
                  pl.BlockSpec((1, d), lambda i: (0, 0)),
                  pl.BlockSpec((d, ROUTER_PAD), lambda i: (0, 0)),
                  pl.BlockSpec((1, ROUTER_PAD), lambda i: (0, 0))],
        out_specs=[pl.BlockSpec((tm * slabs, LANES), lambda i: (i, 0)),
                   pl.BlockSpec((tm, ROUTER_PAD), lambda i: (i, 0))],
        out_shape=[jax.ShapeDtypeStruct((m * slabs, LANES), jnp.uint32),
                   jax.ShapeDtypeStruct((m, ROUTER_PAD), F32)],
        compiler_params=_params("parallel"),
        name="ffn_norm_router",
    )(x, g.reshape(1, d).astype(F32), w_pad, b_pad)


def _row_gather(idx_ref, first, n, src_ref, buf_ref, sem, slabs):
    def copy(r):
        src = pl.multiple_of(idx_ref[first + r] * slabs, slabs)
        return pltpu.make_async_copy(src_ref.at[pl.ds(src, slabs)],
                                     buf_ref.at[pl.ds(r * slabs, slabs)], sem)

    def start():
        lax.fori_loop(0, n, lambda r, c: (copy(r).start(), c)[1], 0, unroll=8)

    def wait():
        lax.fori_loop(0, n, lambda r, c: (copy(r).wait(), c)[1], 0, unroll=8)

    return start, wait


def _expert_body(be_ref, nb_ref, tok_ref, h_ref, wg_ref, wu_ref, wd_ref, o_ref,
                 xbuf, wg_s, wu_s, wd_s, sem, *, in_slabs, slabs):
    i = pl.program_id(0)
    n_blocks = nb_ref[0]
    slot = i % 2

    def gather(block, slot):
        return _row_gather(tok_ref, block * MOE_ROWS, MOE_ROWS, h_ref, xbuf.at[slot], sem.at[slot], in_slabs)

    @pl.when(i == 0)
    def _():
        gather(0, 0)[0]()

    @pl.when(i + 1 < n_blocks)
    def _():
        gather(i + 1, 1 - slot)[0]()

    e = be_ref[i]

    @pl.when((i == 0) | (e != be_ref[jnp.maximum(i - 1, 0)]))
    def _():
        wg_s[...] = wg_ref[0, 0].astype(BF16)
        wu_s[...] = wu_ref[0, 0].astype(BF16)
        wd_s[...] = wd_ref[0, 0].astype(BF16)

    @pl.when(i < n_blocks)
    def _():
        gather(i, slot)[1]()
        x = _unpack_bf16_pairs(_slab_cols(xbuf.at[slot], MOE_ROWS, in_slabs))
        gate = _dot(x, wg_s[...])
        up = _dot(x, wu_s[...])
        mid = (jax.nn.silu(gate) * up).astype(BF16)
        _store_slabs(o_ref, _dot(mid, wd_s[...]), slabs)

    @pl.when(i >= n_blocks)
    def _():
        o_ref[...] = jnp.zeros_like(o_ref)


def expert_ffn(h_rows, row_tok, block_e, n_blocks, layer, w_gate, w_up, w_down):
    _, n_exp, d, ff = w_gate.shape
    slabs = d // LANES
    in_slabs = d // (2 * LANES)
    p = row_tok.shape[0]
    return pl.pallas_call(
        functools.partial(_expert_body, in_slabs=in_slabs, slabs=slabs),
        grid_spec=pltpu.PrefetchScalarGridSpec(
            num_scalar_prefetch=3,
            grid=(p // MOE_ROWS,),
            in_specs=[pl.BlockSpec(memory_space=pl.ANY),
                      pl.BlockSpec((1, 1, d, ff), lambda i, be, nb, tok: (layer, be[i], 0, 0)),
                      pl.BlockSpec((1, 1, d, ff), lambda i, be, nb, tok: (layer, be[i], 0, 0)),
                      pl.BlockSpec((1, 1, ff, d), lambda i, be, nb, tok: (layer, be[i], 0, 0))],
            out_specs=pl.BlockSpec((MOE_ROWS * slabs, LANES), lambda i, be, nb, tok: (i, 0)),
            scratch_shapes=[pltpu.VMEM((2, MOE_ROWS * in_slabs, LANES), jnp.uint32),
                            pltpu.VMEM((d, ff), BF16), pltpu.VMEM((d, ff), BF16),
                            pltpu.VMEM((ff, d), BF16),
                            pltpu.SemaphoreType.DMA((2,))]),
        out_shape=jax.ShapeDtypeStruct((p * slabs, LANES), F32),
        compiler_params=_params("arbitrary"),
        name="expert_ffn",
    )(block_e, n_blocks, row_tok, h_rows, w_gate, w_up, w_down)


def _combine_body(dest_ref, x_ref, gate_ref, y_ref, o_ref, ybuf, sem, *, tm, slabs):
    i = pl.program_id(0)
    slot = i % 2

    def gather(step, slot):
        return _row_gather(dest_ref, step * tm * TOP_K, tm * TOP_K, y_ref, ybuf.at[slot], sem.at[slot], slabs)

    @pl.when(i == 0)
    def _():
        gather(0, 0)[0]()

    @pl.when(i + 1 < pl.num_programs(0))
    def _():
        gather(i + 1, 1 - slot)[0]()

    gather(i, slot)[1]()
    buf = ybuf.at[slot]
    gate = gate_ref[...]
    g0, g1 = gate[:, 0:1], gate[:, 1:2]
    for s in range(slabs):
        y0 = buf[pl.ds(s, tm, stride=TOP_K * slabs), :]
        y1 = buf[pl.ds(slabs + s, tm, stride=TOP_K * slabs), :]
        cols = slice(s * LANES, (s + 1) * LANES)
        o_ref[:, cols] = x_ref[:, cols] + (g0 * y0 + g1 * y1)


def moe_combine(x, y_rows, dest, gate):
    m, d = x.shape
    slabs = d // LANES
    tm = _pick(m, (256, 128, 64, 32, 16, 8))
    tile = lambda i, dest: (i, 0)
    return pl.pallas_call(
        functools.partial(_combine_body, tm=tm, slabs=slabs),
        grid_spec=pltpu.PrefetchScalarGridSpec(
            num_scalar_prefetch=1,
            grid=(m // tm,),
            in_specs=[pl.BlockSpec((tm, d), tile), pl.BlockSpec((tm, TOP_K), tile),
                      pl.BlockSpec(memory_space=pl.ANY)],
            out_specs=pl.BlockSpec((tm, d), tile),
            scratch_shapes=[pltpu.VMEM((2, tm * TOP_K * slabs, LANES), F32),
                            pltpu.SemaphoreType.DMA((2,))]),
        out_shape=jax.ShapeDtypeStruct((m, d), F32),
        compiler_params=_params("arbitrary"),
        name="moe_combine",
    )(dest.reshape(m * TOP_K), x, gate, y_rows)


def _routing_tables(logits):
    t = logits.shape[0]
    n_assign = t * TOP_K
    tok = jnp.arange(t)
    g_logits = logits[:, :N_GROUPS]
    grp = jnp.argmax(g_logits, axis=-1)
    p_grp = jax.nn.softmax(g_logits, axis=-1)[tok, grp]
    e_logits = logits[:, N_GROUPS:N_GROUPS + N_EXPERTS].reshape(t, N_GROUPS, EXPERTS_PER_GROUP)
    p_in, idx_in = lax.top_k(jax.nn.softmax(e_logits[tok, grp], axis=-1), TOP_K)
    gate = p_grp[:, None] * p_in / jnp.sum(p_in, axis=-1, keepdims=True)
    flat_e = (grp[:, None] * EXPERTS_PER_GROUP + idx_in).reshape(n_assign).astype(jnp.int32)
    onehot = (flat_e[:, None] == jnp.arange(N_EXPERTS, dtype=jnp.int32)[None, :]).astype(jnp.int32)
    rank = jnp.sum((jnp.cumsum(onehot, axis=0) - onehot) * onehot, axis=1)
    counts = jnp.sum(onehot, axis=0)
    padded = (counts + MOE_ROWS - 1) // MOE_ROWS * MOE_ROWS
    pend = jnp.cumsum(padded)
    dest = ((pend - padded)[flat_e] + rank).astype(jnp.int32)
    n_blocks_max = -(-n_assign // MOE_ROWS) + N_EXPERTS
    row_tok = jnp.zeros((n_blocks_max * MOE_ROWS,), jnp.int32).at[dest].set(
        jnp.repeat(tok, TOP_K).astype(jnp.int32))
    block_e = jnp.minimum(
        jnp.searchsorted(pend, jnp.arange(n_blocks_max) * MOE_ROWS, side="right"),
        N_EXPERTS - 1).astype(jnp.int32)
    n_blocks = (pend[-1:] // MOE_ROWS).astype(jnp.int32)
    return gate.astype(F32), dest.reshape(t, TOP_K), row_tok, block_e, n_blocks


def hier_moe_residual(x, norm_g, rg_w, rg_b, re_w, re_b, layer, w_gate, w_up, w_down):
    t, d = x.shape
    w_pad = jnp.zeros((d, ROUTER_PAD), F32).at[:, :N_GROUPS].set(rg_w)
    w_pad = w_pad.at[:, N_GROUPS:N_GROUPS + N_EXPERTS].set(re_w)
    b_pad = jnp.zeros((1, ROUTER_PAD), F32).at[0, :N_GROUPS].set(rg_b)
    b_pad = b_pad.at[0, N_GROUPS:N_GROUPS + N_EXPERTS].set(re_b)
    h_rows, logits = router(x, norm_g, w_pad, b_pad)
    gate, dest, row_tok, block_e, n_blocks = _routing_tables(logits)
    y_rows = expert_ffn(h_rows, row_tok, block_e, n_blocks, layer, w_gate, w_up, w_down)
    return moe_combine(x, y_rows, dest, gate)


def _mla_q_weight(w_uq):
    w = w_uq.reshape(MLA_Q_LORA, HEADS, HEAD_DIM + MLA_ROPE_DIM)
    w = jnp.pad(w, ((0, 0), (0, 0), (0, LANES - HEAD_DIM - MLA_ROPE_DIM)))
    return w.reshape(MLA_Q_LORA, HEADS * LANES).astype(BF16)


def _mla_kv_weights(w_ukv):
    w = w_ukv.reshape(MLA_KV_LORA, HEADS, 2 * HEAD_DIM)
    wk = jnp.pad(w[:, :, :HEAD_DIM], ((0, 0), (0, 0), (0, LANES - HEAD_DIM)))
    wv = w[:, :, HEAD_DIM:]
    return (wk.reshape(MLA_KV_LORA, HEADS * LANES).astype(BF16),
            wv.reshape(MLA_KV_LORA, WIDTH).astype(BF16))


def _rope_slot_tables(positions):
    half = MLA_ROPE_DIM // 2
    inv_freq = ROPE_THETA ** (-jnp.arange(0, MLA_ROPE_DIM, 2, dtype=F32) / MLA_ROPE_DIM)
    ang = positions.astype(F32).reshape(-1, 1) * inv_freq[None, :]
    cos, sin = jnp.cos(ang), jnp.sin(ang)
    t = ang.shape[0]
    pad = jnp.zeros((t, LANES - HEAD_DIM - 2 * half), F32)
    cos_t = jnp.concatenate([jnp.ones((t, HEAD_DIM), F32), cos, cos, pad], axis=1)
    sin_t = jnp.concatenate([jnp.zeros((t, HEAD_DIM), F32), -sin, sin, pad], axis=1)
    return cos_t, sin_t


def _mixer_block(x, l, batch, seq, cos_t, sin_t, v_first, p):
    t, d = x.shape
    h = rmsnorm(x, p["attn_norm_g"][l], BF16, name="attn_norm")
    w_in = p["w_in"][l]
    o_rwkv = 3 * WIDTH
    n_rwkv = 3 * WIDTH + RWKV_DECAY_LORA + RWKV_ICLR_LORA + RWKV_GATE_LORA
    o_mla = o_rwkv + n_rwkv
    n_mla = MLA_Q_LORA + MLA_KV_LORA + MLA_ROPE_DIM
    o_gate = o_mla + n_mla

    qscale = jnp.concatenate([jnp.full((1, WIDTH), HEAD_DIM ** -0.5, F32), jnp.ones((1, 2 * WIDTH), F32)], axis=1)
    (qkv,) = matmul(h, w_in[:, :o_rwkv].astype(BF16), [BF16],
                    epilogue=lambda acc, s: (acc * s,), extras=[(qscale, "col", 0)], name="proj_sb")
    o_sb = sb_attention(qkv, batch, seq)

    wr = w_in[:, o_rwkv:o_mla]
    zpad = jnp.zeros((d, RWKV_LORA_PAD - RWKV_DECAY_LORA), F32)
    c0 = 3 * WIDTH
    w_rwkv = jnp.concatenate([wr[:, :c0], wr[:, c0:c0 + RWKV_DECAY_LORA], zpad,
                              wr[:, c0 + RWKV_DECAY_LORA:c0 + 2 * RWKV_DECAY_LORA], zpad,
                              wr[:, c0 + 2 * RWKV_DECAY_LORA:]], axis=1).astype(BF16)
    (cols,) = matmul(h, w_rwkv, [F32], name="proj_rwkv")
    mu = p["rwkv_mu"][l]
    mpad = jnp.zeros((RWKV_LORA_PAD - RWKV_DECAY_LORA,), F32)
    mu_pad = jnp.concatenate([mu[:c0], mu[c0:c0 + RWKV_DECAY_LORA], mpad,
                              mu[c0 + RWKV_DECAY_LORA:c0 + 2 * RWKV_DECAY_LORA], mpad,
                              mu[c0 + 2 * RWKV_DECAY_LORA:]])
    rpad = ((0, RWKV_LORA_PAD - RWKV_DECAY_LORA), (0, 0))
    vres = None
    if l > 0:
        vres = (p["rwkv_v0"][l - 1], p["rwkv_v1"][l - 1].astype(BF16), p["rwkv_v2"][l - 1].astype(BF16), v_first)
    r, lw, k, v, kk, a, g = rwkv_prep(
        cols, seq, mu_pad, p["rwkv_w0"][l], jnp.pad(p["rwkv_w2"][l], rpad).astype(BF16),
        p["rwkv_a0"][l], jnp.pad(p["rwkv_a2"][l], rpad).astype(BF16), p["rwkv_g2"][l].astype(BF16),
        p["rwkv_k_k"][l], p["rwkv_k_a"][l], vres)
    if l == 0:
        v_first = v
    o_rw = rwkv_scan(r, lw, k, v, kk, a, g, p["rwkv_ln_g"][l], p["rwkv_ln_b"][l], p["rwkv_r_k"][l],
                     batch, seq)

    wm = w_in[:, o_mla:o_gate]
    w_mla = jnp.concatenate([wm[:, :MLA_Q_LORA + MLA_KV_LORA], jnp.zeros((d, HEAD_DIM), F32),
                             wm[:, MLA_Q_LORA + MLA_KV_LORA:],
                             jnp.zeros((d, LANES - HEAD_DIM - MLA_ROPE_DIM), F32)], axis=1).astype(BF16)
    (mcols,) = matmul(h, w_mla, [F32], name="proj_mla")
    cq = rmsnorm(mcols, p["mla_q_norm_g"][l], BF16, width=MLA_Q_LORA, col_block=0, name="mla_q_norm")
    ckv = rmsnorm(mcols, p["mla_kv_norm_g"][l], BF16, width=MLA_KV_LORA,
                  col_block=MLA_Q_LORA // MLA_KV_LORA, name="mla_kv_norm")
    mla_scale = (HEAD_DIM + MLA_ROPE_DIM) ** -0.5
    (q_mla,) = matmul(cq, _mla_q_weight(p["mla_w_uq"][l]), [BF16],
                      epilogue=lambda acc, c, s: (_rope_slots(acc, c, s) * mla_scale,),
                      extras=[(cos_t, "row", 0), (sin_t, "row", 0)], name="mla_q_up")
    wk, wv = _mla_kv_weights(p["mla_w_ukv"][l])
    kpe_block = (MLA_Q_LORA + MLA_KV_LORA) // LANES

    def k_epilogue(acc, kpe, c, s):
        kr = _rope_slots(kpe, c, s)
        return (acc + jnp.concatenate([kr] * (acc.shape[1] // LANES), axis=1),)

    kpe = mcols[:, kpe_block * LANES:(kpe_block + 1) * LANES]
    (k_mla,) = matmul(ckv, wk, [BF16], epilogue=k_epilogue,
                      extras=[(kpe, "row", 0), (cos_t, "row", 0), (sin_t, "row", 0)], name="mla_k_up")
    (v_mla,) = matmul(ckv, wv, [BF16], name="mla_v_up")
    o_mla_out = mla_attention(q_mla, k_mla, v_mla, batch, seq)

    merged = merge_branches(h, [o_sb, o_rw, o_mla_out], w_in[:, o_gate:].astype(BF16),
                            [p["w_br_sb"][l].astype(BF16), p["w_br_rwkv"][l].astype(BF16),
                             p["w_br_mla"][l].astype(BF16)])
    (x,) = matmul(merged, p["w_out"][l].astype(BF16), [F32],
                  epilogue=lambda acc, res: (res + acc,), extras=[(x, "tile", 0)], name="out_proj")
    return x, v_first


def kernel(x, positions, attn_norm_g, w_in, rwkv_mu, rwkv_w0, rwkv_w2, rwkv_a0, rwkv_a2, rwkv_g2, rwkv_k_k, rwkv_k_a, rwkv_r_k, rwkv_ln_g, rwkv_ln_b, rwkv_v0, rwkv_v1, rwkv_v2, mla_q_norm_g, mla_w_uq, mla_kv_norm_g, mla_w_ukv, w_br_sb, w_br_rwkv, w_br_mla, w_out, ffn_norm_g, router_group_w, router_group_b, router_expert_w, router_expert_b, expert_w_gate, expert_w_up, expert_w_down, final_norm_g):
    p = dict(attn_norm_g=attn_norm_g, w_in=w_in, rwkv_mu=rwkv_mu, rwkv_w0=rwkv_w0, rwkv_w2=rwkv_w2,
             rwkv_a0=rwkv_a0, rwkv_a2=rwkv_a2, rwkv_g2=rwkv_g2, rwkv_k_k=rwkv_k_k, rwkv_k_a=rwkv_k_a,
             rwkv_r_k=rwkv_r_k, rwkv_ln_g=rwkv_ln_g, rwkv_ln_b=rwkv_ln_b, rwkv_v0=rwkv_v0,
             rwkv_v1=rwkv_v1, rwkv_v2=rwkv_v2, mla_q_norm_g=mla_q_norm_g, mla_w_uq=mla_w_uq,
             mla_kv_norm_g=mla_kv_norm_g, mla_w_ukv=mla_w_ukv, w_br_sb=w_br_sb, w_br_rwkv=w_br_rwkv,
             w_br_mla=w_br_mla, w_out=w_out)
    batch, seq, d = x.shape
    depth = w_in.shape[0]
    cos_t, sin_t = _rope_slot_tables(positions)
    xt = x.reshape(batch * seq, d)
    v_first = None
    for l in range(depth):
        xt, v_first = _mixer_block(xt, l, batch, seq, cos_t, sin_t, v_first, p)
        xt = hier_moe_residual(xt, ffn_norm_g[l], router_group_w[l], router_group_b[l],
                               router_expert_w[l], router_expert_b[l],
                               l, expert_w_gate, expert_w_up, expert_w_down)
    return rmsnorm(xt, final_norm_g, x.dtype, name="final_norm").reshape(batch, seq, d)
```

```python
import functools

import jax
import jax.numpy as jnp
from jax import lax
from jax.experimental import pallas as pl
from jax.experimental.pallas import tpu as pltpu

F32 = jnp.float32
BF16 = jnp.bfloat16
HIGHEST = lax.Precision.HIGHEST

NORM_EPS = 1e-6
HEADS = 16
HEAD_DIM = 64
WIDTH = HEADS * HEAD_DIM
LANES = 128
RWKV_DECAY_LORA = 96
RWKV_ICLR_LORA = 96
RWKV_GATE_LORA = 256
RWKV_LORA_PAD = 128
RWKV_GN_EPS = 64e-5
RWKV_CHUNK = 64
MLA_Q_LORA = 768
MLA_KV_LORA = 256
MLA_ROPE_DIM = 32
ROPE_THETA = 10000.0
N_GROUPS = 8
EXPERTS_PER_GROUP = 8
N_EXPERTS = N_GROUPS * EXPERTS_PER_GROUP
TOP_K = 2
MOE_ROWS = 256
VMEM_LIMIT = 48 * 1024 * 1024


def _params(*sem):
    return pltpu.CompilerParams(dimension_semantics=sem, vmem_limit_bytes=VMEM_LIMIT)


def _pick(n, cands):
    for c in cands:
        if n % c == 0:
            return c
    raise ValueError(f"no tile for {n} in {cands}")


def _dot(x, y):
    return jnp.dot(x, y, preferred_element_type=F32)


def _dot_nt(x, y):
    return lax.dot_general(x, y, (((1,), (1,)), ((), ())), preferred_element_type=F32)


def _dot_tn(x, y):
    return lax.dot_general(x, y, (((0,), (0,)), ((), ())), preferred_element_type=F32)


def _split(x):
    hi = x.astype(BF16)
    return hi, (x - hi.astype(F32)).astype(BF16)


def _rmsnorm_body(x_ref, g_ref, o_ref):
    x = x_ref[...].astype(F32)
    y = x * lax.rsqrt(jnp.mean(x * x, axis=-1, keepdims=True) + NORM_EPS)
    o_ref[...] = (y * g_ref[...]).astype(o_ref.dtype)


def rmsnorm(x, g, out_dtype, *, width=None, col_block=0, name="rmsnorm"):
    m = x.shape[0]
    width = x.shape[1] if width is None else width
    tm = _pick(m, (512, 256, 128, 64, 32, 16, 8))
    return pl.pallas_call(
        _rmsnorm_body,
        grid=(m // tm,),
        in_specs=[pl.BlockSpec((tm, width), lambda i: (i, col_block)),
                  pl.BlockSpec((1, width), lambda i: (0, 0))],
        out_specs=pl.BlockSpec((tm, width), lambda i: (i, 0)),
        out_shape=jax.ShapeDtypeStruct((m, width), out_dtype),
        compiler_params=_params("parallel"),
        name=name,
    )(x, g.reshape(1, width).astype(F32))


def _mm_body(*refs, n_extra, epilogue):
    a_ref, w_ref = refs[:2]
    extra = [r[...] for r in refs[2:2 + n_extra]]
    outs = refs[2 + n_extra:]
    acc = _dot(a_ref[...], w_ref[...])
    res = epilogue(acc, *extra)
    for o, r in zip(outs, res):
        o[...] = r.astype(o.dtype)


def matmul(a, w, out_dtypes, *, epilogue=None, extras=(), a_col_block=0, name="matmul"):
    m = a.shape[0]
    k, n = w.shape
    tm = _pick(m, (1024, 512, 256, 128, 64, 32, 16, 8))
    tn = n if n <= 1280 else _pick(n, (512, 256, 128))
    if epilogue is None:
        epilogue = lambda acc: (acc,)
    in_specs = [pl.BlockSpec((tm, k), lambda i, j: (i, a_col_block)),
                pl.BlockSpec((k, tn), lambda i, j: (0, j))]
    args = [a, w]
    for arr, kind, off in extras:
        if kind == "row":
            in_specs.append(pl.BlockSpec((tm, arr.shape[1]), lambda i, j: (i, 0)))
        elif kind == "col":
            in_specs.append(pl.BlockSpec((1, tn), lambda i, j: (0, j)))
        else:
            in_specs.append(pl.BlockSpec((tm, tn), lambda i, j, off=off: (i, j + off)))
        args.append(arr)
    return pl.pallas_call(
        functools.partial(_mm_body, n_extra=len(extras), epilogue=epilogue),
        grid=(m // tm, n // tn),
        in_specs=in_specs,
        out_specs=[pl.BlockSpec((tm, tn), lambda i, j: (i, j)) for _ in out_dtypes],
        out_shape=[jax.ShapeDtypeStruct((m, n), dt) for dt in out_dtypes],
        compiler_params=_params("parallel", "parallel"),
        name=name,
    )(*args)


def _rope_slots(x, cos_t, sin_t):
    n = x.shape[-1]
    reps = n // LANES
    lane = lax.broadcasted_iota(jnp.int32, x.shape, 1) % LANES
    from_hi = pltpu.roll(x, n - MLA_ROPE_DIM // 2, 1)
    from_lo = pltpu.roll(x, MLA_ROPE_DIM // 2, 1)
    swapped = jnp.where(lane < HEAD_DIM + MLA_ROPE_DIM // 2, from_hi, from_lo)
    if reps > 1:
        cos_t = jnp.concatenate([cos_t] * reps, axis=1)
        sin_t = jnp.concatenate([sin_t] * reps, axis=1)
    return x * cos_t + swapped * sin_t


def _sb_body(q_ref, k_ref, v_ref, o_ref, *, t, cw):
    qi = pl.program_id(2)
    n_sub = t // cw
    rows = lax.broadcasted_iota(jnp.int32, (t, cw), 0)
    cols = lax.broadcasted_iota(jnp.int32, (t, cw), 1)
    later = (lax.broadcasted_iota(jnp.int32, (cw, cw), 0)
             > lax.broadcasted_iota(jnp.int32, (cw, cw), 1)).astype(BF16)
    later2 = jnp.concatenate([later, later], axis=0)
    heads = [slice(hh * HEAD_DIM, (hh + 1) * HEAD_DIM) for hh in range(2)]
    q = [q_ref[0, :, h] for h in heads]

    def block(ks, carry, diagonal):
        z = [_dot_nt(q[hh], k_ref[0, pl.ds(ks, t), h]) for hh, h in enumerate(heads)]
        acc = [carry[0], carry[2]]
        run = [carry[1], carry[3]]
        weights = [[None] * n_sub for _ in heads]
        for sub in reversed(range(n_sub)):
            for hh in range(2):
                zz = z[hh][:, sub * cw:(sub + 1) * cw]
                neg_abs = pltpu.bitcast(pltpu.bitcast(zz, jnp.uint32) | jnp.uint32(0x80000000), F32)
                sp = jnp.maximum(zz, 0.0) + jnp.log(1.0 + jnp.exp(neg_abs))
                if diagonal:
                    before = (cols + sub * cw) < rows
                    mass = jnp.where(before, sp, 0.0)
                else:
                    mass = sp
                hi, lo = _split(mass)
                after = _dot(jnp.concatenate([hi, lo], axis=1), later2)
                wgt = jnp.exp(((zz - sp) - after) - run[hh])
                if diagonal:
                    wgt = jnp.where(before, wgt, 0.0)
                weights[hh][sub] = wgt.astype(BF16)
                run[hh] = run[hh] + (after[:, :1] + mass[:, :1])
        for hh, h in enumerate(heads):
            acc[hh] = acc[hh] + _dot(jnp.concatenate(weights[hh], axis=1), v_ref[0, pl.ds(ks, t), h])
        return acc[0], run[0], acc[1], run[1]

    zeros = (jnp.zeros((t, HEAD_DIM), F32), jnp.zeros((t, 1), F32)) * 2
    carry = block(pl.multiple_of(qi * t, t), zeros, True)
    carry = lax.fori_loop(
        0, qi, lambda it, c: block(pl.multiple_of((qi - 1 - it) * t, t), c, False), carry)
    o_ref[0] = jnp.concatenate([carry[0], carry[2]], axis=1).astype(o_ref.dtype)


def sb_attention(qkv, batch, seq):
    t = _pick(seq, (512, 256, 128))
    x = qkv.reshape(batch, seq, 3 * WIDTH)
    hp = WIDTH // LANES
    out = pl.pallas_call(
        functools.partial(_sb_body, t=t, cw=min(t, 256)),
        grid=(batch, hp, seq // t),
        in_specs=[pl.BlockSpec((1, t, LANES), lambda b, h, i: (b, i, h)),
                  pl.BlockSpec((1, seq, LANES), lambda b, h, i: (b, 0, hp + h)),
                  pl.BlockSpec((1, seq, LANES), lambda b, h, i: (b, 0, 2 * hp + h))],
        out_specs=pl.BlockSpec((1, t, LANES), lambda b, h, i: (b, i, h)),
        out_shape=jax.ShapeDtypeStruct((batch, seq, WIDTH), BF16),
        compiler_params=_params("parallel", "parallel", "parallel"),
        name="sb_attention",
    )(x, x, x)
    return out.reshape(batch * seq, WIDTH)


def _mla_body(q_ref, k_ref, v_ref, o_ref, *, t):
    qi = pl.program_id(2)
    rows = lax.broadcasted_iota(jnp.int32, (t, t), 0)
    cols = lax.broadcasted_iota(jnp.int32, (t, t), 1)
    q = [q_ref[0, :, hh * LANES:(hh + 1) * LANES] for hh in range(2)]

    def block(ks, carry, diagonal):
        s = [_dot_nt(q[hh], k_ref[0, pl.ds(ks, t), hh * LANES:(hh + 1) * LANES]) for hh in range(2)]
        out = []
        for hh in range(2):
            m, l, acc = carry[3 * hh:3 * hh + 3]
            ss = jnp.where(cols <= rows, s[hh], -1e30) if diagonal else s[hh]
            m_new = jnp.maximum(m, jnp.max(ss, axis=-1, keepdims=True))
            p = jnp.exp(ss - m_new)
            corr = jnp.exp(m - m_new)
            l = corr * l + jnp.sum(p, axis=-1, keepdims=True)
            v = v_ref[0, pl.ds(ks, t), hh * HEAD_DIM:(hh + 1) * HEAD_DIM]
            acc = corr * acc + _dot(p.astype(BF16), v)
            out += [m_new, l, acc]
        return tuple(out)

    init = (jnp.full((t, 1), -1e30, F32), jnp.zeros((t, 1), F32), jnp.zeros((t, HEAD_DIM), F32)) * 2
    carry = block(pl.multiple_of(qi * t, t), init, True)
    carry = lax.fori_loop(0, qi, lambda kb, c: block(pl.multiple_of(kb * t, t), c, False), carry)
    o_ref[0] = jnp.concatenate([carry[2] / carry[1], carry[5] / carry[4]], axis=1).astype(o_ref.dtype)


def mla_attention(q, k, v, batch, seq):
    t = _pick(seq, (512, 256, 128))
    hp = HEADS // 2
    out = pl.pallas_call(
        functools.partial(_mla_body, t=t),
        grid=(batch, hp, seq // t),
        in_specs=[pl.BlockSpec((1, t, 2 * LANES), lambda b, h, i: (b, i, h)),
                  pl.BlockSpec((1, seq, 2 * LANES), lambda b, h, i: (b, 0, h)),
                  pl.BlockSpec((1, seq, LANES), lambda b, h, i: (b, 0, h))],
        out_specs=pl.BlockSpec((1, t, LANES), lambda b, h, i: (b, i, h)),
        out_shape=jax.ShapeDtypeStruct((batch, seq, WIDTH), BF16),
        compiler_params=_params("parallel", "parallel", "parallel"),
        name="mla_attention",
    )(q.reshape(batch, seq, HEADS * LANES), k.reshape(batch, seq, HEADS * LANES),
      v.reshape(batch, seq, WIDTH))
    return out.reshape(batch * seq, WIDTH)


RWKV_IN = 3 * WIDTH + 2 * RWKV_LORA_PAD + RWKV_GATE_LORA


def _rwkv_prep_body(*refs, tiles_per_seq, has_vres):
    if has_vres:
        (cols_ref, prev_ref, mu_ref, w0_ref, w2_ref, a0_ref, a2_ref, g2_ref, kk_ref, ka_ref,
         v0_ref, v1_ref, v2_ref, vfirst_ref,
         r_out, lw_out, k_out, v_out, kk_out, a_out, g_out) = refs
    else:
        (cols_ref, prev_ref, mu_ref, w0_ref, w2_ref, a0_ref, a2_ref, g2_ref, kk_ref, ka_ref,
         r_out, lw_out, k_out, v_out, kk_out, a_out, g_out) = refs
    x = cols_ref[...]
    starts_seq = (pl.program_id(0) % tiles_per_seq) == 0
    last_prev = jnp.where(starts_seq, 0.0, prev_ref[7:8, :])
    first_row = lax.broadcasted_iota(jnp.int32, x.shape, 0) == 0
    prev = jnp.where(first_row, last_prev, pltpu.roll(x, 1, 0))
    x = x + (prev - x) * mu_ref[...]
    r = x[:, :WIDTH]
    k = x[:, WIDTH:2 * WIDTH]
    v = x[:, 2 * WIDTH:3 * WIDTH]
    o = 3 * WIDTH
    wd = x[:, o:o + RWKV_LORA_PAD]
    ad = x[:, o + RWKV_LORA_PAD:o + 2 * RWKV_LORA_PAD]
    gd = x[:, o + 2 * RWKV_LORA_PAD:]
    wpre = w0_ref[...] + _dot(jnp.tanh(wd).astype(BF16), w2_ref[...])
    w = -jax.nn.softplus(-wpre) - 0.5
    lw_out[...] = -jnp.exp(w)
    a = jax.nn.sigmoid(a0_ref[...] + _dot(ad.astype(BF16), a2_ref[...]))
    g_out[...] = _dot(jax.nn.sigmoid(gd).astype(BF16), g2_ref[...])
    if has_vres:
        low = _dot(v.astype(BF16), v1_ref[...])
        mix = jax.nn.sigmoid(v0_ref[...] + _dot(low.astype(BF16), v2_ref[...]))
        v = v + (vfirst_ref[...] - v) * mix
    r_out[...] = r
    v_out[...] = v
    kk_out[...] = k * kk_ref[...]
    k_out[...] = k * (1.0 + (a - 1.0) * ka_ref[...])
    a_out[...] = a


def rwkv_prep(cols, seq, mu, w0, w2, a0, a2, g2, k_k, k_a, vres):
    m = cols.shape[0]
    tm = _pick(seq, (256, 128, 64, 32, 16, 8))
    row = lambda t: t.reshape(1, -1).astype(F32)
    args = [cols, cols, row(mu), row(w0), w2, row(a0), a2, g2, row(k_k), row(k_a)]
    full = lambda arr: pl.BlockSpec(arr.shape, lambda i: (0, 0))
    tile = pl.BlockSpec((tm, WIDTH), lambda i: (i, 0))
    in_specs = [pl.BlockSpec((tm, RWKV_IN), lambda i: (i, 0)),
                pl.BlockSpec((8, RWKV_IN), lambda i: (jnp.maximum(i * (tm // 8) - 1, 0), 0))]
    in_specs += [full(t) for t in args[2:]]
    if vres is not None:
        v0, v1, v2, v_first = vres
        extra = [row(v0), v1, v2]
        args += extra + [v_first]
        in_specs += [full(t) for t in extra] + [tile]
    return pl.pallas_call(
        functools.partial(_rwkv_prep_body, tiles_per_seq=seq // tm, has_vres=vres is not None),
        grid=(m // tm,),
        in_specs=in_specs,
        out_specs=[tile] * 7,
        out_shape=[jax.ShapeDtypeStruct((m, WIDTH), F32)] * 7,
        compiler_params=_params("parallel"),
        name="rwkv_prep",
    )(*args)


def _halves(x):
    return x[:, :HEAD_DIM], x[:, HEAD_DIM:]


def _head_sum(x, left):
    s0 = jnp.sum(jnp.where(left, x, 0.0), axis=-1, keepdims=True)
    s1 = jnp.sum(jnp.where(left, 0.0, x), axis=-1, keepdims=True)
    return jnp.where(left, s0, s1)


def _dot3(x, y, dot):
    xh, xl = _split(x)
    yh, yl = _split(y)
    return dot(xh, yh) + (dot(xh, yl) + dot(xl, yh))


def _rwkv_chunk_terms(r, lw, k, v, kk_raw, a):
    c = r.shape[0]
    left = lax.broadcasted_iota(jnp.int32, (c, LANES), 1) < HEAD_DIM
    trow = lax.broadcasted_iota(jnp.int32, (c, LANES), 0)
    kk = kk_raw / jnp.maximum(jnp.sqrt(_head_sum(kk_raw * kk_raw, left)), 1e-12)
    beta = kk * a
    cl = lw
    shift = 1
    while shift < c:
        cl = cl + jnp.where(trow >= shift, pltpu.roll(cl, shift, 0), 0.0)
        shift *= 2
    a_t = -kk * jnp.exp(cl - lw)
    r_t = r * jnp.exp(cl)
    inv = jnp.exp(-cl)
    cl_end = cl[c - 1:c, :]
    tail = jnp.exp(cl_end - cl)
    full = (a_t, r_t, beta * inv, k * inv, v, beta * tail, k * tail, jnp.exp(cl_end))
    return [tuple(_halves(x)[hh] for x in full) for hh in range(2)]


def _rwkv_masks(c):
    ti = lax.broadcasted_iota(jnp.int32, (c, c), 0)
    tj = lax.broadcasted_iota(jnp.int32, (c, c), 1)
    merges = []
    half = 1
    while half < c:
        merges.append((ti // (2 * half) == tj // (2 * half)) & (ti % (2 * half) >= half)
                      & (tj % (2 * half) < half))
        half *= 2
    return ti > tj, ti >= tj, ti == tj, merges


def _rwkv_chain(out, head_terms, masks):
    at_h, rt_h, bt_h, kt_h, v_h, bbar_h, kbar_h, gend_h = head_terms
    strict, incl, eye, merges = masks
    c = at_h.shape[0]
    ar = jnp.concatenate([at_h, rt_h], axis=0).astype(BF16)
    bk = jnp.concatenate([bt_h, kt_h], axis=0).astype(BF16)
    pair = _dot_nt(ar, bk)
    yield
    lower = jnp.where(strict, pair[:c, :c], 0.0)
    akv = _dot(jnp.where(strict, pair[:c, c:], 0.0).astype(BF16), v_h.astype(BF16))
    tinv = eye.astype(F32) + jnp.where(merges[0], lower, 0.0)
    for off in merges[1:]:
        tb = tinv.astype(BF16)
        step = _dot(tb, jnp.where(off, lower, 0.0).astype(BF16))
        yield
        tinv = tinv + _dot(step.astype(BF16), tb)
        yield
    solved = _dot(tinv.astype(BF16), jnp.concatenate([at_h, akv], axis=1).astype(BF16))
    yield
    ta, w = _halves(solved)
    rb = jnp.where(incl, pair[c:, :c], 0.0).astype(BF16)
    rk = jnp.where(incl, pair[c:, c:], 0.0).astype(BF16)
    wv = jnp.concatenate([w, v_h], axis=0).astype(BF16)
    q = rt_h + _dot(rb, ta.astype(BF16))
    y0 = _dot(jnp.concatenate([rb, rk], axis=1), wv)
    g = jnp.where(eye, gend_h, 0.0) + _dot_tn(ta.astype(BF16), bbar_h.astype(BF16))
    d = _dot_tn(wv, jnp.concatenate([bbar_h, kbar_h], axis=0).astype(BF16))
    out.append((q, y0, g, d))


def _rwkv_state_chain(hh, results, state_ref, ys):
    state = state_ref[hh]
    for ci, res in enumerate(results):
        while not res[hh]:
            yield
        q, y0, g, d = res[hh][0]
        ys[ci][hh] = _dot3(q, state, _dot_nt) + y0
        state = _dot3(state, g, _dot) + d
        yield
    state_ref[hh] = state


def _lockstep(chains):
    live = list(chains)
    while live:
        still = []
        for ch in live:
            try:
                next(ch)
                still.append(ch)
            except StopIteration:
                pass
        live = still


def _rwkv_scan_body(r_ref, lw_ref, k_ref, v_ref, kk_ref, a_ref, g_ref, lng_ref, lnb_ref, rk_ref,
                    o_ref, state_ref, *, chunk):
    @pl.when(pl.program_id(2) == 0)
    def _():
        state_ref[...] = jnp.zeros_like(state_ref)

    n_chunks = r_ref.shape[1] // chunk
    left = lax.broadcasted_iota(jnp.int32, (chunk, LANES), 1) < HEAD_DIM
    masks = _rwkv_masks(chunk)
    rows = [slice(ci * chunk, (ci + 1) * chunk) for ci in range(n_chunks)]
    results = [[[], []] for _ in range(n_chunks)]
    chains = []
    for ci in range(n_chunks):
        terms = _rwkv_chunk_terms(r_ref[0, rows[ci], :], lw_ref[0, rows[ci], :], k_ref[0, rows[ci], :],
                                  v_ref[0, rows[ci], :], kk_ref[0, rows[ci], :], a_ref[0, rows[ci], :])
        chains += [_rwkv_chain(results[ci][hh], terms[hh], masks) for hh in range(2)]
    _lockstep(chains)
    ys = [[None, None] for _ in range(n_chunks)]
    _lockstep([_rwkv_state_chain(hh, results, state_ref, ys) for hh in range(2)])

    for ci in range(n_chunks):
        y = jnp.concatenate(ys[ci], axis=1)
        r, k, v = r_ref[0, rows[ci], :], k_ref[0, rows[ci], :], v_ref[0, rows[ci], :]
        mean = _head_sum(y, left) * (1.0 / HEAD_DIM)
        var = _head_sum(jnp.square(y - mean), left) * (1.0 / HEAD_DIM)
        y = (y - mean) * lax.rsqrt(var + RWKV_GN_EPS) * lng_ref[...] + lnb_ref[...]
        bonus = _head_sum(r * k * rk_ref[...], left) * v
        o_ref[0, rows[ci], :] = ((y + bonus) * g_ref[0, rows[ci], :]).astype(o_ref.dtype)


def rwkv_scan(r, lw, k, v, kk, a, g, ln_g, ln_b, r_k, batch, seq):
    chunk = min(RWKV_CHUNK, seq)
    rows = _pick(seq, (512, 256, 128, 64, 32, 16, 8))
    hp = WIDTH // LANES
    seq_spec = pl.BlockSpec((1, rows, LANES), lambda b, h, c: (b, c, h))
    par_spec = pl.BlockSpec((1, LANES), lambda b, h, c: (0, h))
    shaped = [t.reshape(batch, seq, WIDTH) for t in (r, lw, k, v, kk, a, g)]
    pars = [t.reshape(1, WIDTH).astype(F32) for t in (ln_g, ln_b, r_k)]
    out = pl.pallas_call(
        functools.partial(_rwkv_scan_body, chunk=chunk),
        grid=(batch, hp, seq // rows),
        in_specs=[seq_spec] * 7 + [par_spec] * 3,
        out_specs=seq_spec,
        out_shape=jax.ShapeDtypeStruct((batch, seq, WIDTH), BF16),
        scratch_shapes=[pltpu.VMEM((2, HEAD_DIM, HEAD_DIM), F32)],
        compiler_params=_params("parallel", "parallel", "arbitrary"),
        name="rwkv_scan",
    )(*shaped, *pars)
    return out.reshape(batch * seq, WIDTH)


def _merge_body(h, a1, a2, a3, g1, g2, g3, w1, w2, w3, o_ref):
    hh = h[...]
    acc = jax.nn.sigmoid(_dot(hh, g1[...])) * _dot(a1[...], w1[...])
    acc += jax.nn.sigmoid(_dot(hh, g2[...])) * _dot(a2[...], w2[...])
    acc += jax.nn.sigmoid(_dot(hh, g3[...])) * _dot(a3[...], w3[...])
    o_ref[...] = acc.astype(o_ref.dtype)


def merge_branches(h, outs, w_gates, weights):
    m, d = h.shape
    tm = _pick(m, (512, 256, 128, 64, 32, 16, 8))
    tn = _pick(d, (512, 256, 128))
    nb = d // tn
    h_spec = pl.BlockSpec((tm, d), lambda i, j: (i, 0))
    a_spec = pl.BlockSpec((tm, WIDTH), lambda i, j: (i, 0))
    w_spec = pl.BlockSpec((WIDTH, tn), lambda i, j: (0, j))
    g_specs = [pl.BlockSpec((d, tn), lambda i, j, b=b: (0, j + b * nb)) for b in range(3)]
    return pl.pallas_call(
        _merge_body,
        grid=(m // tm, nb),
        in_specs=[h_spec] + [a_spec] * 3 + g_specs + [w_spec] * 3,
        out_specs=pl.BlockSpec((tm, tn), lambda i, j: (i, j)),
        out_shape=jax.ShapeDtypeStruct((m, d), BF16),
        compiler_params=_params("parallel", "parallel"),
        name="merge_branches",
    )(h, *outs, w_gates, w_gates, w_gates, *weights)


ROUTER_PAD = 128


def _slab_cols(ref, n, slabs, first=0, pitch=None):
    pitch = slabs if pitch is None else pitch
    return jnp.concatenate([ref[pl.ds(first + s, n, stride=pitch), :] for s in range(slabs)], axis=1)


def _store_slabs(ref, value, slabs):
    n = value.shape[0]
    for s in range(slabs):
        ref[pl.ds(s, n, stride=slabs), :] = value[:, s * LANES:(s + 1) * LANES].astype(ref.dtype)


def _pack_bf16_pairs(h):
    half = h.shape[1] // 2
    bits = pltpu.bitcast(h.astype(BF16).astype(F32), jnp.uint32)
    return (bits[:, :half] >> 16) | bits[:, half:]


def _unpack_bf16_pairs(words):
    lo = pltpu.bitcast(words << 16, F32)
    hi = pltpu.bitcast(words & jnp.uint32(0xFFFF0000), F32)
    return jnp.concatenate([lo, hi], axis=1).astype(BF16)


def _router_body(x_ref, g_ref, w_ref, b_ref, h_ref, logit_ref, *, slabs):
    x = x_ref[...]
    y = x * lax.rsqrt(jnp.mean(x * x, axis=-1, keepdims=True) + NORM_EPS)
    h = y * g_ref[...]
    _store_slabs(h_ref, _pack_bf16_pairs(h), slabs)
    logit_ref[...] = jnp.dot(h, w_ref[...], preferred_element_type=F32, precision=HIGHEST) + b_ref[...]


def router(x, g, w_pad, b_pad):
    m, d = x.shape
    slabs = d // (2 * LANES)
    tm = _pick(m, (256, 128, 64, 32, 16, 8))
    return pl.pallas_call(
        functools.partial(_router_body, slabs=slabs),
        grid=(m // tm,),
        in_specs=[pl.BlockSpec((tm, d), lambda i: (i, 0)),
                  pl.BlockSpec((1, d), lambda i: (0, 0)),
                  pl.BlockSpec((d, ROUTER_PAD), lambda i: (0, 0)),
                  pl.BlockSpec((1, ROUTER_PAD), lambda i: (0, 0))],
        out_specs=[pl.BlockSpec((tm * slabs, LANES), lambda i: (i, 0)),
                   pl.BlockSpec((tm, ROUTER_PAD), lambda i: (i, 0))],
        out_shape=[jax.ShapeDtypeStruct((m * slabs, LANES), jnp.uint32),
                   jax.ShapeDtypeStruct((m, ROUTER_PAD), F32)],
        compiler_params=_params("parallel"),
        name="ffn_norm_router",
    )(x, g.reshape(1, d).astype(F32), w_pad, b_pad)


def _row_gather(idx_ref, first, n, src_ref, buf_ref, sem, slabs):
    def copy(r):
        src = pl.multiple_of(idx_ref[first + r] * slabs, slabs)
        return pltpu.make_async_copy(src_ref.at[pl.ds(src, slabs)],
                                     buf_ref.at[pl.ds(r * slabs, slabs)], sem)

    def start():
        lax.fori_loop(0, n, lambda r, c: (copy(r).start(), c)[1], 0, unroll=8)

    def wait():
        pltpu.make_async_copy(src_ref.at[pl.ds(0, n * slabs)], buf_ref, sem).wait()

    return start, wait


def _expert_body(be_ref, nb_ref, next_ref, tok_ref, h_ref, wg_hbm, wu_hbm, wd_hbm, o_ref,
                 xbuf, wg_f, wu_f, wd_f, wg_s, wu_s, wd_s, sem, wsem, *, layer, in_slabs, slabs):
    i = pl.program_id(0)
    n_blocks = nb_ref[0]
    slot = i % 2

    def gather(block, slot):
        return _row_gather(tok_ref, block * MOE_ROWS, MOE_ROWS, h_ref, xbuf.at[slot], sem.at[slot], in_slabs)

    def weight_copies(expert):
        pairs = ((wg_hbm, wg_f), (wu_hbm, wu_f), (wd_hbm, wd_f))
        return [pltpu.make_async_copy(src.at[layer, expert], dst, wsem.at[j])
                for j, (src, dst) in enumerate(pairs)]

    e = be_ref[i]

    @pl.when(i == 0)
    def _():
        gather(0, 0)[0]()
        for c in weight_copies(e):
            c.start()

    @pl.when(i + 1 < n_blocks)
    def _():
        gather(i + 1, 1 - slot)[0]()

    @pl.when((i < n_blocks) & ((i == 0) | (e != be_ref[jnp.maximum(i - 1, 0)])))
    def _():
        for c in weight_copies(e):
            c.wait()
        wg_s[...] = wg_f[...].astype(BF16)
        wu_s[...] = wu_f[...].astype(BF16)
        wd_s[...] = wd_f[...].astype(BF16)
        upcoming = next_ref[i]

        @pl.when(upcoming >= 0)
        def _():
            for c in weight_copies(upcoming):
                c.start()

    @pl.when(i < n_blocks)
    def _():
        gather(i, slot)[1]()
        x = _unpack_bf16_pairs(_slab_cols(xbuf.at[slot], MOE_ROWS, in_slabs))
        gate = _dot(x, wg_s[...])
        up = _dot(x, wu_s[...])
        mid = (jax.nn.silu(gate) * up).astype(BF16)
        _store_slabs(o_ref, _dot(mid, wd_s[...]), slabs)

    @pl.when(i >= n_blocks)
    def _():
        o_ref[...] = jnp.zeros_like(o_ref)


def expert_ffn(h_rows, row_tok, block_e, next_e, n_blocks, layer, w_gate, w_up, w_down):
    _, n_exp, d, ff = w_gate.shape
    slabs = d // LANES
    in_slabs = d // (2 * LANES)
    p = row_tok.shape[0]
    hbm = pl.BlockSpec(memory_space=pl.ANY)
    return pl.pallas_call(
        functools.partial(_expert_body, layer=layer, in_slabs=in_slabs, slabs=slabs),
        grid_spec=pltpu.PrefetchScalarGridSpec(
            num_scalar_prefetch=4,
            grid=(p // MOE_ROWS,),
            in_specs=[hbm, hbm, hbm, hbm],
            out_specs=pl.BlockSpec((MOE_ROWS * slabs, LANES), lambda i, be, nb, nxt, tok: (i, 0)),
            scratch_shapes=[pltpu.VMEM((2, MOE_ROWS * in_slabs, LANES), jnp.uint32),
                            pltpu.VMEM((d, ff), F32), pltpu.VMEM((d, ff), F32), pltpu.VMEM((ff, d), F32),
                            pltpu.VMEM((d, ff), BF16), pltpu.VMEM((d, ff), BF16),
                            pltpu.VMEM((ff, d), BF16),
                            pltpu.SemaphoreType.DMA((2,)), pltpu.SemaphoreType.DMA((3,))]),
        out_shape=jax.ShapeDtypeStruct((p * slabs, LANES), F32),
        compiler_params=_params("arbitrary"),
        name="expert_ffn",
    )(block_e, n_blocks, next_e, row_tok, h_rows, w_gate, w_up, w_down)


def _combine_body(dest_ref, x_ref, gate_ref, y_ref, o_ref, ybuf, sem, *, tm, slabs):
    i = pl.program_id(0)
    slot = i % 2

    def gather(step, slot):
        return _row_gather(dest_ref, step * tm * TOP_K, tm * TOP_K, y_ref, ybuf.at[slot], sem.at[slot], slabs)

    @pl.when(i == 0)
    def _():
        gather(0, 0)[0]()

    @pl.when(i + 1 < pl.num_programs(0))
    def _():
        gather(i + 1, 1 - slot)[0]()

    gather(i, slot)[1]()
    buf = ybuf.at[slot]
    gate = gate_ref[...]
    g0, g1 = gate[:, 0:1], gate[:, 1:2]
    for s in range(slabs):
        y0 = buf[pl.ds(s, tm, stride=slabs), :]
        y1 = buf[pl.ds(tm * slabs + s, tm, stride=slabs), :]
        cols = slice(s * LANES, (s + 1) * LANES)
        o_ref[:, cols] = x_ref[:, cols] + (g0 * y0 + g1 * y1)


def moe_combine(x, y_rows, dest, gate):
    m, d = x.shape
    slabs = d // LANES
    tm = _pick(m, (256, 128, 64, 32, 16, 8))
    tile = lambda i, dest: (i, 0)
    return pl.pallas_call(
        functools.partial(_combine_body, tm=tm, slabs=slabs),
        grid_spec=pltpu.PrefetchScalarGridSpec(
            num_scalar_prefetch=1,
            grid=(m // tm,),
            in_specs=[pl.BlockSpec((tm, d), tile), pl.BlockSpec((tm, TOP_K), tile),
                      pl.BlockSpec(memory_space=pl.ANY)],
            out_specs=pl.BlockSpec((tm, d), tile),
            scratch_shapes=[pltpu.VMEM((2, tm * TOP_K * slabs, LANES), F32),
                            pltpu.SemaphoreType.DMA((2,))]),
        out_shape=jax.ShapeDtypeStruct((m, d), F32),
        compiler_params=_params("arbitrary"),
        name="moe_combine",
    )(dest.reshape(m // tm, tm, TOP_K).transpose(0, 2, 1).reshape(m * TOP_K), x, gate, y_rows)


def _routing_tables(logits):
    t = logits.shape[0]
    n_assign = t * TOP_K
    tok = jnp.arange(t)
    g_logits = logits[:, :N_GROUPS]
    grp = jnp.argmax(g_logits, axis=-1)
    p_grp = jax.nn.softmax(g_logits, axis=-1)[tok, grp]
    e_logits = logits[:, N_GROUPS:N_GROUPS + N_EXPERTS].reshape(t, N_GROUPS, EXPERTS_PER_GROUP)
    p_in, idx_in = lax.top_k(jax.nn.softmax(e_logits[tok, grp], axis=-1), TOP_K)
    gate = p_grp[:, None] * p_in / jnp.sum(p_in, axis=-1, keepdims=True)
    flat_e = (grp[:, None] * EXPERTS_PER_GROUP + idx_in).reshape(n_assign).astype(jnp.int32)
    onehot = (flat_e[:, None] == jnp.arange(N_EXPERTS, dtype=jnp.int32)[None, :]).astype(jnp.int32)
    rank = jnp.sum((jnp.cumsum(onehot, axis=0) - onehot) * onehot, axis=1)
    counts = jnp.sum(onehot, axis=0)
    padded = (counts + MOE_ROWS - 1) // MOE_ROWS * MOE_ROWS
    pend = jnp.cumsum(padded)
    dest = ((pend - padded)[flat_e] + rank).astype(jnp.int32)
    n_blocks_max = -(-n_assign // MOE_ROWS) + N_EXPERTS
    row_tok = jnp.zeros((n_blocks_max * MOE_ROWS,), jnp.int32).at[dest].set(
        jnp.repeat(tok, TOP_K).astype(jnp.int32))
    block_e = jnp.minimum(
        jnp.searchsorted(pend, jnp.arange(n_blocks_max) * MOE_ROWS, side="right"),
        N_EXPERTS - 1).astype(jnp.int32)
    n_blocks = (pend[-1:] // MOE_ROWS).astype(jnp.int32)
    after = jnp.searchsorted(block_e, block_e, side="right")
    next_e = jnp.where(after < n_blocks[0], block_e[jnp.minimum(after, n_blocks_max - 1)], -1)
    return gate.astype(F32), dest.reshape(t, TOP_K), row_tok, block_e, next_e.astype(jnp.int32), n_blocks


def hier_moe_residual(x, norm_g, rg_w, rg_b, re_w, re_b, layer, w_gate, w_up, w_down):
    t, d = x.shape
    w_pad = jnp.zeros((d, ROUTER_PAD), F32).at[:, :N_GROUPS].set(rg_w)
    w_pad = w_pad.at[:, N_GROUPS:N_GROUPS + N_EXPERTS].set(re_w)
    b_pad = jnp.zeros((1, ROUTER_PAD), F32).at[0, :N_GROUPS].set(rg_b)
    b_pad = b_pad.at[0, N_GROUPS:N_GROUPS + N_EXPERTS].set(re_b)
    h_rows, logits = router(x, norm_g, w_pad, b_pad)
    gate, dest, row_tok, block_e, next_e, n_blocks = _routing_tables(logits)
    y_rows = expert_ffn(h_rows, row_tok, block_e, next_e, n_blocks, layer, w_gate, w_up, w_down)
    return moe_combine(x, y_rows, dest, gate)


def _mla_q_weight(w_uq):
    w = w_uq.reshape(MLA_Q_LORA, HEADS, HEAD_DIM + MLA_ROPE_DIM)
    w = jnp.pad(w, ((0, 0), (0, 0), (0, LANES - HEAD_DIM - MLA_ROPE_DIM)))
    return w.reshape(MLA_Q_LORA, HEADS * LANES).astype(BF16)


def _mla_kv_weights(w_ukv):
    w = w_ukv.reshape(MLA_KV_LORA, HEADS, 2 * HEAD_DIM)
    wk = jnp.pad(w[:, :, :HEAD_DIM], ((0, 0), (0, 0), (0, LANES - HEAD_DIM)))
    wv = w[:, :, HEAD_DIM:]
    return (wk.reshape(MLA_KV_LORA, HEADS * LANES).astype(BF16),
            wv.reshape(MLA_KV_LORA, WIDTH).astype(BF16))


def _rope_slot_tables(positions):
    half = MLA_ROPE_DIM // 2
    inv_freq = ROPE_THETA ** (-jnp.arange(0, MLA_ROPE_DIM, 2, dtype=F32) / MLA_ROPE_DIM)
    ang = positions.astype(F32).reshape(-1, 1) * inv_freq[None, :]
    cos, sin = jnp.cos(ang), jnp.sin(ang)
    t = ang.shape[0]
    pad = jnp.zeros((t, LANES - HEAD_DIM - 2 * half), F32)
    cos_t = jnp.concatenate([jnp.ones((t, HEAD_DIM), F32), cos, cos, pad], axis=1)
    sin_t = jnp.concatenate([jnp.zeros((t, HEAD_DIM), F32), -sin, sin, pad], axis=1)
    return cos_t, sin_t


def _mixer_block(x, l, batch, seq, cos_t, sin_t, v_first, p):
    t, d = x.shape
    h = rmsnorm(x, p["attn_norm_g"][l], BF16, name="attn_norm")
    w_in = p["w_in"][l]
    o_rwkv = 3 * WIDTH
    n_rwkv = 3 * WIDTH + RWKV_DECAY_LORA + RWKV_ICLR_LORA + RWKV_GATE_LORA
    o_mla = o_rwkv + n_rwkv
    n_mla = MLA_Q_LORA + MLA_KV_LORA + MLA_ROPE_DIM
    o_gate = o_mla + n_mla

    qscale = jnp.concatenate([jnp.full((1, WIDTH), HEAD_DIM ** -0.5, F32), jnp.ones((1, 2 * WIDTH), F32)], axis=1)
    (qkv,) = matmul(h, w_in[:, :o_rwkv].astype(BF16), [BF16],
                    epilogue=lambda acc, s: (acc * s,), extras=[(qscale, "col", 0)], name="proj_sb")
    o_sb = sb_attention(qkv, batch, seq)

    wr = w_in[:, o_rwkv:o_mla]
    zpad = jnp.zeros((d, RWKV_LORA_PAD - RWKV_DECAY_LORA), F32)
    c0 = 3 * WIDTH
    w_rwkv = jnp.concatenate([wr[:, :c0], wr[:, c0:c0 + RWKV_DECAY_LORA], zpad,
                              wr[:, c0 + RWKV_DECAY_LORA:c0 + 2 * RWKV_DECAY_LORA], zpad,
                              wr[:, c0 + 2 * RWKV_DECAY_LORA:]], axis=1).astype(BF16)
    (cols,) = matmul(h, w_rwkv, [F32], name="proj_rwkv")
    mu = p["rwkv_mu"][l]
    mpad = jnp.zeros((RWKV_LORA_PAD - RWKV_DECAY_LORA,), F32)
    mu_pad = jnp.concatenate([mu[:c0], mu[c0:c0 + RWKV_DECAY_LORA], mpad,
                              mu[c0 + RWKV_DECAY_LORA:c0 + 2 * RWKV_DECAY_LORA], mpad,
                              mu[c0 + 2 * RWKV_DECAY_LORA:]])
    rpad = ((0, RWKV_LORA_PAD - RWKV_DECAY_LORA), (0, 0))
    vres = None
    if l > 0:
        vres = (p["rwkv_v0"][l - 1], p["rwkv_v1"][l - 1].astype(BF16), p["rwkv_v2"][l - 1].astype(BF16), v_first)
    r, lw, k, v, kk, a, g = rwkv_prep(
        cols, seq, mu_pad, p["rwkv_w0"][l], jnp.pad(p["rwkv_w2"][l], rpad).astype(BF16),
        p["rwkv_a0"][l], jnp.pad(p["rwkv_a2"][l], rpad).astype(BF16), p["rwkv_g2"][l].astype(BF16),
        p["rwkv_k_k"][l], p["rwkv_k_a"][l], vres)
    if l == 0:
        v_first = v
    o_rw = rwkv_scan(r, lw, k, v, kk, a, g, p["rwkv_ln_g"][l], p["rwkv_ln_b"][l], p["rwkv_r_k"][l],
                     batch, seq)

    wm = w_in[:, o_mla:o_gate]
    w_mla = jnp.concatenate([wm[:, :MLA_Q_LORA + MLA_KV_LORA], jnp.zeros((d, HEAD_DIM), F32),
                             wm[:, MLA_Q_LORA + MLA_KV_LORA:],
                             jnp.zeros((d, LANES - HEAD_DIM - MLA_ROPE_DIM), F32)], axis=1).astype(BF16)
    (mcols,) = matmul(h, w_mla, [F32], name="proj_mla")
    cq = rmsnorm(mcols, p["mla_q_norm_g"][l], BF16, width=MLA_Q_LORA, col_block=0, name="mla_q_norm")
    ckv = rmsnorm(mcols, p["mla_kv_norm_g"][l], BF16, width=MLA_KV_LORA,
                  col_block=MLA_Q_LORA // MLA_KV_LORA, name="mla_kv_norm")
    mla_scale = (HEAD_DIM + MLA_ROPE_DIM) ** -0.5
    (q_mla,) = matmul(cq, _mla_q_weight(p["mla_w_uq"][l]), [BF16],
                      epilogue=lambda acc, c, s: (_rope_slots(acc, c, s) * mla_scale,),
                      extras=[(cos_t, "row", 0), (sin_t, "row", 0)], name="mla_q_up")
    wk, wv = _mla_kv_weights(p["mla_w_ukv"][l])
    kpe_block = (MLA_Q_LORA + MLA_KV_LORA) // LANES

    def k_epilogue(acc, kpe, c, s):
        kr = _rope_slots(kpe, c, s)
        return (acc + jnp.concatenate([kr] * (acc.shape[1] // LANES), axis=1),)

    kpe = mcols[:, kpe_block * LANES:(kpe_block + 1) * LANES]
    (k_mla,) = matmul(ckv, wk, [BF16], epilogue=k_epilogue,
                      extras=[(kpe, "row", 0), (cos_t, "row", 0), (sin_t, "row", 0)], name="mla_k_up")
    (v_mla,) = matmul(ckv, wv, [BF16], name="mla_v_up")
    o_mla_out = mla_attention(q_mla, k_mla, v_mla, batch, seq)

    merged = merge_branches(h, [o_sb, o_rw, o_mla_out], w_in[:, o_gate:].astype(BF16),
                            [p["w_br_sb"][l].astype(BF16), p["w_br_rwkv"][l].astype(BF16),
                             p["w_br_mla"][l].astype(BF16)])
    (x,) = matmul(merged, p["w_out"][l].astype(BF16), [F32],
                  epilogue=lambda acc, res: (res + acc,), extras=[(x, "tile", 0)], name="out_proj")
    return x, v_first


def kernel(x, positions, attn_norm_g, w_in, rwkv_mu, rwkv_w0, rwkv_w2, rwkv_a0, rwkv_a2, rwkv_g2, rwkv_k_k, rwkv_k_a, rwkv_r_k, rwkv_ln_g, rwkv_ln_b, rwkv_v0, rwkv_v1, rwkv_v2, mla_q_norm_g, mla_w_uq, mla_kv_norm_g, mla_w_ukv, w_br_sb, w_br_rwkv, w_br_mla, w_out, ffn_norm_g, router_group_w, router_group_b, router_expert_w, router_expert_b, expert_w_gate, expert_w_up, expert_w_down, final_norm_g):
    p = dict(attn_norm_g=attn_norm_g, w_in=w_in, rwkv_mu=rwkv_mu, rwkv_w0=rwkv_w0, rwkv_w2=rwkv_w2,
             rwkv_a0=rwkv_a0, rwkv_a2=rwkv_a2, rwkv_g2=rwkv_g2, rwkv_k_k=rwkv_k_k, rwkv_k_a=rwkv_k_a,
             rwkv_r_k=rwkv_r_k, rwkv_ln_g=rwkv_ln_g, rwkv_ln_b=rwkv_ln_b, rwkv_v0=rwkv_v0,
             rwkv_v1=rwkv_v1, rwkv_v2=rwkv_v2, mla_q_norm_g=mla_q_norm_g, mla_w_uq=mla_w_uq,
             mla_kv_norm_g=mla_kv_norm_g, mla_w_ukv=mla_w_ukv, w_br_sb=w_br_sb, w_br_rwkv=w_br_rwkv,
             w_br_mla=w_br_mla, w_out=w_out)
    batch, seq, d = x.shape
    depth = w_in.shape[0]
    cos_t, sin_t = _rope_slot_tables(positions)
    xt = x.reshape(batch * seq, d)
    v_first = None
    for l in range(depth):
        xt, v_first = _mixer_block(xt, l, batch, seq, cos_t, sin_t, v_first, p)
        xt = hier_moe_residual(xt, ffn_norm_g[l], router_group_w[l], router_group_b[l],
                               router_expert_w[l], router_expert_b[l],
                               l, expert_w_gate, expert_w_up, expert_w_down)
    return rmsnorm(xt, final_norm_g, x.dtype, name="final_norm").reshape(batch, seq, d)
```

```python
import functools

import jax
import jax.numpy as jnp
from jax import lax
from jax.experimental import pallas as pl
from jax.experimental.pallas import tpu as pltpu

F32 = jnp.float32
BF16 = jnp.bfloat16
HIGHEST = lax.Precision.HIGHEST

NORM_EPS = 1e-6
HEADS = 16
HEAD_DIM = 64
WIDTH = HEADS * HEAD_DIM
LANES = 128
RWKV_DECAY_LORA = 96
RWKV_ICLR_LORA = 96
RWKV_GATE_LORA = 256
RWKV_LORA_PAD = 128
RWKV_GN_EPS = 64e-5
RWKV_CHUNK = 64
MLA_Q_LORA = 768
MLA_KV_LORA = 256
MLA_ROPE_DIM = 32
ROPE_THETA = 10000.0
N_GROUPS = 8
EXPERTS_PER_GROUP = 8
N_EXPERTS = N_GROUPS * EXPERTS_PER_GROUP
TOP_K = 2
MOE_ROWS = 512
VMEM_LIMIT = 48 * 1024 * 1024


def _params(*sem):
    return pltpu.CompilerParams(dimension_semantics=sem, vmem_limit_bytes=VMEM_LIMIT)


def _pick(n, cands):
    for c in cands:
        if n % c == 0:
            return c
    raise ValueError(f"no tile for {n} in {cands}")


def _dot(x, y):
    return jnp.dot(x, y, preferred_element_type=F32)


def _dot_nt(x, y):
    return lax.dot_general(x, y, (((1,), (1,)), ((), ())), preferred_element_type=F32)


def _dot_tn(x, y):
    return lax.dot_general(x, y, (((0,), (0,)), ((), ())), preferred_element_type=F32)


def _split(x):
    hi = x.astype(BF16)
    return hi, (x - hi.astype(F32)).astype(BF16)


def _rmsnorm_body(x_ref, g_ref, o_ref):
    x = x_ref[...].astype(F32)
    y = x * lax.rsqrt(jnp.mean(x * x, axis=-1, keepdims=True) + NORM_EPS)
    o_ref[...] = (y * g_ref[...]).astype(o_ref.dtype)


def rmsnorm(x, g, out_dtype, *, width=None, col_block=0, name="rmsnorm"):
    m = x.shape[0]
    width = x.shape[1] if width is None else width
    tm = _pick(m, (512, 256, 128, 64, 32, 16, 8))
    return pl.pallas_call(
        _rmsnorm_body,
        grid=(m // tm,),
        in_specs=[pl.BlockSpec((tm, width), lambda i: (i, col_block)),
                  pl.BlockSpec((1, width), lambda i: (0, 0))],
        out_specs=pl.BlockSpec((tm, width), lambda i: (i, 0)),
        out_shape=jax.ShapeDtypeStruct((m, width), out_dtype),
        compiler_params=_params("parallel"),
        name=name,
    )(x, g.reshape(1, width).astype(F32))


def _mm_body(*refs, n_extra, epilogue):
    a_ref, w_ref = refs[:2]
    extra = [r[...] for r in refs[2:2 + n_extra]]
    outs = refs[2 + n_extra:]
    acc = _dot(a_ref[...], w_ref[...])
    res = epilogue(acc, *extra)
    for o, r in zip(outs, res):
        o[...] = r.astype(o.dtype)


def matmul(a, w, out_dtypes, *, epilogue=None, extras=(), a_col_block=0, name="matmul"):
    m = a.shape[0]
    k, n = w.shape
    tm = _pick(m, (1024, 512, 256, 128, 64, 32, 16, 8))
    tn = n if n <= 1280 else _pick(n, (512, 256, 128))
    if epilogue is None:
        epilogue = lambda acc: (acc,)
    in_specs = [pl.BlockSpec((tm, k), lambda i, j: (i, a_col_block)),
                pl.BlockSpec((k, tn), lambda i, j: (0, j))]
    args = [a, w]
    for arr, kind, off in extras:
        if kind == "row":
            in_specs.append(pl.BlockSpec((tm, arr.shape[1]), lambda i, j: (i, 0)))
        elif kind == "col":
            in_specs.append(pl.BlockSpec((1, tn), lambda i, j: (0, j)))
        else:
            in_specs.append(pl.BlockSpec((tm, tn), lambda i, j, off=off: (i, j + off)))
        args.append(arr)
    return pl.pallas_call(
        functools.partial(_mm_body, n_extra=len(extras), epilogue=epilogue),
        grid=(m // tm, n // tn),
        in_specs=in_specs,
        out_specs=[pl.BlockSpec((tm, tn), lambda i, j: (i, j)) for _ in out_dtypes],
        out_shape=[jax.ShapeDtypeStruct((m, n), dt) for dt in out_dtypes],
        compiler_params=_params("parallel", "parallel"),
        name=name,
    )(*args)


def _rope_slots(x, cos_t, sin_t):
    n = x.shape[-1]
    reps = n // LANES
    lane = lax.broadcasted_iota(jnp.int32, x.shape, 1) % LANES
    from_hi = pltpu.roll(x, n - MLA_ROPE_DIM // 2, 1)
    from_lo = pltpu.roll(x, MLA_ROPE_DIM // 2, 1)
    swapped = jnp.where(lane < HEAD_DIM + MLA_ROPE_DIM // 2, from_hi, from_lo)
    if reps > 1:
        cos_t = jnp.concatenate([cos_t] * reps, axis=1)
        sin_t = jnp.concatenate([sin_t] * reps, axis=1)
    return x * cos_t + swapped * sin_t


def _sb_body(q_ref, k_ref, v_ref, o_ref, *, t, cw):
    qi = pl.program_id(2)
    n_sub = t // cw
    first = lax.broadcasted_iota(jnp.int32, (cw, cw), 0)
    second = lax.broadcasted_iota(jnp.int32, (cw, cw), 1)
    before = second < first
    later = (first > second).astype(BF16)
    later2 = jnp.concatenate([later, later], axis=0)
    heads = [slice(hh * HEAD_DIM, (hh + 1) * HEAD_DIM) for hh in range(2)]

    def sweep(q, carry, ks, n_keys, diag_sub):
        z = [_dot_nt(q[hh], k_ref[0, pl.ds(ks, n_keys), h]) for hh, h in enumerate(heads)]
        acc = [carry[0], carry[2]]
        run = [carry[1], carry[3]]
        subs = n_keys // cw
        weights = [[None] * subs for _ in heads]
        for sub in reversed(range(subs)):
            for hh in range(2):
                zz = z[hh][:, sub * cw:(sub + 1) * cw]
                neg_abs = pltpu.bitcast(pltpu.bitcast(zz, jnp.uint32) | jnp.uint32(0x80000000), F32)
                sp = jnp.maximum(zz, 0.0) + jnp.log(1.0 + jnp.exp(neg_abs))
                mass = jnp.where(before, sp, 0.0) if sub == diag_sub else sp
                hi, lo = _split(mass)
                after = _dot(jnp.concatenate([hi, lo], axis=1), later2)
                wgt = jnp.exp(((zz - sp) - after) - run[hh])
                if sub == diag_sub:
                    wgt = jnp.where(before, wgt, 0.0)
                weights[hh][sub] = wgt.astype(BF16)
                run[hh] = run[hh] + (after[:, :1] + mass[:, :1])
        for hh, h in enumerate(heads):
            acc[hh] = acc[hh] + _dot(jnp.concatenate(weights[hh], axis=1), v_ref[0, pl.ds(ks, n_keys), h])
        return acc[0], run[0], acc[1], run[1]

    k0 = pl.multiple_of(qi * t, t)
    zeros = (jnp.zeros((cw, HEAD_DIM), F32), jnp.zeros((cw, 1), F32)) * 2
    groups = [sweep([q_ref[0, j * cw:(j + 1) * cw, h] for h in heads], zeros, k0, (j + 1) * cw, j)
              for j in range(n_sub)]
    carry = tuple(jnp.concatenate([g[c] for g in groups], axis=0) for c in range(4))
    q = [q_ref[0, :, h] for h in heads]
    carry = lax.fori_loop(
        0, qi, lambda it, c: sweep(q, c, pl.multiple_of((qi - 1 - it) * t, t), t, None), carry)
    o_ref[0] = jnp.concatenate([carry[0], carry[2]], axis=1).astype(o_ref.dtype)


def sb_attention(qkv, batch, seq):
    t = _pick(seq, (512, 256, 128))
    x = qkv.reshape(batch, seq, 3 * WIDTH)
    hp = WIDTH // LANES
    out = pl.pallas_call(
        functools.partial(_sb_body, t=t, cw=min(t, 256)),
        grid=(batch, hp, seq // t),
        in_specs=[pl.BlockSpec((1, t, LANES), lambda b, h, i: (b, i, h)),
                  pl.BlockSpec((1, seq, LANES), lambda b, h, i: (b, 0, hp + h)),
                  pl.BlockSpec((1, seq, LANES), lambda b, h, i: (b, 0, 2 * hp + h))],
        out_specs=pl.BlockSpec((1, t, LANES), lambda b, h, i: (b, i, h)),
        out_shape=jax.ShapeDtypeStruct((batch, seq, WIDTH), BF16),
        compiler_params=_params("parallel", "parallel", "parallel"),
        name="sb_attention",
    )(x, x, x)
    return out.reshape(batch * seq, WIDTH)


def _mla_body(q_ref, k_ref, v_ref, o_ref, *, t):
    qi = pl.program_id(2)
    rows = lax.broadcasted_iota(jnp.int32, (t, t), 0)
    cols = lax.broadcasted_iota(jnp.int32, (t, t), 1)
    q = [q_ref[0, :, hh * LANES:(hh + 1) * LANES] for hh in range(2)]

    def block(ks, carry, diagonal):
        s = [_dot_nt(q[hh], k_ref[0, pl.ds(ks, t), hh * LANES:(hh + 1) * LANES]) for hh in range(2)]
        out = []
        for hh in range(2):
            m, l, acc = carry[3 * hh:3 * hh + 3]
            ss = jnp.where(cols <= rows, s[hh], -1e30) if diagonal else s[hh]
            m_new = jnp.maximum(m, jnp.max(ss, axis=-1, keepdims=True))
            p = jnp.exp(ss - m_new)
            corr = jnp.exp(m - m_new)
            l = corr * l + jnp.sum(p, axis=-1, keepdims=True)
            v = v_ref[0, pl.ds(ks, t), hh * HEAD_DIM:(hh + 1) * HEAD_DIM]
            acc = corr * acc + _dot(p.astype(BF16), v)
            out += [m_new, l, acc]
        return tuple(out)

    init = (jnp.full((t, 1), -1e30, F32), jnp.zeros((t, 1), F32), jnp.zeros((t, HEAD_DIM), F32)) * 2
    carry = block(pl.multiple_of(qi * t, t), init, True)
    carry = lax.fori_loop(0, qi, lambda kb, c: block(pl.multiple_of(kb * t, t), c, False), carry)
    o_ref[0] = jnp.concatenate([carry[2] / carry[1], carry[5] / carry[4]], axis=1).astype(o_ref.dtype)


def mla_attention(q, k, v, batch, seq):
    t = _pick(seq, (512, 256, 128))
    hp = HEADS // 2
    out = pl.pallas_call(
        functools.partial(_mla_body, t=t),
        grid=(batch, hp, seq // t),
        in_specs=[pl.BlockSpec((1, t, 2 * LANES), lambda b, h, i: (b, i, h)),
                  pl.BlockSpec((1, seq, 2 * LANES), lambda b, h, i: (b, 0, h)),
                  pl.BlockSpec((1, seq, LANES), lambda b, h, i: (b, 0, h))],
        out_specs=pl.BlockSpec((1, t, LANES), lambda b, h, i: (b, i, h)),
        out_shape=jax.ShapeDtypeStruct((batch, seq, WIDTH), BF16),
        compiler_params=_params("parallel", "parallel", "parallel"),
        name="mla_attention",
    )(q.reshape(batch, seq, HEADS * LANES), k.reshape(batch, seq, HEADS * LANES),
      v.reshape(batch, seq, WIDTH))
    return out.reshape(batch * seq, WIDTH)


RWKV_IN = 3 * WIDTH + 2 * RWKV_LORA_PAD + RWKV_GATE_LORA


def _rwkv_prep_body(*refs, tiles_per_seq, has_vres):
    if has_vres:
        (cols_ref, prev_ref, mu_ref, w0_ref, w2_ref, a0_ref, a2_ref, g2_ref, kk_ref, ka_ref,
         v0_ref, v1_ref, v2_ref, vfirst_ref,
         r_out, lw_out, k_out, v_out, kk_out, a_out, g_out) = refs
    else:
        (cols_ref, prev_ref, mu_ref, w0_ref, w2_ref, a0_ref, a2_ref, g2_ref, kk_ref, ka_ref,
         r_out, lw_out, k_out, v_out, kk_out, a_out, g_out) = refs
    x = cols_ref[...]
    starts_seq = (pl.program_id(0) % tiles_per_seq) == 0
    last_prev = jnp.where(starts_seq, 0.0, prev_ref[7:8, :])
    first_row = lax.broadcasted_iota(jnp.int32, x.shape, 0) == 0
    prev = jnp.where(first_row, last_prev, pltpu.roll(x, 1, 0))
    x = x + (prev - x) * mu_ref[...]
    r = x[:, :WIDTH]
    k = x[:, WIDTH:2 * WIDTH]
    v = x[:, 2 * WIDTH:3 * WIDTH]
    o = 3 * WIDTH
    wd = x[:, o:o + RWKV_LORA_PAD]
    ad = x[:, o + RWKV_LORA_PAD:o + 2 * RWKV_LORA_PAD]
    gd = x[:, o + 2 * RWKV_LORA_PAD:]
    wpre = w0_ref[...] + _dot(jnp.tanh(wd).astype(BF16), w2_ref[...])
    w = -jax.nn.softplus(-wpre) - 0.5
    lw_out[...] = -jnp.exp(w)
    a = jax.nn.sigmoid(a0_ref[...] + _dot(ad.astype(BF16), a2_ref[...]))
    g_out[...] = _dot(jax.nn.sigmoid(gd).astype(BF16), g2_ref[...])
    if has_vres:
        low = _dot(v.astype(BF16), v1_ref[...])
        mix = jax.nn.sigmoid(v0_ref[...] + _dot(low.astype(BF16), v2_ref[...]))
        v = v + (vfirst_ref[...] - v) * mix
    r_out[...] = r
    v_out[...] = v
    kk_out[...] = k * kk_ref[...]
    k_out[...] = k * (1.0 + (a - 1.0) * ka_ref[...])
    a_out[...] = a


def rwkv_prep(cols, seq, mu, w0, w2, a0, a2, g2, k_k, k_a, vres):
    m = cols.shape[0]
    tm = _pick(seq, (256, 128, 64, 32, 16, 8))
    row = lambda t: t.reshape(1, -1).astype(F32)
    args = [cols, cols, row(mu), row(w0), w2, row(a0), a2, g2, row(k_k), row(k_a)]
    full = lambda arr: pl.BlockSpec(arr.shape, lambda i: (0, 0))
    tile = pl.BlockSpec((tm, WIDTH), lambda i: (i, 0))
    in_specs = [pl.BlockSpec((tm, RWKV_IN), lambda i: (i, 0)),
                pl.BlockSpec((8, RWKV_IN), lambda i: (jnp.maximum(i * (tm // 8) - 1, 0), 0))]
    in_specs += [full(t) for t in args[2:]]
    if vres is not None:
        v0, v1, v2, v_first = vres
        extra = [row(v0), v1, v2]
        args += extra + [v_first]
        in_specs += [full(t) for t in extra] + [tile]
    return pl.pallas_call(
        functools.partial(_rwkv_prep_body, tiles_per_seq=seq // tm, has_vres=vres is not None),
        grid=(m // tm,),
        in_specs=in_specs,
        out_specs=[tile] * 7,
        out_shape=[jax.ShapeDtypeStruct((m, WIDTH), F32)] * 7,
        compiler_params=_params("parallel"),
        name="rwkv_prep",
    )(*args)


def _halves(x):
    return x[:, :HEAD_DIM], x[:, HEAD_DIM:]


def _head_sum(x, left):
    s0 = jnp.sum(jnp.where(left, x, 0.0), axis=-1, keepdims=True)
    s1 = jnp.sum(jnp.where(left, 0.0, x), axis=-1, keepdims=True)
    return jnp.where(left, s0, s1)


def _dot3(x, y, dot):
    xh, xl = _split(x)
    yh, yl = _split(y)
    return dot(xh, yh) + (dot(xh, yl) + dot(xl, yh))


def _rwkv_chunk_terms(r, lw, k, v, kk_raw, a):
    c = r.shape[0]
    left = lax.broadcasted_iota(jnp.int32, (c, LANES), 1) < HEAD_DIM
    trow = lax.broadcasted_iota(jnp.int32, (c, LANES), 0)
    kk = kk_raw / jnp.maximum(jnp.sqrt(_head_sum(kk_raw * kk_raw, left)), 1e-12)
    beta = kk * a
    cl = lw
    shift = 1
    while shift < c:
        cl = cl + jnp.where(trow >= shift, pltpu.roll(cl, shift, 0), 0.0)
        shift *= 2
    a_t = -kk * jnp.exp(cl - lw)
    r_t = r * jnp.exp(cl)
    inv = jnp.exp(-cl)
    cl_end = cl[c - 1:c, :]
    tail = jnp.exp(cl_end - cl)
    full = (a_t, r_t, beta * inv, k * inv, v, beta * tail, k * tail, jnp.exp(cl_end))
    return [tuple(_halves(x)[hh] for x in full) for hh in range(2)]


def _rwkv_masks(c):
    ti = lax.broadcasted_iota(jnp.int32, (c, c), 0)
    tj = lax.broadcasted_iota(jnp.int32, (c, c), 1)
    merges = []
    half = 1
    while half < c:
        merges.append((ti // (2 * half) == tj // (2 * half)) & (ti % (2 * half) >= half)
                      & (tj % (2 * half) < half))
        half *= 2
    return ti > tj, ti >= tj, ti == tj, merges


def _rwkv_chain(out, head_terms, masks):
    at_h, rt_h, bt_h, kt_h, v_h, bbar_h, kbar_h, gend_h = head_terms
    strict, incl, eye, merges = masks
    c = at_h.shape[0]
    ar = jnp.concatenate([at_h, rt_h], axis=0).astype(BF16)
    bk = jnp.concatenate([bt_h, kt_h], axis=0).astype(BF16)
    pair = _dot_nt(ar, bk)
    yield
    lower = jnp.where(strict, pair[:c, :c], 0.0)
    akv = _dot(jnp.where(strict, pair[:c, c:], 0.0).astype(BF16), v_h.astype(BF16))
    tinv = eye.astype(F32) + jnp.where(merges[0], lower, 0.0)
    for off in merges[1:]:
        tb = tinv.astype(BF16)
        step = _dot(tb, jnp.where(off, lower, 0.0).astype(BF16))
        yield
        tinv = tinv + _dot(step.astype(BF16), tb)
        yield
    solved = _dot(tinv.astype(BF16), jnp.concatenate([at_h, akv], axis=1).astype(BF16))
    yield
    ta, w = _halves(solved)
    rb = jnp.where(incl, pair[c:, :c], 0.0).astype(BF16)
    rk = jnp.where(incl, pair[c:, c:], 0.0).astype(BF16)
    wv = jnp.concatenate([w, v_h], axis=0).astype(BF16)
    q = rt_h + _dot(rb, ta.astype(BF16))
    y0 = _dot(jnp.concatenate([rb, rk], axis=1), wv)
    g = jnp.where(eye, gend_h, 0.0) + _dot_tn(ta.astype(BF16), bbar_h.astype(BF16))
    d = _dot_tn(wv, jnp.concatenate([bbar_h, kbar_h], axis=0).astype(BF16))
    out.append((q, y0, g, d))


def _rwkv_state_chain(hh, results, state_ref, ys):
    state = state_ref[hh]
    for ci, res in enumerate(results):
        while not res[hh]:
            yield
        q, y0, g, d = res[hh][0]
        ys[ci][hh] = _dot3(q, state, _dot_nt) + y0
        state = _dot3(state, g, _dot) + d
        yield
    state_ref[hh] = state


def _lockstep(chains):
    live = list(chains)
    while live:
        still = []
        for ch in live:
            try:
                next(ch)
                still.append(ch)
            except StopIteration:
                pass
        live = still


def _rwkv_scan_body(r_ref, lw_ref, k_ref, v_ref, kk_ref, a_ref, g_ref, lng_ref, lnb_ref, rk_ref,
                    o_ref, state_ref, *, chunk):
    @pl.when(pl.program_id(2) == 0)
    def _():
        state_ref[...] = jnp.zeros_like(state_ref)

    n_chunks = r_ref.shape[1] // chunk
    left = lax.broadcasted_iota(jnp.int32, (chunk, LANES), 1) < HEAD_DIM
    masks = _rwkv_masks(chunk)
    rows = [slice(ci * chunk, (ci + 1) * chunk) for ci in range(n_chunks)]
    results = [[[], []] for _ in range(n_chunks)]
    chains = []
    for ci in range(n_chunks):
        terms = _rwkv_chunk_terms(r_ref[0, rows[ci], :], lw_ref[0, rows[ci], :], k_ref[0, rows[ci], :],
                                  v_ref[0, rows[ci], :], kk_ref[0, rows[ci], :], a_ref[0, rows[ci], :])
        chains += [_rwkv_chain(results[ci][hh], terms[hh], masks) for hh in range(2)]
    _lockstep(chains)
    ys = [[None, None] for _ in range(n_chunks)]
    _lockstep([_rwkv_state_chain(hh, results, state_ref, ys) for hh in range(2)])

    for ci in range(n_chunks):
        y = jnp.concatenate(ys[ci], axis=1)
        r, k, v = r_ref[0, rows[ci], :], k_ref[0, rows[ci], :], v_ref[0, rows[ci], :]
        mean = _head_sum(y, left) * (1.0 / HEAD_DIM)
        var = _head_sum(jnp.square(y - mean), left) * (1.0 / HEAD_DIM)
        y = (y - mean) * lax.rsqrt(var + RWKV_GN_EPS) * lng_ref[...] + lnb_ref[...]
        bonus = _head_sum(r * k * rk_ref[...], left) * v
        o_ref[0, rows[ci], :] = ((y + bonus) * g_ref[0, rows[ci], :]).astype(o_ref.dtype)


def rwkv_scan(r, lw, k, v, kk, a, g, ln_g, ln_b, r_k, batch, seq):
    chunk = min(RWKV_CHUNK, seq)
    rows = _pick(seq, (512, 256, 128, 64, 32, 16, 8))
    hp = WIDTH // LANES
    seq_spec = pl.BlockSpec((1, rows, LANES), lambda b, h, c: (b, c, h))
    par_spec = pl.BlockSpec((1, LANES), lambda b, h, c: (0, h))
    shaped = [t.reshape(batch, seq, WIDTH) for t in (r, lw, k, v, kk, a, g)]
    pars = [t.reshape(1, WIDTH).astype(F32) for t in (ln_g, ln_b, r_k)]
    out = pl.pallas_call(
        functools.partial(_rwkv_scan_body, chunk=chunk),
        grid=(batch, hp, seq // rows),
        in_specs=[seq_spec] * 7 + [par_spec] * 3,
        out_specs=seq_spec,
        out_shape=jax.ShapeDtypeStruct((batch, seq, WIDTH), BF16),
        scratch_shapes=[pltpu.VMEM((2, HEAD_DIM, HEAD_DIM), F32)],
        compiler_params=_params("parallel", "parallel", "arbitrary"),
        name="rwkv_scan",
    )(*shaped, *pars)
    return out.reshape(batch * seq, WIDTH)


def _merge_body(h, a1, a2, a3, g1, g2, g3, w1, w2, w3, o_ref):
    hh = h[...]
    acc = jax.nn.sigmoid(_dot(hh, g1[...])) * _dot(a1[...], w1[...])
    acc += jax.nn.sigmoid(_dot(hh, g2[...])) * _dot(a2[...], w2[...])
    acc += jax.nn.sigmoid(_dot(hh, g3[...])) * _dot(a3[...], w3[...])
    o_ref[...] = acc.astype(o_ref.dtype)


def merge_branches(h, outs, w_gates, weights):
    m, d = h.shape
    tm = _pick(m, (512, 256, 128, 64, 32, 16, 8))
    tn = _pick(d, (512, 256, 128))
    nb = d // tn
    h_spec = pl.BlockSpec((tm, d), lambda i, j: (i, 0))
    a_spec = pl.BlockSpec((tm, WIDTH), lambda i, j: (i, 0))
    w_spec = pl.BlockSpec((WIDTH, tn), lambda i, j: (0, j))
    g_specs = [pl.BlockSpec((d, tn), lambda i, j, b=b: (0, j + b * nb)) for b in range(3)]
    return pl.pallas_call(
        _merge_body,
        grid=(m // tm, nb),
        in_specs=[h_spec] + [a_spec] * 3 + g_specs + [w_spec] * 3,
        out_specs=pl.BlockSpec((tm, tn), lambda i, j: (i, j)),
        out_shape=jax.ShapeDtypeStruct((m, d), BF16),
        compiler_params=_params("parallel", "parallel"),
        name="merge_branches",
    )(h, *outs, w_gates, w_gates, w_gates, *weights)


ROUTER_PAD = 128


def _slab_cols(ref, n, slabs, first=0, pitch=None):
    pitch = slabs if pitch is None else pitch
    return jnp.concatenate([ref[pl.ds(first + s, n, stride=pitch), :] for s in range(slabs)], axis=1)


def _store_slabs(ref, value, slabs):
    n = value.shape[0]
    for s in range(slabs):
        ref[pl.ds(s, n, stride=slabs), :] = value[:, s * LANES:(s + 1) * LANES].astype(ref.dtype)


def _pack_bf16_pairs(h):
    half = h.shape[1] // 2
    bits = pltpu.bitcast(h.astype(BF16).astype(F32), jnp.uint32)
    return (bits[:, :half] >> 16) | bits[:, half:]


def _unpack_bf16_pairs(words):
    lo = pltpu.bitcast(words << 16, F32)
    hi = pltpu.bitcast(words & jnp.uint32(0xFFFF0000), F32)
    return jnp.concatenate([lo, hi], axis=1).astype(BF16)


def _router_body(x_ref, g_ref, w_ref, b_ref, h_ref, logit_ref, *, slabs):
    x = x_ref[...]
    y = x * lax.rsqrt(jnp.mean(x * x, axis=-1, keepdims=True) + NORM_EPS)
    h = y * g_ref[...]
    _store_slabs(h_ref, _pack_bf16_pairs(h), slabs)
    logit_ref[...] = jnp.dot(h, w_ref[...], preferred_element_type=F32, precision=HIGHEST) + b_ref[...]


def router(x, g, w_pad, b_pad):
    m, d = x.shape
    slabs = d // (2 * LANES)
    tm = _pick(m, (256, 128, 64, 32, 16, 8))
    return pl.pallas_call(
        functools.partial(_router_body, slabs=slabs),
        grid=(m // tm,),
        in_specs=[pl.BlockSpec((tm, d), lambda i: (i, 0)),
                  pl.BlockSpec((1, d), lambda i: (0, 0)),
                  pl.BlockSpec((d, ROUTER_PAD), lambda i: (0, 0)),
                  pl.BlockSpec((1, ROUTER_PAD), lambda i: (0, 0))],
        out_specs=[pl.BlockSpec((tm * slabs, LANES), lambda i: (i, 0)),
                   pl.BlockSpec((tm, ROUTER_PAD), lambda i: (i, 0))],
        out_shape=[jax.ShapeDtypeStruct((m * slabs, LANES), jnp.uint32),
                   jax.ShapeDtypeStruct((m, ROUTER_PAD), F32)],
        compiler_params=_params("parallel"),
        name="ffn_norm_router",
    )(x, g.reshape(1, d).astype(F32), w_pad, b_pad)


def _row_gather(idx_ref, first, n, src_ref, buf_ref, sem, slabs):
    def copy(r):
        src = pl.multiple_of(idx_ref[first + r] * slabs, slabs)
        return pltpu.make_async_copy(src_ref.at[pl.ds(src, slabs)],
                                     buf_ref.at[pl.ds(r * slabs, slabs)], sem)

    def start():
        lax.fori_loop(0, n, lambda r, c: (copy(r).start(), c)[1], 0, unroll=8)

    def wait():
        pltpu.make_async_copy(src_ref.at[pl.ds(0, n * slabs)], buf_ref, sem).wait()

    return start, wait


def _expert_body(be_ref, nb_ref, next_ref, tok_ref, h_ref, wg_hbm, wu_hbm, wd_hbm, o_ref,
                 xbuf, wg_f, wu_f, wd_f, wg_s, wu_s, wd_s, sem, wsem, *, layer, in_slabs, slabs):
    i = pl.program_id(0)
    n_blocks = nb_ref[0]
    slot = i % 2

    def gather(block, slot):
        return _row_gather(tok_ref, block * MOE_ROWS, MOE_ROWS, h_ref, xbuf.at[slot], sem.at[slot], in_slabs)

    def weight_copies(expert):
        pairs = ((wg_hbm, wg_f), (wu_hbm, wu_f), (wd_hbm, wd_f))
        return [pltpu.make_async_copy(src.at[layer, expert], dst, wsem.at[j])
                for j, (src, dst) in enumerate(pairs)]

    e = be_ref[i]

    @pl.when(i == 0)
    def _():
        gather(0, 0)[0]()
        for c in weight_copies(e):
            c.start()

    @pl.when(i + 1 < n_blocks)
    def _():
        gather(i + 1, 1 - slot)[0]()

    @pl.when((i < n_blocks) & ((i == 0) | (e != be_ref[jnp.maximum(i - 1, 0)])))
    def _():
        for c in weight_copies(e):
            c.wait()
        wg_s[...] = wg_f[...].astype(BF16)
        wu_s[...] = wu_f[...].astype(BF16)
        wd_s[...] = wd_f[...].astype(BF16)
        upcoming = next_ref[i]

        @pl.when(upcoming >= 0)
        def _():
            for c in weight_copies(upcoming):
                c.start()

    @pl.when(i < n_blocks)
    def _():
        gather(i, slot)[1]()
        x = _unpack_bf16_pairs(_slab_cols(xbuf.at[slot], MOE_ROWS, in_slabs))
        gate = _dot(x, wg_s[...])
        up = _dot(x, wu_s[...])
        mid = (jax.nn.silu(gate) * up).astype(BF16)
        _store_slabs(o_ref, _dot(mid, wd_s[...]), slabs)

    @pl.when(i >= n_blocks)
    def _():
        o_ref[...] = jnp.zeros_like(o_ref)


def expert_ffn(h_rows, row_tok, block_e, next_e, n_blocks, layer, w_gate, w_up, w_down):
    _, n_exp, d, ff = w_gate.shape
    slabs = d // LANES
    in_slabs = d // (2 * LANES)
    p = row_tok.shape[0]
    hbm = pl.BlockSpec(memory_space=pl.ANY)
    return pl.pallas_call(
        functools.partial(_expert_body, layer=layer, in_slabs=in_slabs, slabs=slabs),
        grid_spec=pltpu.PrefetchScalarGridSpec(
            num_scalar_prefetch=4,
            grid=(p // MOE_ROWS,),
            in_specs=[hbm, hbm, hbm, hbm],
            out_specs=pl.BlockSpec((MOE_ROWS * slabs, LANES), lambda i, be, nb, nxt, tok: (i, 0)),
            scratch_shapes=[pltpu.VMEM((2, MOE_ROWS * in_slabs, LANES), jnp.uint32),
                            pltpu.VMEM((d, ff), F32), pltpu.VMEM((d, ff), F32), pltpu.VMEM((ff, d), F32),
                            pltpu.VMEM((d, ff), BF16), pltpu.VMEM((d, ff), BF16),
                            pltpu.VMEM((ff, d), BF16),
                            pltpu.SemaphoreType.DMA((2,)), pltpu.SemaphoreType.DMA((3,))]),
        out_shape=jax.ShapeDtypeStruct((p * slabs, LANES), F32),
        compiler_params=_params("arbitrary"),
        name="expert_ffn",
    )(block_e, n_blocks, next_e, row_tok, h_rows, w_gate, w_up, w_down)


def _combine_body(dest_ref, x_ref, gate_ref, y_ref, o_ref, ybuf, sem, *, tm, slabs):
    i = pl.program_id(0)
    slot = i % 2

    def gather(step, slot):
        return _row_gather(dest_ref, step * tm * TOP_K, tm * TOP_K, y_ref, ybuf.at[slot], sem.at[slot], slabs)

    @pl.when(i == 0)
    def _():
        gather(0, 0)[0]()

    @pl.when(i + 1 < pl.num_programs(0))
    def _():
        gather(i + 1, 1 - slot)[0]()

    gather(i, slot)[1]()
    buf = ybuf.at[slot]
    gate = gate_ref[...]
    g0, g1 = gate[:, 0:1], gate[:, 1:2]
    for s in range(slabs):
        y0 = buf[pl.ds(s, tm, stride=slabs), :]
        y1 = buf[pl.ds(tm * slabs + s, tm, stride=slabs), :]
        cols = slice(s * LANES, (s + 1) * LANES)
        o_ref[:, cols] = x_ref[:, cols] + (g0 * y0 + g1 * y1)


def moe_combine(x, y_rows, dest, gate):
    m, d = x.shape
    slabs = d // LANES
    tm = _pick(m, (256, 128, 64, 32, 16, 8))
    tile = lambda i, dest: (i, 0)
    return pl.pallas_call(
        functools.partial(_combine_body, tm=tm, slabs=slabs),
        grid_spec=pltpu.PrefetchScalarGridSpec(
            num_scalar_prefetch=1,
            grid=(m // tm,),
            in_specs=[pl.BlockSpec((tm, d), tile), pl.BlockSpec((tm, TOP_K), tile),
                      pl.BlockSpec(memory_space=pl.ANY)],
            out_specs=pl.BlockSpec((tm, d), tile),
            scratch_shapes=[pltpu.VMEM((2, tm * TOP_K * slabs, LANES), F32),
                            pltpu.SemaphoreType.DMA((2,))]),
        out_shape=jax.ShapeDtypeStruct((m, d), F32),
        compiler_params=_params("arbitrary"),
        name="moe_combine",
    )(dest.reshape(m // tm, tm, TOP_K).transpose(0, 2, 1).reshape(m * TOP_K), x, gate, y_rows)


def _routing_tables(logits):
    t = logits.shape[0]
    n_assign = t * TOP_K
    tok = jnp.arange(t)
    g_logits = logits[:, :N_GROUPS]
    grp = jnp.argmax(g_logits, axis=-1)
    p_grp = jax.nn.softmax(g_logits, axis=-1)[tok, grp]
    e_logits = logits[:, N_GROUPS:N_GROUPS + N_EXPERTS].reshape(t, N_GROUPS, EXPERTS_PER_GROUP)
    p_in, idx_in = lax.top_k(jax.nn.softmax(e_logits[tok, grp], axis=-1), TOP_K)
    gate = p_grp[:, None] * p_in / jnp.sum(p_in, axis=-1, keepdims=True)
    flat_e = (grp[:, None] * EXPERTS_PER_GROUP + idx_in).reshape(n_assign).astype(jnp.int32)
    onehot = (flat_e[:, None] == jnp.arange(N_EXPERTS, dtype=jnp.int32)[None, :]).astype(jnp.int32)
    rank = jnp.sum((jnp.cumsum(onehot, axis=0) - onehot) * onehot, axis=1)
    counts = jnp.sum(onehot, axis=0)
    padded = (counts + MOE_ROWS - 1) // MOE_ROWS * MOE_ROWS
    pend = jnp.cumsum(padded)
    dest = ((pend - padded)[flat_e] + rank).astype(jnp.int32)
    n_blocks_max = -(-n_assign // MOE_ROWS) + N_EXPERTS
    row_tok = jnp.zeros((n_blocks_max * MOE_ROWS,), jnp.int32).at[dest].set(
        jnp.repeat(tok, TOP_K).astype(jnp.int32))
    block_e = jnp.minimum(
        jnp.searchsorted(pend, jnp.arange(n_blocks_max) * MOE_ROWS, side="right"),
        N_EXPERTS - 1).astype(jnp.int32)
    n_blocks = (pend[-1:] // MOE_ROWS).astype(jnp.int32)
    after = jnp.searchsorted(block_e, block_e, side="right")
    next_e = jnp.where(after < n_blocks[0], block_e[jnp.minimum(after, n_blocks_max - 1)], -1)
    return gate.astype(F32), dest.reshape(t, TOP_K), row_tok, block_e, next_e.astype(jnp.int32), n_blocks


def hier_moe_residual(x, norm_g, rg_w, rg_b, re_w, re_b, layer, w_gate, w_up, w_down):
    t, d = x.shape
    w_pad = jnp.zeros((d, ROUTER_PAD), F32).at[:, :N_GROUPS].set(rg_w)
    w_pad = w_pad.at[:, N_GROUPS:N_GROUPS + N_EXPERTS].set(re_w)
    b_pad = jnp.zeros((1, ROUTER_PAD), F32).at[0, :N_GROUPS].set(rg_b)
    b_pad = b_pad.at[0, N_GROUPS:N_GROUPS + N_EXPERTS].set(re_b)
    h_rows, logits = router(x, norm_g, w_pad, b_pad)
    gate, dest, row_tok, block_e, next_e, n_blocks = _routing_tables(logits)
    y_rows = expert_ffn(h_rows, row_tok, block_e, next_e, n_blocks, layer, w_gate, w_up, w_down)
    return moe_combine(x, y_rows, dest, gate)


def _mla_q_weight(w_uq):
    w = w_uq.reshape(MLA_Q_LORA, HEADS, HEAD_DIM + MLA_ROPE_DIM)
    w = jnp.pad(w, ((0, 0), (0, 0), (0, LANES - HEAD_DIM - MLA_ROPE_DIM)))
    return w.reshape(MLA_Q_LORA, HEADS * LANES).astype(BF16)


def _mla_kv_weights(w_ukv):
    w = w_ukv.reshape(MLA_KV_LORA, HEADS, 2 * HEAD_DIM)
    wk = jnp.pad(w[:, :, :HEAD_DIM], ((0, 0), (0, 0), (0, LANES - HEAD_DIM)))
    wv = w[:, :, HEAD_DIM:]
    return (wk.reshape(MLA_KV_LORA, HEADS * LANES).astype(BF16),
            wv.reshape(MLA_KV_LORA, WIDTH).astype(BF16))


def _rope_slot_tables(positions):
    half = MLA_ROPE_DIM // 2
    inv_freq = ROPE_THETA ** (-jnp.arange(0, MLA_ROPE_DIM, 2, dtype=F32) / MLA_ROPE_DIM)
    ang = positions.astype(F32).reshape(-1, 1) * inv_freq[None, :]
    cos, sin = jnp.cos(ang), jnp.sin(ang)
    t = ang.shape[0]
    pad = jnp.zeros((t, LANES - HEAD_DIM - 2 * half), F32)
    cos_t = jnp.concatenate([jnp.ones((t, HEAD_DIM), F32), cos, cos, pad], axis=1)
    sin_t = jnp.concatenate([jnp.zeros((t, HEAD_DIM), F32), -sin, sin, pad], axis=1)
    return cos_t, sin_t


def _mixer_block(x, l, batch, seq, cos_t, sin_t, v_first, p):
    t, d = x.shape
    h = rmsnorm(x, p["attn_norm_g"][l], BF16, name="attn_norm")
    w_in = p["w_in"][l]
    o_rwkv = 3 * WIDTH
    n_rwkv = 3 * WIDTH + RWKV_DECAY_LORA + RWKV_ICLR_LORA + RWKV_GATE_LORA
    o_mla = o_rwkv + n_rwkv
    n_mla = MLA_Q_LORA + MLA_KV_LORA + MLA_ROPE_DIM
    o_gate = o_mla + n_mla

    qscale = jnp.concatenate([jnp.full((1, WIDTH), HEAD_DIM ** -0.5, F32), jnp.ones((1, 2 * WIDTH), F32)], axis=1)
    (qkv,) = matmul(h, w_in[:, :o_rwkv].astype(BF16), [BF16],
                    epilogue=lambda acc, s: (acc * s,), extras=[(qscale, "col", 0)], name="proj_sb")
    o_sb = sb_attention(qkv, batch, seq)

    wr = w_in[:, o_rwkv:o_mla]
    zpad = jnp.zeros((d, RWKV_LORA_PAD - RWKV_DECAY_LORA), F32)
    c0 = 3 * WIDTH
    w_rwkv = jnp.concatenate([wr[:, :c0], wr[:, c0:c0 + RWKV_DECAY_LORA], zpad,
                              wr[:, c0 + RWKV_DECAY_LORA:c0 + 2 * RWKV_DECAY_LORA], zpad,
                              wr[:, c0 + 2 * RWKV_DECAY_LORA:]], axis=1).astype(BF16)
    (cols,) = matmul(h, w_rwkv, [F32], name="proj_rwkv")
    mu = p["rwkv_mu"][l]
    mpad = jnp.zeros((RWKV_LORA_PAD - RWKV_DECAY_LORA,), F32)
    mu_pad = jnp.concatenate([mu[:c0], mu[c0:c0 + RWKV_DECAY_LORA], mpad,
                              mu[c0 + RWKV_DECAY_LORA:c0 + 2 * RWKV_DECAY_LORA], mpad,
                              mu[c0 + 2 * RWKV_DECAY_LORA:]])
    rpad = ((0, RWKV_LORA_PAD - RWKV_DECAY_LORA), (0, 0))
    vres = None
    if l > 0:
        vres = (p["rwkv_v0"][l - 1], p["rwkv_v1"][l - 1].astype(BF16), p["rwkv_v2"][l - 1].astype(BF16), v_first)
    r, lw, k, v, kk, a, g = rwkv_prep(
        cols, seq, mu_pad, p["rwkv_w0"][l], jnp.pad(p["rwkv_w2"][l], rpad).astype(BF16),
        p["rwkv_a0"][l], jnp.pad(p["rwkv_a2"][l], rpad).astype(BF16), p["rwkv_g2"][l].astype(BF16),
        p["rwkv_k_k"][l], p["rwkv_k_a"][l], vres)
    if l == 0:
        v_first = v
    o_rw = rwkv_scan(r, lw, k, v, kk, a, g, p["rwkv_ln_g"][l], p["rwkv_ln_b"][l], p["rwkv_r_k"][l],
                     batch, seq)

    wm = w_in[:, o_mla:o_gate]
    w_mla = jnp.concatenate([wm[:, :MLA_Q_LORA + MLA_KV_LORA], jnp.zeros((d, HEAD_DIM), F32),
                             wm[:, MLA_Q_LORA + MLA_KV_LORA:],
                             jnp.zeros((d, LANES - HEAD_DIM - MLA_ROPE_DIM), F32)], axis=1).astype(BF16)
    (mcols,) = matmul(h, w_mla, [F32], name="proj_mla")
    cq = rmsnorm(mcols, p["mla_q_norm_g"][l], BF16, width=MLA_Q_LORA, col_block=0, name="mla_q_norm")
    ckv = rmsnorm(mcols, p["mla_kv_norm_g"][l], BF16, width=MLA_KV_LORA,
                  col_block=MLA_Q_LORA // MLA_KV_LORA, name="mla_kv_norm")
    mla_scale = (HEAD_DIM + MLA_ROPE_DIM) ** -0.5
    (q_mla,) = matmul(cq, _mla_q_weight(p["mla_w_uq"][l]), [BF16],
                      epilogue=lambda acc, c, s: (_rope_slots(acc, c, s) * mla_scale,),
                      extras=[(cos_t, "row", 0), (sin_t, "row", 0)], name="mla_q_up")
    wk, wv = _mla_kv_weights(p["mla_w_ukv"][l])
    kpe_block = (MLA_Q_LORA + MLA_KV_LORA) // LANES

    def k_epilogue(acc, kpe, c, s):
        kr = _rope_slots(kpe, c, s)
        return (acc + jnp.concatenate([kr] * (acc.shape[1] // LANES), axis=1),)

    kpe = mcols[:, kpe_block * LANES:(kpe_block + 1) * LANES]
    (k_mla,) = matmul(ckv, wk, [BF16], epilogue=k_epilogue,
                      extras=[(kpe, "row", 0), (cos_t, "row", 0), (sin_t, "row", 0)], name="mla_k_up")
    (v_mla,) = matmul(ckv, wv, [BF16], name="mla_v_up")
    o_mla_out = mla_attention(q_mla, k_mla, v_mla, batch, seq)

    merged = merge_branches(h, [o_sb, o_rw, o_mla_out], w_in[:, o_gate:].astype(BF16),
                            [p["w_br_sb"][l].astype(BF16), p["w_br_rwkv"][l].astype(BF16),
                             p["w_br_mla"][l].astype(BF16)])
    (x,) = matmul(merged, p["w_out"][l].astype(BF16), [F32],
                  epilogue=lambda acc, res: (res + acc,), extras=[(x, "tile", 0)], name="out_proj")
    return x, v_first


def kernel(x, positions, attn_norm_g, w_in, rwkv_mu, rwkv_w0, rwkv_w2, rwkv_a0, rwkv_a2, rwkv_g2, rwkv_k_k, rwkv_k_a, rwkv_r_k, rwkv_ln_g, rwkv_ln_b, rwkv_v0, rwkv_v1, rwkv_v2, mla_q_norm_g, mla_w_uq, mla_kv_norm_g, mla_w_ukv, w_br_sb, w_br_rwkv, w_br_mla, w_out, ffn_norm_g, router_group_w, router_group_b, router_expert_w, router_expert_b, expert_w_gate, expert_w_up, expert_w_down, final_norm_g):
    p = dict(attn_norm_g=attn_norm_g, w_in=w_in, rwkv_mu=rwkv_mu, rwkv_w0=rwkv_w0, rwkv_w2=rwkv_w2,
             rwkv_a0=rwkv_a0, rwkv_a2=rwkv_a2, rwkv_g2=rwkv_g2, rwkv_k_k=rwkv_k_k, rwkv_k_a=rwkv_k_a,
             rwkv_r_k=rwkv_r_k, rwkv_ln_g=rwkv_ln_g, rwkv_ln_b=rwkv_ln_b, rwkv_v0=rwkv_v0,
             rwkv_v1=rwkv_v1, rwkv_v2=rwkv_v2, mla_q_norm_g=mla_q_norm_g, mla_w_uq=mla_w_uq,
             mla_kv_norm_g=mla_kv_norm_g, mla_w_ukv=mla_w_ukv, w_br_sb=w_br_sb, w_br_rwkv=w_br_rwkv,
             w_br_mla=w_br_mla, w_out=w_out)
    batch, seq, d = x.shape
    depth = w_in.shape[0]
    cos_t, sin_t = _rope_slot_tables(positions)
    xt = x.reshape(batch * seq, d)
    v_first = None
    for l in range(depth):
        xt, v_first = _mixer_block(xt, l, batch, seq, cos_t, sin_t, v_first, p)
        xt = hier_moe_residual(xt, ffn_norm_g[l], router_group_w[l], router_group_b[l],
                               router_expert_w[l], router_expert_b[l],
                               l, expert_w_gate, expert_w_up, expert_w_down)
    return rmsnorm(xt, final_norm_g, x.dtype, name="final_norm").reshape(batch, seq, d)
```

```python
import functools

import jax
import jax.numpy as jnp
from jax import lax
from jax.experimental import pallas as pl
from jax.experimental.pallas import tpu as pltpu

F32 = jnp.float32
BF16 = jnp.bfloat16
HIGHEST = lax.Precision.HIGHEST

NORM_EPS = 1e-6
HEADS = 16
HEAD_DIM = 64
WIDTH = HEADS * HEAD_DIM
LANES = 128
RWKV_DECAY_LORA = 96
RWKV_ICLR_LORA = 96
RWKV_GATE_LORA = 256
RWKV_LORA_PAD = 128
RWKV_GN_EPS = 64e-5
RWKV_CHUNK = 64
MLA_Q_LORA = 768
MLA_KV_LORA = 256
MLA_ROPE_DIM = 32
ROPE_THETA = 10000.0
N_GROUPS = 8
EXPERTS_PER_GROUP = 8
N_EXPERTS = N_GROUPS * EXPERTS_PER_GROUP
TOP_K = 2
MOE_ROWS = 256
WEIGHT_CHUNKS = 4
VMEM_LIMIT = 48 * 1024 * 1024


def _params(*sem):
    return pltpu.CompilerParams(dimension_semantics=sem, vmem_limit_bytes=VMEM_LIMIT)


def _pick(n, cands):
    for c in cands:
        if n % c == 0:
            return c
    raise ValueError(f"no tile for {n} in {cands}")


def _dot(x, y):
    return jnp.dot(x, y, preferred_element_type=F32)


def _dot_nt(x, y):
    return lax.dot_general(x, y, (((1,), (1,)), ((), ())), preferred_element_type=F32)


def _dot_tn(x, y):
    return lax.dot_general(x, y, (((0,), (0,)), ((), ())), preferred_element_type=F32)


def _split(x):
    hi = x.astype(BF16)
    return hi, (x - hi.astype(F32)).astype(BF16)


def _rmsnorm_body(x_ref, g_ref, o_ref):
    x = x_ref[...].astype(F32)
    y = x * lax.rsqrt(jnp.mean(x * x, axis=-1, keepdims=True) + NORM_EPS)
    o_ref[...] = (y * g_ref[...]).astype(o_ref.dtype)


def rmsnorm(x, g, out_dtype, *, width=None, col_block=0, name="rmsnorm"):
    m = x.shape[0]
    width = x.shape[1] if width is None else width
    tm = _pick(m, (512, 256, 128, 64, 32, 16, 8))
    return pl.pallas_call(
        _rmsnorm_body,
        grid=(m // tm,),
        in_specs=[pl.BlockSpec((tm, width), lambda i: (i, col_block)),
                  pl.BlockSpec((1, width), lambda i: (0, 0))],
        out_specs=pl.BlockSpec((tm, width), lambda i: (i, 0)),
        out_shape=jax.ShapeDtypeStruct((m, width), out_dtype),
        compiler_params=_params("parallel"),
        name=name,
    )(x, g.reshape(1, width).astype(F32))


def _mm_body(*refs, n_extra, epilogue):
    a_ref, w_ref = refs[:2]
    extra = [r[...] for r in refs[2:2 + n_extra]]
    outs = refs[2 + n_extra:]
    acc = _dot(a_ref[...], w_ref[...])
    res = epilogue(acc, *extra)
    for o, r in zip(outs, res):
        o[...] = r.astype(o.dtype)


def matmul(a, w, out_dtypes, *, epilogue=None, extras=(), a_col_block=0, name="matmul"):
    m = a.shape[0]
    k, n = w.shape
    tm = _pick(m, (1024, 512, 256, 128, 64, 32, 16, 8))
    tn = n if n <= 1280 else _pick(n, (512, 256, 128))
    if epilogue is None:
        epilogue = lambda acc: (acc,)
    in_specs = [pl.BlockSpec((tm, k), lambda i, j: (i, a_col_block)),
                pl.BlockSpec((k, tn), lambda i, j: (0, j))]
    args = [a, w]
    for arr, kind, off in extras:
        if kind == "row":
            in_specs.append(pl.BlockSpec((tm, arr.shape[1]), lambda i, j: (i, 0)))
        elif kind == "col":
            in_specs.append(pl.BlockSpec((1, tn), lambda i, j: (0, j)))
        else:
            in_specs.append(pl.BlockSpec((tm, tn), lambda i, j, off=off: (i, j + off)))
        args.append(arr)
    return pl.pallas_call(
        functools.partial(_mm_body, n_extra=len(extras), epilogue=epilogue),
        grid=(m // tm, n // tn),
        in_specs=in_specs,
        out_specs=[pl.BlockSpec((tm, tn), lambda i, j: (i, j)) for _ in out_dtypes],
        out_shape=[jax.ShapeDtypeStruct((m, n), dt) for dt in out_dtypes],
        compiler_params=_params("parallel", "parallel"),
        name=name,
    )(*args)


def _rope_slots(x, cos_t, sin_t):
    n = x.shape[-1]
    reps = n // LANES
    lane = lax.broadcasted_iota(jnp.int32, x.shape, 1) % LANES
    from_hi = pltpu.roll(x, n - MLA_ROPE_DIM // 2, 1)
    from_lo = pltpu.roll(x, MLA_ROPE_DIM // 2, 1)
    swapped = jnp.where(lane < HEAD_DIM + MLA_ROPE_DIM // 2, from_hi, from_lo)
    if reps > 1:
        cos_t = jnp.concatenate([cos_t] * reps, axis=1)
        sin_t = jnp.concatenate([sin_t] * reps, axis=1)
    return x * cos_t + swapped * sin_t


def _sb_body(q_ref, k_ref, v_ref, o_ref, *, t, cw):
    qi = pl.program_id(2)
    n_sub = t // cw
    first = lax.broadcasted_iota(jnp.int32, (cw, cw), 0)
    second = lax.broadcasted_iota(jnp.int32, (cw, cw), 1)
    before = second < first
    later = (first > second).astype(BF16)
    later2 = jnp.concatenate([later, later], axis=0)
    heads = [slice(hh * HEAD_DIM, (hh + 1) * HEAD_DIM) for hh in range(2)]

    def sweep(q, carry, ks, n_keys, diag_sub):
        z = [_dot_nt(q[hh], k_ref[0, pl.ds(ks, n_keys), h]) for hh, h in enumerate(heads)]
        acc = [carry[0], carry[2]]
        run = [carry[1], carry[3]]
        subs = n_keys // cw
        weights = [[None] * subs for _ in heads]
        for sub in reversed(range(subs)):
            for hh in range(2):
                zz = z[hh][:, sub * cw:(sub + 1) * cw]
                neg_abs = pltpu.bitcast(pltpu.bitcast(zz, jnp.uint32) | jnp.uint32(0x80000000), F32)
                sp = jnp.maximum(zz, 0.0) + jnp.log(1.0 + jnp.exp(neg_abs))
                mass = jnp.where(before, sp, 0.0) if sub == diag_sub else sp
                hi, lo = _split(mass)
                after = _dot(jnp.concatenate([hi, lo], axis=1), later2)
                wgt = jnp.exp(((zz - sp) - after) - run[hh])
                if sub == diag_sub:
                    wgt = jnp.where(before, wgt, 0.0)
                weights[hh][sub] = wgt.astype(BF16)
                run[hh] = run[hh] + (after[:, :1] + mass[:, :1])
        for hh, h in enumerate(heads):
            acc[hh] = acc[hh] + _dot(jnp.concatenate(weights[hh], axis=1), v_ref[0, pl.ds(ks, n_keys), h])
        return acc[0], run[0], acc[1], run[1]

    k0 = pl.multiple_of(qi * t, t)
    zeros = (jnp.zeros((cw, HEAD_DIM), F32), jnp.zeros((cw, 1), F32)) * 2
    groups = [sweep([q_ref[0, j * cw:(j + 1) * cw, h] for h in heads], zeros, k0, (j + 1) * cw, j)
              for j in range(n_sub)]
    carry = tuple(jnp.concatenate([g[c] for g in groups], axis=0) for c in range(4))
    q = [q_ref[0, :, h] for h in heads]
    carry = lax.fori_loop(
        0, qi, lambda it, c: sweep(q, c, pl.multiple_of((qi - 1 - it) * t, t), t, None), carry)
    o_ref[0] = jnp.concatenate([carry[0], carry[2]], axis=1).astype(o_ref.dtype)


def sb_attention(qkv, batch, seq):
    t = _pick(seq, (512, 256, 128))
    x = qkv.reshape(batch, seq, 3 * WIDTH)
    hp = WIDTH // LANES
    out = pl.pallas_call(
        functools.partial(_sb_body, t=t, cw=min(t, 256)),
        grid=(batch, hp, seq // t),
        in_specs=[pl.BlockSpec((1, t, LANES), lambda b, h, i: (b, i, h)),
                  pl.BlockSpec((1, seq, LANES), lambda b, h, i: (b, 0, hp + h)),
                  pl.BlockSpec((1, seq, LANES), lambda b, h, i: (b, 0, 2 * hp + h))],
        out_specs=pl.BlockSpec((1, t, LANES), lambda b, h, i: (b, i, h)),
        out_shape=jax.ShapeDtypeStruct((batch, seq, WIDTH), BF16),
        compiler_params=_params("parallel", "parallel", "parallel"),
        name="sb_attention",
    )(x, x, x)
    return out.reshape(batch * seq, WIDTH)


def _mla_body(q_ref, k_ref, v_ref, o_ref, *, t):
    qi = pl.program_id(2)
    rows = lax.broadcasted_iota(jnp.int32, (t, t), 0)
    cols = lax.broadcasted_iota(jnp.int32, (t, t), 1)
    q = [q_ref[0, :, hh * LANES:(hh + 1) * LANES] for hh in range(2)]

    def block(ks, carry, diagonal):
        s = [_dot_nt(q[hh], k_ref[0, pl.ds(ks, t), hh * LANES:(hh + 1) * LANES]) for hh in range(2)]
        out = []
        for hh in range(2):
            m, l, acc = carry[3 * hh:3 * hh + 3]
            ss = jnp.where(cols <= rows, s[hh], -1e30) if diagonal else s[hh]
            m_new = jnp.maximum(m, jnp.max(ss, axis=-1, keepdims=True))
            p = jnp.exp(ss - m_new)
            corr = jnp.exp(m - m_new)
            l = corr * l + jnp.sum(p, axis=-1, keepdims=True)
            v = v_ref[0, pl.ds(ks, t), hh * HEAD_DIM:(hh + 1) * HEAD_DIM]
            acc = corr * acc + _dot(p.astype(BF16), v)
            out += [m_new, l, acc]
        return tuple(out)

    init = (jnp.full((t, 1), -1e30, F32), jnp.zeros((t, 1), F32), jnp.zeros((t, HEAD_DIM), F32)) * 2
    carry = block(pl.multiple_of(qi * t, t), init, True)
    carry = lax.fori_loop(0, qi, lambda kb, c: block(pl.multiple_of(kb * t, t), c, False), carry)
    o_ref[0] = jnp.concatenate([carry[2] / carry[1], carry[5] / carry[4]], axis=1).astype(o_ref.dtype)


def mla_attention(q, k, v, batch, seq):
    t = _pick(seq, (512, 256, 128))
    hp = HEADS // 2
    out = pl.pallas_call(
        functools.partial(_mla_body, t=t),
        grid=(batch, hp, seq // t),
        in_specs=[pl.BlockSpec((1, t, 2 * LANES), lambda b, h, i: (b, i, h)),
                  pl.BlockSpec((1, seq, 2 * LANES), lambda b, h, i: (b, 0, h)),
                  pl.BlockSpec((1, seq, LANES), lambda b, h, i: (b, 0, h))],
        out_specs=pl.BlockSpec((1, t, LANES), lambda b, h, i: (b, i, h)),
        out_shape=jax.ShapeDtypeStruct((batch, seq, WIDTH), BF16),
        compiler_params=_params("parallel", "parallel", "parallel"),
        name="mla_attention",
    )(q.reshape(batch, seq, HEADS * LANES), k.reshape(batch, seq, HEADS * LANES),
      v.reshape(batch, seq, WIDTH))
    return out.reshape(batch * seq, WIDTH)


RWKV_IN = 3 * WIDTH + 2 * RWKV_LORA_PAD + RWKV_GATE_LORA


def _rwkv_prep_body(*refs, tiles_per_seq, has_vres):
    if has_vres:
        (cols_ref, prev_ref, mu_ref, w0_ref, w2_ref, a0_ref, a2_ref, g2_ref, kk_ref, ka_ref,
         v0_ref, v1_ref, v2_ref, vfirst_ref,
         r_out, lw_out, k_out, v_out, kk_out, a_out, g_out) = refs
    else:
        (cols_ref, prev_ref, mu_ref, w0_ref, w2_ref, a0_ref, a2_ref, g2_ref, kk_ref, ka_ref,
         r_out, lw_out, k_out, v_out, kk_out, a_out, g_out) = refs
    x = cols_ref[...]
    starts_seq = (pl.program_id(0) % tiles_per_seq) == 0
    last_prev = jnp.where(starts_seq, 0.0, prev_ref[7:8, :])
    first_row = lax.broadcasted_iota(jnp.int32, x.shape, 0) == 0
    prev = jnp.where(first_row, last_prev, pltpu.roll(x, 1, 0))
    x = x + (prev - x) * mu_ref[...]
    r = x[:, :WIDTH]
    k = x[:, WIDTH:2 * WIDTH]
    v = x[:, 2 * WIDTH:3 * WIDTH]
    o = 3 * WIDTH
    wd = x[:, o:o + RWKV_LORA_PAD]
    ad = x[:, o + RWKV_LORA_PAD:o + 2 * RWKV_LORA_PAD]
    gd = x[:, o + 2 * RWKV_LORA_PAD:]
    wpre = w0_ref[...] + _dot(jnp.tanh(wd).astype(BF16), w2_ref[...])
    w = -jax.nn.softplus(-wpre) - 0.5
    lw_out[...] = -jnp.exp(w)
    a = jax.nn.sigmoid(a0_ref[...] + _dot(ad.astype(BF16), a2_ref[...]))
    g_out[...] = _dot(jax.nn.sigmoid(gd).astype(BF16), g2_ref[...])
    if has_vres:
        low = _dot(v.astype(BF16), v1_ref[...])
        mix = jax.nn.sigmoid(v0_ref[...] + _dot(low.astype(BF16), v2_ref[...]))
        v = v + (vfirst_ref[...] - v) * mix
    r_out[...] = r
    v_out[...] = v
    kk_out[...] = k * kk_ref[...]
    k_out[...] = k * (1.0 + (a - 1.0) * ka_ref[...])
    a_out[...] = a


def rwkv_prep(cols, seq, mu, w0, w2, a0, a2, g2, k_k, k_a, vres):
    m = cols.shape[0]
    tm = _pick(seq, (256, 128, 64, 32, 16, 8))
    row = lambda t: t.reshape(1, -1).astype(F32)
    args = [cols, cols, row(mu), row(w0), w2, row(a0), a2, g2, row(k_k), row(k_a)]
    full = lambda arr: pl.BlockSpec(arr.shape, lambda i: (0, 0))
    tile = pl.BlockSpec((tm, WIDTH), lambda i: (i, 0))
    in_specs = [pl.BlockSpec((tm, RWKV_IN), lambda i: (i, 0)),
                pl.BlockSpec((8, RWKV_IN), lambda i: (jnp.maximum(i * (tm // 8) - 1, 0), 0))]
    in_specs += [full(t) for t in args[2:]]
    if vres is not None:
        v0, v1, v2, v_first = vres
        extra = [row(v0), v1, v2]
        args += extra + [v_first]
        in_specs += [full(t) for t in extra] + [tile]
    return pl.pallas_call(
        functools.partial(_rwkv_prep_body, tiles_per_seq=seq // tm, has_vres=vres is not None),
        grid=(m // tm,),
        in_specs=in_specs,
        out_specs=[tile] * 7,
        out_shape=[jax.ShapeDtypeStruct((m, WIDTH), F32)] * 7,
        compiler_params=_params("parallel"),
        name="rwkv_prep",
    )(*args)


def _halves(x):
    return x[:, :HEAD_DIM], x[:, HEAD_DIM:]


def _head_sum(x, left):
    s0 = jnp.sum(jnp.where(left, x, 0.0), axis=-1, keepdims=True)
    s1 = jnp.sum(jnp.where(left, 0.0, x), axis=-1, keepdims=True)
    return jnp.where(left, s0, s1)


def _dot3(x, y, dot):
    xh, xl = _split(x)
    yh, yl = _split(y)
    return dot(xh, yh) + (dot(xh, yl) + dot(xl, yh))


def _rwkv_chunk_terms(r, lw, k, v, kk_raw, a):
    c = r.shape[0]
    left = lax.broadcasted_iota(jnp.int32, (c, LANES), 1) < HEAD_DIM
    trow = lax.broadcasted_iota(jnp.int32, (c, LANES), 0)
    kk = kk_raw / jnp.maximum(jnp.sqrt(_head_sum(kk_raw * kk_raw, left)), 1e-12)
    beta = kk * a
    cl = lw
    shift = 1
    while shift < c:
        cl = cl + jnp.where(trow >= shift, pltpu.roll(cl, shift, 0), 0.0)
        shift *= 2
    a_t = -kk * jnp.exp(cl - lw)
    r_t = r * jnp.exp(cl)
    inv = jnp.exp(-cl)
    cl_end = cl[c - 1:c, :]
    tail = jnp.exp(cl_end - cl)
    full = (a_t, r_t, beta * inv, k * inv, v, beta * tail, k * tail, jnp.exp(cl_end))
    return [tuple(_halves(x)[hh] for x in full) for hh in range(2)]


def _rwkv_masks(c):
    ti = lax.broadcasted_iota(jnp.int32, (c, c), 0)
    tj = lax.broadcasted_iota(jnp.int32, (c, c), 1)
    merges = []
    half = 1
    while half < c:
        merges.append((ti // (2 * half) == tj // (2 * half)) & (ti % (2 * half) >= half)
                      & (tj % (2 * half) < half))
        half *= 2
    return ti > tj, ti >= tj, ti == tj, merges


def _rwkv_chain(out, head_terms, masks):
    at_h, rt_h, bt_h, kt_h, v_h, bbar_h, kbar_h, gend_h = head_terms
    strict, incl, eye, merges = masks
    c = at_h.shape[0]
    ar = jnp.concatenate([at_h, rt_h], axis=0).astype(BF16)
    bk = jnp.concatenate([bt_h, kt_h], axis=0).astype(BF16)
    pair = _dot_nt(ar, bk)
    yield
    lower = jnp.where(strict, pair[:c, :c], 0.0)
    akv = _dot(jnp.where(strict, pair[:c, c:], 0.0).astype(BF16), v_h.astype(BF16))
    tinv = eye.astype(F32) + jnp.where(merges[0], lower, 0.0)
    for off in merges[1:]:
        tb = tinv.astype(BF16)
        step = _dot(tb, jnp.where(off, lower, 0.0).astype(BF16))
        yield
        tinv = tinv + _dot(step.astype(BF16), tb)
        yield
    solved = _dot(tinv.astype(BF16), jnp.concatenate([at_h, akv], axis=1).astype(BF16))
    yield
    ta, w = _halves(solved)
    rb = jnp.where(incl, pair[c:, :c], 0.0).astype(BF16)
    rk = jnp.where(incl, pair[c:, c:], 0.0).astype(BF16)
    wv = jnp.concatenate([w, v_h], axis=0).astype(BF16)
    q = rt_h + _dot(rb, ta.astype(BF16))
    y0 = _dot(jnp.concatenate([rb, rk], axis=1), wv)
    g = jnp.where(eye, gend_h, 0.0) + _dot_tn(ta.astype(BF16), bbar_h.astype(BF16))
    d = _dot_tn(wv, jnp.concatenate([bbar_h, kbar_h], axis=0).astype(BF16))
    out.append((q, y0, g, d))


def _rwkv_state_chain(hh, results, state_ref, ys):
    state = state_ref[hh]
    for ci, res in enumerate(results):
        while not res[hh]:
            yield
        q, y0, g, d = res[hh][0]
        ys[ci][hh] = _dot3(q, state, _dot_nt) + y0
        state = _dot3(state, g, _dot) + d
        yield
    state_ref[hh] = state


def _lockstep(chains):
    live = list(chains)
    while live:
        still = []
        for ch in live:
            try:
                next(ch)
                still.append(ch)
            except StopIteration:
                pass
        live = still


def _rwkv_scan_body(r_ref, lw_ref, k_ref, v_ref, kk_ref, a_ref, g_ref, lng_ref, lnb_ref, rk_ref,
                    o_ref, state_ref, *, chunk):
    @pl.when(pl.program_id(2) == 0)
    def _():
        state_ref[...] = jnp.zeros_like(state_ref)

    n_chunks = r_ref.shape[1] // chunk
    left = lax.broadcasted_iota(jnp.int32, (chunk, LANES), 1) < HEAD_DIM
    masks = _rwkv_masks(chunk)
    rows = [slice(ci * chunk, (ci + 1) * chunk) for ci in range(n_chunks)]
    results = [[[], []] for _ in range(n_chunks)]
    chains = []
    for ci in range(n_chunks):
        terms = _rwkv_chunk_terms(r_ref[0, rows[ci], :], lw_ref[0, rows[ci], :], k_ref[0, rows[ci], :],
                                  v_ref[0, rows[ci], :], kk_ref[0, rows[ci], :], a_ref[0, rows[ci], :])
        chains += [_rwkv_chain(results[ci][hh], terms[hh], masks) for hh in range(2)]
    _lockstep(chains)
    ys = [[None, None] for _ in range(n_chunks)]
    _lockstep([_rwkv_state_chain(hh, results, state_ref, ys) for hh in range(2)])

    for ci in range(n_chunks):
        y = jnp.concatenate(ys[ci], axis=1)
        r, k, v = r_ref[0, rows[ci], :], k_ref[0, rows[ci], :], v_ref[0, rows[ci], :]
        mean = _head_sum(y, left) * (1.0 / HEAD_DIM)
        var = _head_sum(jnp.square(y - mean), left) * (1.0 / HEAD_DIM)
        y = (y - mean) * lax.rsqrt(var + RWKV_GN_EPS) * lng_ref[...] + lnb_ref[...]
        bonus = _head_sum(r * k * rk_ref[...], left) * v
        o_ref[0, rows[ci], :] = ((y + bonus) * g_ref[0, rows[ci], :]).astype(o_ref.dtype)


def rwkv_scan(r, lw, k, v, kk, a, g, ln_g, ln_b, r_k, batch, seq):
    chunk = min(RWKV_CHUNK, seq)
    rows = _pick(seq, (512, 256, 128, 64, 32, 16, 8))
    hp = WIDTH // LANES
    seq_spec = pl.BlockSpec((1, rows, LANES), lambda b, h, c: (b, c, h))
    par_spec = pl.BlockSpec((1, LANES), lambda b, h, c: (0, h))
    shaped = [t.reshape(batch, seq, WIDTH) for t in (r, lw, k, v, kk, a, g)]
    pars = [t.reshape(1, WIDTH).astype(F32) for t in (ln_g, ln_b, r_k)]
    out = pl.pallas_call(
        functools.partial(_rwkv_scan_body, chunk=chunk),
        grid=(batch, hp, seq // rows),
        in_specs=[seq_spec] * 7 + [par_spec] * 3,
        out_specs=seq_spec,
        out_shape=jax.ShapeDtypeStruct((batch, seq, WIDTH), BF16),
        scratch_shapes=[pltpu.VMEM((2, HEAD_DIM, HEAD_DIM), F32)],
        compiler_params=_params("parallel", "parallel", "arbitrary"),
        name="rwkv_scan",
    )(*shaped, *pars)
    return out.reshape(batch * seq, WIDTH)


def _merge_body(h, a1, a2, a3, g1, g2, g3, w1, w2, w3, o_ref):
    hh = h[...]
    acc = jax.nn.sigmoid(_dot(hh, g1[...])) * _dot(a1[...], w1[...])
    acc += jax.nn.sigmoid(_dot(hh, g2[...])) * _dot(a2[...], w2[...])
    acc += jax.nn.sigmoid(_dot(hh, g3[...])) * _dot(a3[...], w3[...])
    o_ref[...] = acc.astype(o_ref.dtype)


def merge_branches(h, outs, w_gates, weights):
    m, d = h.shape
    tm = _pick(m, (512, 256, 128, 64, 32, 16, 8))
    tn = _pick(d, (512, 256, 128))
    nb = d // tn
    h_spec = pl.BlockSpec((tm, d), lambda i, j: (i, 0))
    a_spec = pl.BlockSpec((tm, WIDTH), lambda i, j: (i, 0))
    w_spec = pl.BlockSpec((WIDTH, tn), lambda i, j: (0, j))
    g_specs = [pl.BlockSpec((d, tn), lambda i, j, b=b: (0, j + b * nb)) for b in range(3)]
    return pl.pallas_call(
        _merge_body,
        grid=(m // tm, nb),
        in_specs=[h_spec] + [a_spec] * 3 + g_specs + [w_spec] * 3,
        out_specs=pl.BlockSpec((tm, tn), lambda i, j: (i, j)),
        out_shape=jax.ShapeDtypeStruct((m, d), BF16),
        compiler_params=_params("parallel", "parallel"),
        name="merge_branches",
    )(h, *outs, w_gates, w_gates, w_gates, *weights)


ROUTER_PAD = 128


def _slab_cols(ref, n, slabs, first=0, pitch=None):
    pitch = slabs if pitch is None else pitch
    return jnp.concatenate([ref[pl.ds(first + s, n, stride=pitch), :] for s in range(slabs)], axis=1)


def _store_slabs(ref, value, slabs):
    n = value.shape[0]
    for s in range(slabs):
        ref[pl.ds(s, n, stride=slabs), :] = value[:, s * LANES:(s + 1) * LANES].astype(ref.dtype)


def _pack_bf16_pairs(h):
    half = h.shape[1] // 2
    bits = pltpu.bitcast(h.astype(BF16).astype(F32), jnp.uint32)
    return (bits[:, :half] >> 16) | bits[:, half:]


def _unpack_bf16_pairs(words):
    lo = pltpu.bitcast(words << 16, F32)
    hi = pltpu.bitcast(words & jnp.uint32(0xFFFF0000), F32)
    return jnp.concatenate([lo, hi], axis=1).astype(BF16)


def _router_body(x_ref, g_ref, w_ref, b_ref, h_ref, logit_ref, *, slabs):
    x = x_ref[...]
    y = x * lax.rsqrt(jnp.mean(x * x, axis=-1, keepdims=True) + NORM_EPS)
    h = y * g_ref[...]
    _store_slabs(h_ref, _pack_bf16_pairs(h), slabs)
    logit_ref[...] = jnp.dot(h, w_ref[...], preferred_element_type=F32, precision=HIGHEST) + b_ref[...]


def router(x, g, w_pad, b_pad):
    m, d = x.shape
    slabs = d // (2 * LANES)
    tm = _pick(m, (256, 128, 64, 32, 16, 8))
    return pl.pallas_call(
        functools.partial(_router_body, slabs=slabs),
        grid=(m // tm,),
        in_specs=[pl.BlockSpec((tm, d), lambda i: (i, 0)),
                  pl.BlockSpec((1, d), lambda i: (0, 0)),
                  pl.BlockSpec((d, ROUTER_PAD), lambda i: (0, 0)),
                  pl.BlockSpec((1, ROUTER_PAD), lambda i: (0, 0))],
        out_specs=[pl.BlockSpec((tm * slabs, LANES), lambda i: (i, 0)),
                   pl.BlockSpec((tm, ROUTER_PAD), lambda i: (i, 0))],
        out_shape=[jax.ShapeDtypeStruct((m * slabs, LANES), jnp.uint32),
                   jax.ShapeDtypeStruct((m, ROUTER_PAD), F32)],
        compiler_params=_params("parallel"),
        name="ffn_norm_router",
    )(x, g.reshape(1, d).astype(F32), w_pad, b_pad)


def _row_gather(idx_ref, first, n, src_ref, buf_ref, sem, slabs):
    def copy(r):
        src = pl.multiple_of(idx_ref[first + r] * slabs, slabs)
        return pltpu.make_async_copy(src_ref.at[pl.ds(src, slabs)],
                                     buf_ref.at[pl.ds(r * slabs, slabs)], sem)

    def start(lo=None, hi=None):
        if lo is None:
            lax.fori_loop(0, n, lambda r, c: (copy(r).start(), c)[1], 0, unroll=8)
        else:
            for r in range(lo, hi):
                copy(r).start()

    def wait():
        pltpu.make_async_copy(src_ref.at[pl.ds(0, n * slabs)], buf_ref, sem).wait()

    return start, wait


def _expert_body(be_ref, nb_ref, next_ref, tok_ref, h_ref, wg_hbm, wu_hbm, wd_hbm, o_ref,
                 xbuf, wg_f, wu_f, wd_f, wg_s, wu_s, wd_s, sem, wsem, *, layer, in_slabs, slabs):
    i = pl.program_id(0)
    n_blocks = nb_ref[0]
    slot = i % 2

    def gather(block, slot):
        return _row_gather(tok_ref, block * MOE_ROWS, MOE_ROWS, h_ref, xbuf.at[slot], sem.at[slot], in_slabs)

    pairs = ((wg_hbm, wg_f), (wu_hbm, wu_f), (wd_hbm, wd_f))

    def start_weights(expert):
        for j, (src, dst) in enumerate(pairs):
            rows = dst.shape[0] // WEIGHT_CHUNKS
            for c in range(WEIGHT_CHUNKS):
                part = pl.ds(c * rows, rows)
                pltpu.make_async_copy(src.at[layer, expert, part], dst.at[part], wsem.at[j]).start(priority=1)

    def wait_weights(expert):
        for j, (src, dst) in enumerate(pairs):
            pltpu.make_async_copy(src.at[layer, expert], dst, wsem.at[j]).wait()

    e = be_ref[i]

    @pl.when(i == 0)
    def _():
        gather(0, 0)[0]()
        start_weights(e)

    @pl.when((i < n_blocks) & ((i == 0) | (e != be_ref[jnp.maximum(i - 1, 0)])))
    def _():
        wait_weights(e)
        wg_s[...] = wg_f[...].astype(BF16)
        wu_s[...] = wu_f[...].astype(BF16)
        wd_s[...] = wd_f[...].astype(BF16)
        upcoming = next_ref[i]

        @pl.when(upcoming >= 0)
        def _():
            start_weights(upcoming)

    @pl.when(i < n_blocks)
    def _():
        start_next, wait_next = gather(jnp.minimum(i + 1, n_blocks - 1), 1 - slot)
        quarter = MOE_ROWS // 4
        gather(i, slot)[1]()
        x = _unpack_bf16_pairs(_slab_cols(xbuf.at[slot], MOE_ROWS, in_slabs))
        start_next(0, quarter)
        gate = _dot(x, wg_s[...])
        start_next(quarter, 2 * quarter)
        up = _dot(x, wu_s[...])
        start_next(2 * quarter, 3 * quarter)
        mid = (jax.nn.silu(gate) * up).astype(BF16)
        y = _dot(mid, wd_s[...])
        start_next(3 * quarter, MOE_ROWS)
        _store_slabs(o_ref, y, slabs)

        @pl.when(i + 1 == n_blocks)
        def _():
            wait_next()

    @pl.when(i >= n_blocks)
    def _():
        o_ref[...] = jnp.zeros_like(o_ref)


def expert_ffn(h_rows, row_tok, block_e, next_e, n_blocks, layer, w_gate, w_up, w_down):
    _, n_exp, d, ff = w_gate.shape
    slabs = d // LANES
    in_slabs = d // (2 * LANES)
    p = row_tok.shape[0]
    hbm = pl.BlockSpec(memory_space=pl.ANY)
    return pl.pallas_call(
        functools.partial(_expert_body, layer=layer, in_slabs=in_slabs, slabs=slabs),
        grid_spec=pltpu.PrefetchScalarGridSpec(
            num_scalar_prefetch=4,
            grid=(p // MOE_ROWS,),
            in_specs=[hbm, hbm, hbm, hbm],
            out_specs=pl.BlockSpec((MOE_ROWS * slabs, LANES), lambda i, be, nb, nxt, tok: (i, 0)),
            scratch_shapes=[pltpu.VMEM((2, MOE_ROWS * in_slabs, LANES), jnp.uint32),
                            pltpu.VMEM((d, ff), F32), pltpu.VMEM((d, ff), F32), pltpu.VMEM((ff, d), F32),
                            pltpu.VMEM((d, ff), BF16), pltpu.VMEM((d, ff), BF16),
                            pltpu.VMEM((ff, d), BF16),
                            pltpu.SemaphoreType.DMA((2,)), pltpu.SemaphoreType.DMA((3,))]),
        out_shape=jax.ShapeDtypeStruct((p * slabs, LANES), F32),
        compiler_params=_params("arbitrary"),
        name="expert_ffn",
    )(block_e, n_blocks, next_e, row_tok, h_rows, w_gate, w_up, w_down)


def _combine_body(dest_ref, x_ref, gate_ref, y_ref, o_ref, ybuf, sem, *, tm, slabs):
    i = pl.program_id(0)
    slot = i % 2

    def gather(step, slot):
        return _row_gather(dest_ref, step * tm * TOP_K, tm * TOP_K, y_ref, ybuf.at[slot], sem.at[slot], slabs)

    @pl.when(i == 0)
    def _():
        gather(0, 0)[0]()

    @pl.when(i + 1 < pl.num_programs(0))
    def _():
        gather(i + 1, 1 - slot)[0]()

    gather(i, slot)[1]()
    buf = ybuf.at[slot]
    gate = gate_ref[...]
    g0, g1 = gate[:, 0:1], gate[:, 1:2]
    for s in range(slabs):
        y0 = buf[pl.ds(s, tm, stride=slabs), :]
        y1 = buf[pl.ds(tm * slabs + s, tm, stride=slabs), :]
        cols = slice(s * LANES, (s + 1) * LANES)
        o_ref[:, cols] = x_ref[:, cols] + (g0 * y0 + g1 * y1)


def moe_combine(x, y_rows, dest, gate):
    m, d = x.shape
    slabs = d // LANES
    tm = _pick(m, (256, 128, 64, 32, 16, 8))
    tile = lambda i, dest: (i, 0)
    return pl.pallas_call(
        functools.partial(_combine_body, tm=tm, slabs=slabs),
        grid_spec=pltpu.PrefetchScalarGridSpec(
            num_scalar_prefetch=1,
            grid=(m // tm,),
            in_specs=[pl.BlockSpec((tm, d), tile), pl.BlockSpec((tm, TOP_K), tile),
                      pl.BlockSpec(memory_space=pl.ANY)],
            out_specs=pl.BlockSpec((tm, d), tile),
            scratch_shapes=[pltpu.VMEM((2, tm * TOP_K * slabs, LANES), F32),
                            pltpu.SemaphoreType.DMA((2,))]),
        out_shape=jax.ShapeDtypeStruct((m, d), F32),
        compiler_params=_params("arbitrary"),
        name="moe_combine",
    )(dest.reshape(m // tm, tm, TOP_K).transpose(0, 2, 1).reshape(m * TOP_K), x, gate, y_rows)


def _routing_tables(logits):
    t = logits.shape[0]
    n_assign = t * TOP_K
    tok = jnp.arange(t)
    g_logits = logits[:, :N_GROUPS]
    grp = jnp.argmax(g_logits, axis=-1)
    p_grp = jax.nn.softmax(g_logits, axis=-1)[tok, grp]
    e_logits = logits[:, N_GROUPS:N_GROUPS + N_EXPERTS].reshape(t, N_GROUPS, EXPERTS_PER_GROUP)
    p_in, idx_in = lax.top_k(jax.nn.softmax(e_logits[tok, grp], axis=-1), TOP_K)
    gate = p_grp[:, None] * p_in / jnp.sum(p_in, axis=-1, keepdims=True)
    flat_e = (grp[:, None] * EXPERTS_PER_GROUP + idx_in).reshape(n_assign).astype(jnp.int32)
    onehot = (flat_e[:, None] == jnp.arange(N_EXPERTS, dtype=jnp.int32)[None, :]).astype(jnp.int32)
    rank = jnp.sum((jnp.cumsum(onehot, axis=0) - onehot) * onehot, axis=1)
    counts = jnp.sum(onehot, axis=0)
    padded = (counts + MOE_ROWS - 1) // MOE_ROWS * MOE_ROWS
    pend = jnp.cumsum(padded)
    dest = ((pend - padded)[flat_e] + rank).astype(jnp.int32)
    n_blocks_max = -(-n_assign // MOE_ROWS) + N_EXPERTS
    row_tok = jnp.zeros((n_blocks_max * MOE_ROWS,), jnp.int32).at[dest].set(
        jnp.repeat(tok, TOP_K).astype(jnp.int32))
    block_e = jnp.minimum(
        jnp.searchsorted(pend, jnp.arange(n_blocks_max) * MOE_ROWS, side="right"),
        N_EXPERTS - 1).astype(jnp.int32)
    n_blocks = (pend[-1:] // MOE_ROWS).astype(jnp.int32)
    after = jnp.searchsorted(block_e, block_e, side="right")
    next_e = jnp.where(after < n_blocks[0], block_e[jnp.minimum(after, n_blocks_max - 1)], -1)
    return gate.astype(F32), dest.reshape(t, TOP_K), row_tok, block_e, next_e.astype(jnp.int32), n_blocks


def hier_moe_residual(x, norm_g, rg_w, rg_b, re_w, re_b, layer, w_gate, w_up, w_down):
    t, d = x.shape
    w_pad = jnp.zeros((d, ROUTER_PAD), F32).at[:, :N_GROUPS].set(rg_w)
    w_pad = w_pad.at[:, N_GROUPS:N_GROUPS + N_EXPERTS].set(re_w)
    b_pad = jnp.zeros((1, ROUTER_PAD), F32).at[0, :N_GROUPS].set(rg_b)
    b_pad = b_pad.at[0, N_GROUPS:N_GROUPS + N_EXPERTS].set(re_b)
    h_rows, logits = router(x, norm_g, w_pad, b_pad)
    gate, dest, row_tok, block_e, next_e, n_blocks = _routing_tables(logits)
    y_rows = expert_ffn(h_rows, row_tok, block_e, next_e, n_blocks, layer, w_gate, w_up, w_down)
    return moe_combine(x, y_rows, dest, gate)


def _mla_q_weight(w_uq):
    w = w_uq.reshape(MLA_Q_LORA, HEADS, HEAD_DIM + MLA_ROPE_DIM)
    w = jnp.pad(w, ((0, 0), (0, 0), (0, LANES - HEAD_DIM - MLA_ROPE_DIM)))
    return w.reshape(MLA_Q_LORA, HEADS * LANES).astype(BF16)


def _mla_kv_weights(w_ukv):
    w = w_ukv.reshape(MLA_KV_LORA, HEADS, 2 * HEAD_DIM)
    wk = jnp.pad(w[:, :, :HEAD_DIM], ((0, 0), (0, 0), (0, LANES - HEAD_DIM)))
    wv = w[:, :, HEAD_DIM:]
    return (wk.reshape(MLA_KV_LORA, HEADS * LANES).astype(BF16),
            wv.reshape(MLA_KV_LORA, WIDTH).astype(BF16))


def _rope_slot_tables(positions):
    half = MLA_ROPE_DIM // 2
    inv_freq = ROPE_THETA ** (-jnp.arange(0, MLA_ROPE_DIM, 2, dtype=F32) / MLA_ROPE_DIM)
    ang = positions.astype(F32).reshape(-1, 1) * inv_freq[None, :]
    cos, sin = jnp.cos(ang), jnp.sin(ang)
    t = ang.shape[0]
    pad = jnp.zeros((t, LANES - HEAD_DIM - 2 * half), F32)
    cos_t = jnp.concatenate([jnp.ones((t, HEAD_DIM), F32), cos, cos, pad], axis=1)
    sin_t = jnp.concatenate([jnp.zeros((t, HEAD_DIM), F32), -sin, sin, pad], axis=1)
    return cos_t, sin_t


def _mixer_block(x, l, batch, seq, cos_t, sin_t, v_first, p):
    t, d = x.shape
    h = rmsnorm(x, p["attn_norm_g"][l], BF16, name="attn_norm")
    w_in = p["w_in"][l]
    o_rwkv = 3 * WIDTH
    n_rwkv = 3 * WIDTH + RWKV_DECAY_LORA + RWKV_ICLR_LORA + RWKV_GATE_LORA
    o_mla = o_rwkv + n_rwkv
    n_mla = MLA_Q_LORA + MLA_KV_LORA + MLA_ROPE_DIM
    o_gate = o_mla + n_mla

    qscale = jnp.concatenate([jnp.full((1, WIDTH), HEAD_DIM ** -0.5, F32), jnp.ones((1, 2 * WIDTH), F32)], axis=1)
    (qkv,) = matmul(h, w_in[:, :o_rwkv].astype(BF16), [BF16],
                    epilogue=lambda acc, s: (acc * s,), extras=[(qscale, "col", 0)], name="proj_sb")
    o_sb = sb_attention(qkv, batch, seq)

    wr = w_in[:, o_rwkv:o_mla]
    zpad = jnp.zeros((d, RWKV_LORA_PAD - RWKV_DECAY_LORA), F32)
    c0 = 3 * WIDTH
    w_rwkv = jnp.concatenate([wr[:, :c0], wr[:, c0:c0 + RWKV_DECAY_LORA], zpad,
                              wr[:, c0 + RWKV_DECAY_LORA:c0 + 2 * RWKV_DECAY_LORA], zpad,
                              wr[:, c0 + 2 * RWKV_DECAY_LORA:]], axis=1).astype(BF16)
    (cols,) = matmul(h, w_rwkv, [F32], name="proj_rwkv")
    mu = p["rwkv_mu"][l]
    mpad = jnp.zeros((RWKV_LORA_PAD - RWKV_DECAY_LORA,), F32)
    mu_pad = jnp.concatenate([mu[:c0], mu[c0:c0 + RWKV_DECAY_LORA], mpad,
                              mu[c0 + RWKV_DECAY_LORA:c0 + 2 * RWKV_DECAY_LORA], mpad,
                              mu[c0 + 2 * RWKV_DECAY_LORA:]])
    rpad = ((0, RWKV_LORA_PAD - RWKV_DECAY_LORA), (0, 0))
    vres = None
    if l > 0:
        vres = (p["rwkv_v0"][l - 1], p["rwkv_v1"][l - 1].astype(BF16), p["rwkv_v2"][l - 1].astype(BF16), v_first)
    r, lw, k, v, kk, a, g = rwkv_prep(
        cols, seq, mu_pad, p["rwkv_w0"][l], jnp.pad(p["rwkv_w2"][l], rpad).astype(BF16),
        p["rwkv_a0"][l], jnp.pad(p["rwkv_a2"][l], rpad).astype(BF16), p["rwkv_g2"][l].astype(BF16),
        p["rwkv_k_k"][l], p["rwkv_k_a"][l], vres)
    if l == 0:
        v_first = v
    o_rw = rwkv_scan(r, lw, k, v, kk, a, g, p["rwkv_ln_g"][l], p["rwkv_ln_b"][l], p["rwkv_r_k"][l],
                     batch, seq)

    wm = w_in[:, o_mla:o_gate]
    w_mla = jnp.concatenate([wm[:, :MLA_Q_LORA + MLA_KV_LORA], jnp.zeros((d, HEAD_DIM), F32),
                             wm[:, MLA_Q_LORA + MLA_KV_LORA:],
                             jnp.zeros((d, LANES - HEAD_DIM - MLA_ROPE_DIM), F32)], axis=1).astype(BF16)
    (mcols,) = matmul(h, w_mla, [F32], name="proj_mla")
    cq = rmsnorm(mcols, p["mla_q_norm_g"][l], BF16, width=MLA_Q_LORA, col_block=0, name="mla_q_norm")
    ckv = rmsnorm(mcols, p["mla_kv_norm_g"][l], BF16, width=MLA_KV_LORA,
                  col_block=MLA_Q_LORA // MLA_KV_LORA, name="mla_kv_norm")
    mla_scale = (HEAD_DIM + MLA_ROPE_DIM) ** -0.5
    (q_mla,) = matmul(cq, _mla_q_weight(p["mla_w_uq"][l]), [BF16],
                      epilogue=lambda acc, c, s: (_rope_slots(acc, c, s) * mla_scale,),
                      extras=[(cos_t, "row", 0), (sin_t, "row", 0)], name="mla_q_up")
    wk, wv = _mla_kv_weights(p["mla_w_ukv"][l])
    kpe_block = (MLA_Q_LORA + MLA_KV_LORA) // LANES

    def k_epilogue(acc, kpe, c, s):
        kr = _rope_slots(kpe, c, s)
        return (acc + jnp.concatenate([kr] * (acc.shape[1] // LANES), axis=1),)

    kpe = mcols[:, kpe_block * LANES:(kpe_block + 1) * LANES]
    (k_mla,) = matmul(ckv, wk, [BF16], epilogue=k_epilogue,
                      extras=[(kpe, "row", 0), (cos_t, "row", 0), (sin_t, "row", 0)], name="mla_k_up")
    (v_mla,) = matmul(ckv, wv, [BF16], name="mla_v_up")
    o_mla_out = mla_attention(q_mla, k_mla, v_mla, batch, seq)

    merged = merge_branches(h, [o_sb, o_rw, o_mla_out], w_in[:, o_gate:].astype(BF16),
                            [p["w_br_sb"][l].astype(BF16), p["w_br_rwkv"][l].astype(BF16),
                             p["w_br_mla"][l].astype(BF16)])
    (x,) = matmul(merged, p["w_out"][l].astype(BF16), [F32],
                  epilogue=lambda acc, res: (res + acc,), extras=[(x, "tile", 0)], name="out_proj")
    return x, v_first


def kernel(x, positions, attn_norm_g, w_in, rwkv_mu, rwkv_w0, rwkv_w2, rwkv_a0, rwkv_a2, rwkv_g2, rwkv_k_k, rwkv_k_a, rwkv_r_k, rwkv_ln_g, rwkv_ln_b, rwkv_v0, rwkv_v1, rwkv_v2, mla_q_norm_g, mla_w_uq, mla_kv_norm_g, mla_w_ukv, w_br_sb, w_br_rwkv, w_br_mla, w_out, ffn_norm_g, router_group_w, router_group_b, router_expert_w, router_expert_b, expert_w_gate, expert_w_up, expert_w_down, final_norm_g):
    p = dict(attn_norm_g=attn_norm_g, w_in=w_in, rwkv_mu=rwkv_mu, rwkv_w0=rwkv_w0, rwkv_w2=rwkv_w2,
             rwkv_a0=rwkv_a0, rwkv_a2=rwkv_a2, rwkv_g2=rwkv_g2, rwkv_k_k=rwkv_k_k, rwkv_k_a=rwkv_k_a,
             rwkv_r_k=rwkv_r_k, rwkv_ln_g=rwkv_ln_g, rwkv_ln_b=rwkv_ln_b, rwkv_v0=rwkv_v0,
             rwkv_v1=rwkv_v1, rwkv_v2=rwkv_v2, mla_q_norm_g=mla_q_norm_g, mla_w_uq=mla_w_uq,
             mla_kv_norm_g=mla_kv_norm_g, mla_w_ukv=mla_w_ukv, w_br_sb=w_br_sb, w_br_rwkv=w_br_rwkv,
             w_br_mla=w_br_mla, w_out=w_out)
    batch, seq, d = x.shape
    depth = w_in.shape[0]
    cos_t, sin_t = _rope_slot_tables(positions)
    xt = x.reshape(batch * seq, d)
    v_first = None
    for l in range(depth):
        xt, v_first = _mixer_block(xt, l, batch, seq, cos_t, sin_t, v_first, p)
        xt = hier_moe_residual(xt, ffn_norm_g[l], router_group_w[l], router_group_b[l],
                               router_expert_w[l], router_expert_b[l],
                               l, expert_w_gate, expert_w_up, expert_w_down)
    return rmsnorm(xt, final_norm_g, x.dtype, name="final_norm").reshape(batch, seq, d)
```

```python
import functools

import jax
import jax.numpy as jnp
from jax import lax
from jax.experimental import pallas as pl
from jax.experimental.pallas import tpu as pltpu

F32 = jnp.float32
BF16 = jnp.bfloat16
HIGHEST = lax.Precision.HIGHEST

NORM_EPS = 1e-6
HEADS = 16
HEAD_DIM = 64
WIDTH = HEADS * HEAD_DIM
LANES = 128
RWKV_DECAY_LORA = 96
RWKV_ICLR_LORA = 96
RWKV_GATE_LORA = 256
RWKV_LORA_PAD = 128
RWKV_GN_EPS = 64e-5
RWKV_CHUNK = 64
MLA_Q_LORA = 768
MLA_KV_LORA = 256
MLA_ROPE_DIM = 32
ROPE_THETA = 10000.0
N_GROUPS = 8
EXPERTS_PER_GROUP = 8
N_EXPERTS = N_GROUPS * EXPERTS_PER_GROUP
TOP_K = 2
MOE_ROWS = 128
VMEM_LIMIT = 48 * 1024 * 1024


def _params(*sem):
    return pltpu.CompilerParams(dimension_semantics=sem, vmem_limit_bytes=VMEM_LIMIT)


def _pick(n, cands):
    for c in cands:
        if n % c == 0:
            return c
    raise ValueError(f"no tile for {n} in {cands}")


def _dot(x, y):
    return jnp.dot(x, y, preferred_element_type=F32)


def _dot_nt(x, y):
    return lax.dot_general(x, y, (((1,), (1,)), ((), ())), preferred_element_type=F32)


def _dot_tn(x, y):
    return lax.dot_general(x, y, (((0,), (0,)), ((), ())), preferred_element_type=F32)


def _split(x):
    hi = x.astype(BF16)
    return hi, (x - hi.astype(F32)).astype(BF16)


def _rmsnorm_body(x_ref, g_ref, o_ref):
    x = x_ref[...].astype(F32)
    y = x * lax.rsqrt(jnp.mean(x * x, axis=-1, keepdims=True) + NORM_EPS)
    o_ref[...] = (y * g_ref[...]).astype(o_ref.dtype)


def rmsnorm(x, g, out_dtype, *, width=None, col_block=0, name="rmsnorm"):
    m = x.shape[0]
    width = x.shape[1] if width is None else width
    tm = _pick(m, (512, 256, 128, 64, 32, 16, 8))
    return pl.pallas_call(
        _rmsnorm_body,
        grid=(m // tm,),
        in_specs=[pl.BlockSpec((tm, width), lambda i: (i, col_block)),
                  pl.BlockSpec((1, width), lambda i: (0, 0))],
        out_specs=pl.BlockSpec((tm, width), lambda i: (i, 0)),
        out_shape=jax.ShapeDtypeStruct((m, width), out_dtype),
        compiler_params=_params("parallel"),
        name=name,
    )(x, g.reshape(1, width).astype(F32))


def _mm_body(*refs, n_extra, epilogue):
    a_ref, w_ref = refs[:2]
    extra = [r[...] for r in refs[2:2 + n_extra]]
    outs = refs[2 + n_extra:]
    acc = _dot(a_ref[...], w_ref[...])
    res = epilogue(acc, *extra)
    for o, r in zip(outs, res):
        o[...] = r.astype(o.dtype)


def matmul(a, w, out_dtypes, *, epilogue=None, extras=(), name="matmul"):
    m = a.shape[0]
    k, n = w.shape
    tm = _pick(m, (1024, 512, 256, 128, 64, 32, 16, 8))
    tn = n if n <= 1280 else _pick(n, (512, 256, 128))
    if epilogue is None:
        epilogue = lambda acc: (acc,)
    in_specs = [pl.BlockSpec((tm, k), lambda i, j: (i, 0)),
                pl.BlockSpec((k, tn), lambda i, j: (0, j))]
    args = [a, w]
    for arr, kind, off in extras:
        if kind == "row":
            in_specs.append(pl.BlockSpec((tm, arr.shape[1]), lambda i, j: (i, 0)))
        elif kind == "col":
            in_specs.append(pl.BlockSpec((1, tn), lambda i, j: (0, j)))
        else:
            in_specs.append(pl.BlockSpec((tm, tn), lambda i, j, off=off: (i, j + off)))
        args.append(arr)
    return pl.pallas_call(
        functools.partial(_mm_body, n_extra=len(extras), epilogue=epilogue),
        grid=(m // tm, n // tn),
        in_specs=in_specs,
        out_specs=[pl.BlockSpec((tm, tn), lambda i, j: (i, j)) for _ in out_dtypes],
        out_shape=[jax.ShapeDtypeStruct((m, n), dt) for dt in out_dtypes],
        compiler_params=_params("parallel", "parallel"),
        name=name,
    )(*args)


def _rope_slots(x, cos_t, sin_t):
    n = x.shape[-1]
    reps = n // LANES
    lane = lax.broadcasted_iota(jnp.int32, x.shape, 1) % LANES
    from_hi = pltpu.roll(x, n - MLA_ROPE_DIM // 2, 1)
    from_lo = pltpu.roll(x, MLA_ROPE_DIM // 2, 1)
    swapped = jnp.where(lane < HEAD_DIM + MLA_ROPE_DIM // 2, from_hi, from_lo)
    if reps > 1:
        cos_t = jnp.concatenate([cos_t] * reps, axis=1)
        sin_t = jnp.concatenate([sin_t] * reps, axis=1)
    return x * cos_t + swapped * sin_t


def _sb_body(q_ref, k_ref, v_ref, o_ref, *, t, cw):
    qi = pl.program_id(2)
    n_sub = t // cw
    first = lax.broadcasted_iota(jnp.int32, (cw, cw), 0)
    second = lax.broadcasted_iota(jnp.int32, (cw, cw), 1)
    before = second < first
    later = (first > second).astype(BF16)
    later2 = jnp.concatenate([later, later], axis=0)
    heads = [slice(hh * HEAD_DIM, (hh + 1) * HEAD_DIM) for hh in range(2)]

    def sweep(q, carry, ks, n_keys, diag_sub):
        z = [_dot_nt(q[hh], k_ref[0, pl.ds(ks, n_keys), h]) for hh, h in enumerate(heads)]
        acc = [carry[0], carry[2]]
        run = [carry[1], carry[3]]
        subs = n_keys // cw
        weights = [[None] * subs for _ in heads]
        for sub in reversed(range(subs)):
            for hh in range(2):
                zz = z[hh][:, sub * cw:(sub + 1) * cw]
                neg_abs = pltpu.bitcast(pltpu.bitcast(zz, jnp.uint32) | jnp.uint32(0x80000000), F32)
                sp = jnp.maximum(zz, 0.0) + jnp.log(1.0 + jnp.exp(neg_abs))
                mass = jnp.where(before, sp, 0.0) if sub == diag_sub else sp
                hi, lo = _split(mass)
                after = _dot(jnp.concatenate([hi, lo], axis=1), later2)
                wgt = jnp.exp(((zz - sp) - after) - run[hh])
                if sub == diag_sub:
                    wgt = jnp.where(before, wgt, 0.0)
                weights[hh][sub] = wgt.astype(BF16)
                run[hh] = run[hh] + (after[:, :1] + mass[:, :1])
        for hh, h in enumerate(heads):
            acc[hh] = acc[hh] + _dot(jnp.concatenate(weights[hh], axis=1), v_ref[0, pl.ds(ks, n_keys), h])
        return acc[0], run[0], acc[1], run[1]

    k0 = pl.multiple_of(qi * t, t)
    zeros = (jnp.zeros((cw, HEAD_DIM), F32), jnp.zeros((cw, 1), F32)) * 2
    groups = [sweep([q_ref[0, j * cw:(j + 1) * cw, h] for h in heads], zeros, k0, (j + 1) * cw, j)
              for j in range(n_sub)]
    carry = tuple(jnp.concatenate([g[c] for g in groups], axis=0) for c in range(4))
    q = [q_ref[0, :, h] for h in heads]
    carry = lax.fori_loop(
        0, qi, lambda it, c: sweep(q, c, pl.multiple_of((qi - 1 - it) * t, t), t, None), carry)
    o_ref[0] = jnp.concatenate([carry[0], carry[2]], axis=1).astype(o_ref.dtype)


def sb_attention(qkv, batch, seq):
    t = _pick(seq, (512, 256, 128))
    x = qkv.reshape(batch, seq, 3 * WIDTH)
    hp = WIDTH // LANES
    out = pl.pallas_call(
        functools.partial(_sb_body, t=t, cw=min(t, 256)),
        grid=(batch, hp, seq // t),
        in_specs=[pl.BlockSpec((1, t, LANES), lambda b, h, i: (b, i, h)),
                  pl.BlockSpec((1, seq, LANES), lambda b, h, i: (b, 0, hp + h)),
                  pl.BlockSpec((1, seq, LANES), lambda b, h, i: (b, 0, 2 * hp + h))],
        out_specs=pl.BlockSpec((1, t, LANES), lambda b, h, i: (b, i, h)),
        out_shape=jax.ShapeDtypeStruct((batch, seq, WIDTH), BF16),
        compiler_params=_params("parallel", "parallel", "parallel"),
        name="sb_attention",
    )(x, x, x)
    return out.reshape(batch * seq, WIDTH)


def _mla_body(q_ref, k_ref, v_ref, o_ref, *, t):
    qi = pl.program_id(2)
    rows = lax.broadcasted_iota(jnp.int32, (t, t), 0)
    cols = lax.broadcasted_iota(jnp.int32, (t, t), 1)
    q = [q_ref[0, :, hh * LANES:(hh + 1) * LANES] for hh in range(2)]

    def block(ks, carry, diagonal):
        s = [_dot_nt(q[hh], k_ref[0, pl.ds(ks, t), hh * LANES:(hh + 1) * LANES]) for hh in range(2)]
        out = []
        for hh in range(2):
            m, l, acc = carry[3 * hh:3 * hh + 3]
            ss = jnp.where(cols <= rows, s[hh], -1e30) if diagonal else s[hh]
            m_new = jnp.maximum(m, jnp.max(ss, axis=-1, keepdims=True))
            p = jnp.exp(ss - m_new)
            corr = jnp.exp(m - m_new)
            l = corr * l + jnp.sum(p, axis=-1, keepdims=True)
            v = v_ref[0, pl.ds(ks, t), hh * HEAD_DIM:(hh + 1) * HEAD_DIM]
            acc = corr * acc + _dot(p.astype(BF16), v)
            out += [m_new, l, acc]
        return tuple(out)

    init = (jnp.full((t, 1), -1e30, F32), jnp.zeros((t, 1), F32), jnp.zeros((t, HEAD_DIM), F32)) * 2
    carry = block(pl.multiple_of(qi * t, t), init, True)
    carry = lax.fori_loop(0, qi, lambda kb, c: block(pl.multiple_of(kb * t, t), c, False), carry)
    o_ref[0] = jnp.concatenate([carry[2] / carry[1], carry[5] / carry[4]], axis=1).astype(o_ref.dtype)


def mla_attention(q, k, v, batch, seq):
    t = _pick(seq, (512, 256, 128))
    hp = HEADS // 2
    out = pl.pallas_call(
        functools.partial(_mla_body, t=t),
        grid=(batch, hp, seq // t),
        in_specs=[pl.BlockSpec((1, t, 2 * LANES), lambda b, h, i: (b, i, h)),
                  pl.BlockSpec((1, seq, 2 * LANES), lambda b, h, i: (b, 0, h)),
                  pl.BlockSpec((1, seq, LANES), lambda b, h, i: (b, 0, h))],
        out_specs=pl.BlockSpec((1, t, LANES), lambda b, h, i: (b, i, h)),
        out_shape=jax.ShapeDtypeStruct((batch, seq, WIDTH), BF16),
        compiler_params=_params("parallel", "parallel", "parallel"),
        name="mla_attention",
    )(q.reshape(batch, seq, HEADS * LANES), k.reshape(batch, seq, HEADS * LANES),
      v.reshape(batch, seq, WIDTH))
    return out.reshape(batch * seq, WIDTH)


RWKV_IN = 3 * WIDTH + 2 * RWKV_LORA_PAD + RWKV_GATE_LORA


def _rwkv_prep_body(*refs, tiles_per_seq, has_vres):
    if has_vres:
        (cols_ref, prev_ref, mu_ref, w0_ref, w2_ref, a0_ref, a2_ref, g2_ref, kk_ref, ka_ref,
         v0_ref, v1_ref, v2_ref, vfirst_ref,
         r_out, lw_out, k_out, v_out, kk_out, a_out, g_out) = refs
    else:
        (cols_ref, prev_ref, mu_ref, w0_ref, w2_ref, a0_ref, a2_ref, g2_ref, kk_ref, ka_ref,
         r_out, lw_out, k_out, v_out, kk_out, a_out, g_out) = refs
    x = cols_ref[...]
    starts_seq = (pl.program_id(0) % tiles_per_seq) == 0
    last_prev = jnp.where(starts_seq, 0.0, prev_ref[7:8, :])
    first_row = lax.broadcasted_iota(jnp.int32, x.shape, 0) == 0
    prev = jnp.where(first_row, last_prev, pltpu.roll(x, 1, 0))
    x = x + (prev - x) * mu_ref[...]
    r = x[:, :WIDTH]
    k = x[:, WIDTH:2 * WIDTH]
    v = x[:, 2 * WIDTH:3 * WIDTH]
    o = 3 * WIDTH
    wd = x[:, o:o + RWKV_LORA_PAD]
    ad = x[:, o + RWKV_LORA_PAD:o + 2 * RWKV_LORA_PAD]
    gd = x[:, o + 2 * RWKV_LORA_PAD:]
    wpre = w0_ref[...] + _dot(jnp.tanh(wd).astype(BF16), w2_ref[...])
    w = -jax.nn.softplus(-wpre) - 0.5
    lw_out[...] = -jnp.exp(w)
    a = jax.nn.sigmoid(a0_ref[...] + _dot(ad.astype(BF16), a2_ref[...]))
    g_out[...] = _dot(jax.nn.sigmoid(gd).astype(BF16), g2_ref[...])
    if has_vres:
        low = _dot(v.astype(BF16), v1_ref[...])
        mix = jax.nn.sigmoid(v0_ref[...] + _dot(low.astype(BF16), v2_ref[...]))
        v = v + (vfirst_ref[...] - v) * mix
    r_out[...] = r
    v_out[...] = v
    kk_out[...] = k * kk_ref[...]
    k_out[...] = k * (1.0 + (a - 1.0) * ka_ref[...])
    a_out[...] = a


def rwkv_prep(cols, seq, mu, w0, w2, a0, a2, g2, k_k, k_a, vres):
    m = cols.shape[0]
    tm = _pick(seq, (256, 128, 64, 32, 16, 8))
    row = lambda t: t.reshape(1, -1).astype(F32)
    args = [cols, cols, row(mu), row(w0), w2, row(a0), a2, g2, row(k_k), row(k_a)]
    full = lambda arr: pl.BlockSpec(arr.shape, lambda i: (0, 0))
    tile = pl.BlockSpec((tm, WIDTH), lambda i: (i, 0))
    in_specs = [pl.BlockSpec((tm, RWKV_IN), lambda i: (i, 0)),
                pl.BlockSpec((8, RWKV_IN), lambda i: (jnp.maximum(i * (tm // 8) - 1, 0), 0))]
    in_specs += [full(t) for t in args[2:]]
    if vres is not None:
        v0, v1, v2, v_first = vres
        extra = [row(v0), v1, v2]
        args += extra + [v_first]
        in_specs += [full(t) for t in extra] + [tile]
    return pl.pallas_call(
        functools.partial(_rwkv_prep_body, tiles_per_seq=seq // tm, has_vres=vres is not None),
        grid=(m // tm,),
        in_specs=in_specs,
        out_specs=[tile] * 7,
        out_shape=[jax.ShapeDtypeStruct((m, WIDTH), F32)] * 7,
        compiler_params=_params("parallel"),
        name="rwkv_prep",
    )(*args)


def _halves(x):
    return x[:, :HEAD_DIM], x[:, HEAD_DIM:]


def _head_sum(x, left):
    s0 = jnp.sum(jnp.where(left, x, 0.0), axis=-1, keepdims=True)
    s1 = jnp.sum(jnp.where(left, 0.0, x), axis=-1, keepdims=True)
    return jnp.where(left, s0, s1)


def _dot3(x, y, dot):
    xh, xl = _split(x)
    yh, yl = _split(y)
    return dot(xh, yh) + (dot(xh, yl) + dot(xl, yh))


def _rwkv_chunk_terms(r, lw, k, v, kk_raw, a):
    c = r.shape[0]
    left = lax.broadcasted_iota(jnp.int32, (c, LANES), 1) < HEAD_DIM
    trow = lax.broadcasted_iota(jnp.int32, (c, LANES), 0)
    kk = kk_raw / jnp.maximum(jnp.sqrt(_head_sum(kk_raw * kk_raw, left)), 1e-12)
    beta = kk * a
    cl = lw
    shift = 1
    while shift < c:
        cl = cl + jnp.where(trow >= shift, pltpu.roll(cl, shift, 0), 0.0)
        shift *= 2
    a_t = -kk * jnp.exp(cl - lw)
    r_t = r * jnp.exp(cl)
    inv = jnp.exp(-cl)
    cl_end = cl[c - 1:c, :]
    tail = jnp.exp(cl_end - cl)
    full = (a_t, r_t, beta * inv, k * inv, v, beta * tail, k * tail, jnp.exp(cl_end))
    return [tuple(_halves(x)[hh] for x in full) for hh in range(2)]


def _rwkv_masks(c):
    ti = lax.broadcasted_iota(jnp.int32, (c, c), 0)
    tj = lax.broadcasted_iota(jnp.int32, (c, c), 1)
    merges = []
    half = 1
    while half < c:
        merges.append((ti // (2 * half) == tj // (2 * half)) & (ti % (2 * half) >= half)
                      & (tj % (2 * half) < half))
        half *= 2
    return ti > tj, ti >= tj, ti == tj, merges


def _rwkv_chain(out, head_terms, masks):
    at_h, rt_h, bt_h, kt_h, v_h, bbar_h, kbar_h, gend_h = head_terms
    strict, incl, eye, merges = masks
    c = at_h.shape[0]
    ar = jnp.concatenate([at_h, rt_h], axis=0).astype(BF16)
    bk = jnp.concatenate([bt_h, kt_h], axis=0).astype(BF16)
    pair = _dot_nt(ar, bk)
    yield
    lower = jnp.where(strict, pair[:c, :c], 0.0)
    akv = _dot(jnp.where(strict, pair[:c, c:], 0.0).astype(BF16), v_h.astype(BF16))
    tinv = eye.astype(F32) + jnp.where(merges[0], lower, 0.0)
    for off in merges[1:]:
        tb = tinv.astype(BF16)
        step = _dot(tb, jnp.where(off, lower, 0.0).astype(BF16))
        yield
        tinv = tinv + _dot(step.astype(BF16), tb)
        yield
    solved = _dot(tinv.astype(BF16), jnp.concatenate([at_h, akv], axis=1).astype(BF16))
    yield
    ta, w = _halves(solved)
    rb = jnp.where(incl, pair[c:, :c], 0.0).astype(BF16)
    rk = jnp.where(incl, pair[c:, c:], 0.0).astype(BF16)
    wv = jnp.concatenate([w, v_h], axis=0).astype(BF16)
    q = rt_h + _dot(rb, ta.astype(BF16))
    y0 = _dot(jnp.concatenate([rb, rk], axis=1), wv)
    g = jnp.where(eye, gend_h, 0.0) + _dot_tn(ta.astype(BF16), bbar_h.astype(BF16))
    d = _dot_tn(wv, jnp.concatenate([bbar_h, kbar_h], axis=0).astype(BF16))
    out.append((q, y0, g, d))


def _rwkv_state_chain(hh, results, state_ref, ys):
    state = state_ref[hh]
    for ci, res in enumerate(results):
        while not res[hh]:
            yield
        q, y0, g, d = res[hh][0]
        ys[ci][hh] = _dot3(q, state, _dot_nt) + y0
        state = _dot3(state, g, _dot) + d
        yield
    state_ref[hh] = state


def _lockstep(chains):
    live = list(chains)
    while live:
        still = []
        for ch in live:
            try:
                next(ch)
                still.append(ch)
            except StopIteration:
                pass
        live = still


def _rwkv_scan_body(r_ref, lw_ref, k_ref, v_ref, kk_ref, a_ref, g_ref, lng_ref, lnb_ref, rk_ref,
                    o_ref, state_ref, *, chunk):
    @pl.when(pl.program_id(2) == 0)
    def _():
        state_ref[...] = jnp.zeros_like(state_ref)

    n_chunks = r_ref.shape[1] // chunk
    left = lax.broadcasted_iota(jnp.int32, (chunk, LANES), 1) < HEAD_DIM
    masks = _rwkv_masks(chunk)
    rows = [slice(ci * chunk, (ci + 1) * chunk) for ci in range(n_chunks)]
    results = [[[], []] for _ in range(n_chunks)]
    chains = []
    for ci in range(n_chunks):
        terms = _rwkv_chunk_terms(r_ref[0, rows[ci], :], lw_ref[0, rows[ci], :], k_ref[0, rows[ci], :],
                                  v_ref[0, rows[ci], :], kk_ref[0, rows[ci], :], a_ref[0, rows[ci], :])
        chains += [_rwkv_chain(results[ci][hh], terms[hh], masks) for hh in range(2)]
    _lockstep(chains)
    ys = [[None, None] for _ in range(n_chunks)]
    _lockstep([_rwkv_state_chain(hh, results, state_ref, ys) for hh in range(2)])

    for ci in range(n_chunks):
        y = jnp.concatenate(ys[ci], axis=1)
        r, k, v = r_ref[0, rows[ci], :], k_ref[0, rows[ci], :], v_ref[0, rows[ci], :]
        mean = _head_sum(y, left) * (1.0 / HEAD_DIM)
        var = _head_sum(jnp.square(y - mean), left) * (1.0 / HEAD_DIM)
        y = (y - mean) * lax.rsqrt(var + RWKV_GN_EPS) * lng_ref[...] + lnb_ref[...]
        bonus = _head_sum(r * k * rk_ref[...], left) * v
        o_ref[0, rows[ci], :] = ((y + bonus) * g_ref[0, rows[ci], :]).astype(o_ref.dtype)


def rwkv_scan(r, lw, k, v, kk, a, g, ln_g, ln_b, r_k, batch, seq):
    chunk = min(RWKV_CHUNK, seq)
    rows = _pick(seq, (512, 256, 128, 64, 32, 16, 8))
    hp = WIDTH // LANES
    seq_spec = pl.BlockSpec((1, rows, LANES), lambda b, h, c: (b, c, h))
    par_spec = pl.BlockSpec((1, LANES), lambda b, h, c: (0, h))
    shaped = [t.reshape(batch, seq, WIDTH) for t in (r, lw, k, v, kk, a, g)]
    pars = [t.reshape(1, WIDTH).astype(F32) for t in (ln_g, ln_b, r_k)]
    out = pl.pallas_call(
        functools.partial(_rwkv_scan_body, chunk=chunk),
        grid=(batch, hp, seq // rows),
        in_specs=[seq_spec] * 7 + [par_spec] * 3,
        out_specs=seq_spec,
        out_shape=jax.ShapeDtypeStruct((batch, seq, WIDTH), BF16),
        scratch_shapes=[pltpu.VMEM((2, HEAD_DIM, HEAD_DIM), F32)],
        compiler_params=_params("parallel", "parallel", "arbitrary"),
        name="rwkv_scan",
    )(*shaped, *pars)
    return out.reshape(batch * seq, WIDTH)


def _merge_body(h, a1, a2, a3, g1, g2, g3, w1, w2, w3, o_ref):
    hh = h[...]
    acc = jax.nn.sigmoid(_dot(hh, g1[...])) * _dot(a1[...], w1[...])
    acc += jax.nn.sigmoid(_dot(hh, g2[...])) * _dot(a2[...], w2[...])
    acc += jax.nn.sigmoid(_dot(hh, g3[...])) * _dot(a3[...], w3[...])
    o_ref[...] = acc.astype(o_ref.dtype)


def merge_branches(h, outs, w_gates, weights):
    m, d = h.shape
    tm = _pick(m, (512, 256, 128, 64, 32, 16, 8))
    tn = _pick(d, (512, 256, 128))
    nb = d // tn
    h_spec = pl.BlockSpec((tm, d), lambda i, j: (i, 0))
    a_spec = pl.BlockSpec((tm, WIDTH), lambda i, j: (i, 0))
    w_spec = pl.BlockSpec((WIDTH, tn), lambda i, j: (0, j))
    g_specs = [pl.BlockSpec((d, tn), lambda i, j, b=b: (0, j + b * nb)) for b in range(3)]
    return pl.pallas_call(
        _merge_body,
        grid=(m // tm, nb),
        in_specs=[h_spec] + [a_spec] * 3 + g_specs + [w_spec] * 3,
        out_specs=pl.BlockSpec((tm, tn), lambda i, j: (i, j)),
        out_shape=jax.ShapeDtypeStruct((m, d), BF16),
        compiler_params=_params("parallel", "parallel"),
        name="merge_branches",
    )(h, *outs, w_gates, w_gates, w_gates, *weights)


ROUTER_PAD = 128


def _slab_cols(ref, n, slabs):
    return jnp.concatenate([ref[pl.ds(s, n, stride=slabs), :] for s in range(slabs)], axis=1)


def _store_slabs(ref, value, slabs):
    n = value.shape[0]
    for s in range(slabs):
        ref[pl.ds(s, n, stride=slabs), :] = value[:, s * LANES:(s + 1) * LANES].astype(ref.dtype)


def _pack_bf16_pairs(h):
    half = h.shape[1] // 2
    bits = pltpu.bitcast(h.astype(BF16).astype(F32), jnp.uint32)
    return (bits[:, :half] >> 16) | bits[:, half:]


def _unpack_bf16_pairs(words):
    lo = pltpu.bitcast(words << 16, F32)
    hi = pltpu.bitcast(words & jnp.uint32(0xFFFF0000), F32)
    return jnp.concatenate([lo, hi], axis=1).astype(BF16)


def _router_body(x_ref, g_ref, w_ref, b_ref, h_ref, logit_ref, *, slabs):
    x = x_ref[...]
    y = x * lax.rsqrt(jnp.mean(x * x, axis=-1, keepdims=True) + NORM_EPS)
    h = y * g_ref[...]
    _store_slabs(h_ref, _pack_bf16_pairs(h), slabs)
    logit_ref[...] = jnp.dot(h, w_ref[...], preferred_element_type=F32, precision=HIGHEST) + b_ref[...]


def router(x, g, w_pad, b_pad):
    m, d = x.shape
    slabs = d // (2 * LANES)
    tm = _pick(m, (256, 128, 64, 32, 16, 8))
    return pl.pallas_call(
        functools.partial(_router_body, slabs=slabs),
        grid=(m // tm,),
        in_specs=[pl.BlockSpec((tm, d), lambda i: (i, 0)),
                  pl.BlockSpec((1, d), lambda i: (0, 0)),
                  pl.BlockSpec((d, ROUTER_PAD), lambda i: (0, 0)),
                  pl.BlockSpec((1, ROUTER_PAD), lambda i: (0, 0))],
        out_specs=[pl.BlockSpec((tm * slabs, LANES), lambda i: (i, 0)),
                   pl.BlockSpec((tm, ROUTER_PAD), lambda i: (i, 0))],
        out_shape=[jax.ShapeDtypeStruct((m * slabs, LANES), jnp.uint32),
                   jax.ShapeDtypeStruct((m, ROUTER_PAD), F32)],
        compiler_params=_params("parallel"),
        name="ffn_norm_router",
    )(x, g.reshape(1, d).astype(F32), w_pad, b_pad)


def _row_gather(idx_ref, first, n, src_ref, buf_ref, sem, slabs):
    def copy(r):
        src = pl.multiple_of(idx_ref[first + r] * slabs, slabs)
        return pltpu.make_async_copy(src_ref.at[pl.ds(src, slabs)],
                                     buf_ref.at[pl.ds(r * slabs, slabs)], sem)

    def start():
        lax.fori_loop(0, n, lambda r, c: (copy(r).start(), c)[1], 0, unroll=8)

    def wait():
        pltpu.make_async_copy(src_ref.at[pl.ds(0, n * slabs)], buf_ref, sem).wait()

    return start, wait


def _expert_body(be_ref, nb_ref, next_ref, tok_ref, h_ref, wg_hbm, wu_hbm, wd_hbm, o_ref,
                 xbuf, wg_f, wu_f, wd_f, wg_s, wu_s, wd_s, sem, wsem, *, layer, in_slabs, slabs):
    i = pl.program_id(0)
    n_blocks = nb_ref[0]
    slot = i % 2

    def gather(block, slot):
        return _row_gather(tok_ref, block * MOE_ROWS, MOE_ROWS, h_ref, xbuf.at[slot], sem.at[slot], in_slabs)

    def weight_copies(expert):
        pairs = ((wg_hbm, wg_f), (wu_hbm, wu_f), (wd_hbm, wd_f))
        return [pltpu.make_async_copy(src.at[layer, expert], dst, wsem.at[j])
                for j, (src, dst) in enumerate(pairs)]

    e = be_ref[i]

    @pl.when(i == 0)
    def _():
        gather(0, 0)[0]()
        for c in weight_copies(e):
            c.start()

    @pl.when(i + 1 < n_blocks)
    def _():
        gather(i + 1, 1 - slot)[0]()

    @pl.when((i < n_blocks) & ((i == 0) | (e != be_ref[jnp.maximum(i - 1, 0)])))
    def _():
        for c in weight_copies(e):
            c.wait()
        wg_s[...] = wg_f[...].astype(BF16)
        wu_s[...] = wu_f[...].astype(BF16)
        wd_s[...] = wd_f[...].astype(BF16)
        upcoming = next_ref[i]

        @pl.when(upcoming >= 0)
        def _():
            for c in weight_copies(upcoming):
                c.start()

    @pl.when(i < n_blocks)
    def _():
        gather(i, slot)[1]()
        x = _unpack_bf16_pairs(_slab_cols(xbuf.at[slot], MOE_ROWS, in_slabs))
        gate = _dot(x, wg_s[...])
        up = _dot(x, wu_s[...])
        mid = (jax.nn.silu(gate) * up).astype(BF16)
        _store_slabs(o_ref, _dot(mid, wd_s[...]), slabs)

    @pl.when(i >= n_blocks)
    def _():
        o_ref[...] = jnp.zeros_like(o_ref)


def expert_ffn(h_rows, row_tok, block_e, next_e, n_blocks, layer, w_gate, w_up, w_down):
    _, n_exp, d, ff = w_gate.shape
    slabs = d // LANES
    in_slabs = d // (2 * LANES)
    p = row_tok.shape[0]
    hbm = pl.BlockSpec(memory_space=pl.ANY)
    return pl.pallas_call(
        functools.partial(_expert_body, layer=layer, in_slabs=in_slabs, slabs=slabs),
        grid_spec=pltpu.PrefetchScalarGridSpec(
            num_scalar_prefetch=4,
            grid=(p // MOE_ROWS,),
            in_specs=[hbm, hbm, hbm, hbm],
            out_specs=pl.BlockSpec((MOE_ROWS * slabs, LANES), lambda i, be, nb, nxt, tok: (i, 0)),
            scratch_shapes=[pltpu.VMEM((2, MOE_ROWS * in_slabs, LANES), jnp.uint32),
                            pltpu.VMEM((d, ff), F32), pltpu.VMEM((d, ff), F32), pltpu.VMEM((ff, d), F32),
                            pltpu.VMEM((d, ff), BF16), pltpu.VMEM((d, ff), BF16),
                            pltpu.VMEM((ff, d), BF16),
                            pltpu.SemaphoreType.DMA((2,)), pltpu.SemaphoreType.DMA((3,))]),
        out_shape=jax.ShapeDtypeStruct((p * slabs, LANES), F32),
        compiler_params=_params("arbitrary"),
        name="expert_ffn",
    )(block_e, n_blocks, next_e, row_tok, h_rows, w_gate, w_up, w_down)


def _combine_body(dest_ref, x_ref, gate_ref, y_ref, o_ref, ybuf, sem, *, tm, slabs):
    i = pl.program_id(0)
    slot = i % 2

    def gather(step, slot):
        return _row_gather(dest_ref, step * tm * TOP_K, tm * TOP_K, y_ref, ybuf.at[slot], sem.at[slot], slabs)

    @pl.when(i == 0)
    def _():
        gather(0, 0)[0]()

    @pl.when(i + 1 < pl.num_programs(0))
    def _():
        gather(i + 1, 1 - slot)[0]()

    gather(i, slot)[1]()
    buf = ybuf.at[slot]
    gate = gate_ref[...]
    g0, g1 = gate[:, 0:1], gate[:, 1:2]
    for s in range(slabs):
        y0 = buf[pl.ds(s, tm, stride=slabs), :]
        y1 = buf[pl.ds(tm * slabs + s, tm, stride=slabs), :]
        cols = slice(s * LANES, (s + 1) * LANES)
        o_ref[:, cols] = x_ref[:, cols] + (g0 * y0 + g1 * y1)


def moe_combine(x, y_rows, dest, gate):
    m, d = x.shape
    slabs = d // LANES
    tm = _pick(m, (256, 128, 64, 32, 16, 8))
    tile = lambda i, dest: (i, 0)
    return pl.pallas_call(
        functools.partial(_combine_body, tm=tm, slabs=slabs),
        grid_spec=pltpu.PrefetchScalarGridSpec(
            num_scalar_prefetch=1,
            grid=(m // tm,),
            in_specs=[pl.BlockSpec((tm, d), tile), pl.BlockSpec((tm, TOP_K), tile),
                      pl.BlockSpec(memory_space=pl.ANY)],
            out_specs=pl.BlockSpec((tm, d), tile),
            scratch_shapes=[pltpu.VMEM((2, tm * TOP_K * slabs, LANES), F32),
                            pltpu.SemaphoreType.DMA((2,))]),
        out_shape=jax.ShapeDtypeStruct((m, d), F32),
        compiler_params=_params("arbitrary"),
        name="moe_combine",
    )(dest.reshape(m // tm, tm, TOP_K).transpose(0, 2, 1).reshape(m * TOP_K), x, gate, y_rows)


def _routing_tables(logits):
    t = logits.shape[0]
    n_assign = t * TOP_K
    tok = jnp.arange(t)
    g_logits = logits[:, :N_GROUPS]
    grp = jnp.argmax(g_logits, axis=-1)
    p_grp = jax.nn.softmax(g_logits, axis=-1)[tok, grp]
    e_logits = logits[:, N_GROUPS:N_GROUPS + N_EXPERTS].reshape(t, N_GROUPS, EXPERTS_PER_GROUP)
    p_in, idx_in = lax.top_k(jax.nn.softmax(e_logits[tok, grp], axis=-1), TOP_K)
    gate = p_grp[:, None] * p_in / jnp.sum(p_in, axis=-1, keepdims=True)
    flat_e = (grp[:, None] * EXPERTS_PER_GROUP + idx_in).reshape(n_assign).astype(jnp.int32)
    onehot = (flat_e[:, None] == jnp.arange(N_EXPERTS, dtype=jnp.int32)[None, :]).astype(jnp.int32)
    rank = jnp.sum((jnp.cumsum(onehot, axis=0) - onehot) * onehot, axis=1)
    counts = jnp.sum(onehot, axis=0)
    padded = (counts + MOE_ROWS - 1) // MOE_ROWS * MOE_ROWS
    pend = jnp.cumsum(padded)
    dest = ((pend - padded)[flat_e] + rank).astype(jnp.int32)
    n_blocks_max = -(-n_assign // MOE_ROWS) + N_EXPERTS
    row_tok = jnp.zeros((n_blocks_max * MOE_ROWS,), jnp.int32).at[dest].set(
        jnp.repeat(tok, TOP_K).astype(jnp.int32))
    block_e = jnp.minimum(
        jnp.searchsorted(pend, jnp.arange(n_blocks_max) * MOE_ROWS, side="right"),
        N_EXPERTS - 1).astype(jnp.int32)
    n_blocks = (pend[-1:] // MOE_ROWS).astype(jnp.int32)
    after = jnp.searchsorted(block_e, block_e, side="right")
    next_e = jnp.where(after < n_blocks[0], block_e[jnp.minimum(after, n_blocks_max - 1)], -1)
    return gate.astype(F32), dest.reshape(t, TOP_K), row_tok, block_e, next_e.astype(jnp.int32), n_blocks


def hier_moe_residual(x, norm_g, rg_w, rg_b, re_w, re_b, layer, w_gate, w_up, w_down):
    t, d = x.shape
    w_pad = jnp.zeros((d, ROUTER_PAD), F32).at[:, :N_GROUPS].set(rg_w)
    w_pad = w_pad.at[:, N_GROUPS:N_GROUPS + N_EXPERTS].set(re_w)
    b_pad = jnp.zeros((1, ROUTER_PAD), F32).at[0, :N_GROUPS].set(rg_b)
    b_pad = b_pad.at[0, N_GROUPS:N_GROUPS + N_EXPERTS].set(re_b)
    h_rows, logits = router(x, norm_g, w_pad, b_pad)
    gate, dest, row_tok, block_e, next_e, n_blocks = _routing_tables(logits)
    y_rows = expert_ffn(h_rows, row_tok, block_e, next_e, n_blocks, layer, w_gate, w_up, w_down)
    return moe_combine(x, y_rows, dest, gate)


def _mla_q_weight(w_uq):
    w = w_uq.reshape(MLA_Q_LORA, HEADS, HEAD_DIM + MLA_ROPE_DIM)
    w = jnp.pad(w, ((0, 0), (0, 0), (0, LANES - HEAD_DIM - MLA_ROPE_DIM)))
    return w.reshape(MLA_Q_LORA, HEADS * LANES).astype(BF16)


def _mla_kv_weights(w_ukv):
    w = w_ukv.reshape(MLA_KV_LORA, HEADS, 2 * HEAD_DIM)
    wk = jnp.pad(w[:, :, :HEAD_DIM], ((0, 0), (0, 0), (0, LANES - HEAD_DIM)))
    wv = w[:, :, HEAD_DIM:]
    return (wk.reshape(MLA_KV_LORA, HEADS * LANES).astype(BF16),
            wv.reshape(MLA_KV_LORA, WIDTH).astype(BF16))


def _rope_slot_tables(positions):
    half = MLA_ROPE_DIM // 2
    inv_freq = ROPE_THETA ** (-jnp.arange(0, MLA_ROPE_DIM, 2, dtype=F32) / MLA_ROPE_DIM)
    ang = positions.astype(F32).reshape(-1, 1) * inv_freq[None, :]
    cos, sin = jnp.cos(ang), jnp.sin(ang)
    t = ang.shape[0]
    pad = jnp.zeros((t, LANES - HEAD_DIM - 2 * half), F32)
    cos_t = jnp.concatenate([jnp.ones((t, HEAD_DIM), F32), cos, cos, pad], axis=1)
    sin_t = jnp.concatenate([jnp.zeros((t, HEAD_DIM), F32), -sin, sin, pad], axis=1)
    return cos_t, sin_t


def _mixer_block(x, l, batch, seq, cos_t, sin_t, v_first, p):
    t, d = x.shape
    h = rmsnorm(x, p["attn_norm_g"][l], BF16, name="attn_norm")
    w_in = p["w_in"][l]
    o_rwkv = 3 * WIDTH
    n_rwkv = 3 * WIDTH + RWKV_DECAY_LORA + RWKV_ICLR_LORA + RWKV_GATE_LORA
    o_mla = o_rwkv + n_rwkv
    n_mla = MLA_Q_LORA + MLA_KV_LORA + MLA_ROPE_DIM
    o_gate = o_mla + n_mla

    qscale = jnp.concatenate([jnp.full((1, WIDTH), HEAD_DIM ** -0.5, F32), jnp.ones((1, 2 * WIDTH), F32)], axis=1)
    (qkv,) = matmul(h, w_in[:, :o_rwkv].astype(BF16), [BF16],
                    epilogue=lambda acc, s: (acc * s,), extras=[(qscale, "col", 0)], name="proj_sb")
    o_sb = sb_attention(qkv, batch, seq)

    wr = w_in[:, o_rwkv:o_mla]
    zpad = jnp.zeros((d, RWKV_LORA_PAD - RWKV_DECAY_LORA), F32)
    c0 = 3 * WIDTH
    w_rwkv = jnp.concatenate([wr[:, :c0], wr[:, c0:c0 + RWKV_DECAY_LORA], zpad,
                              wr[:, c0 + RWKV_DECAY_LORA:c0 + 2 * RWKV_DECAY_LORA], zpad,
                              wr[:, c0 + 2 * RWKV_DECAY_LORA:]], axis=1).astype(BF16)
    (cols,) = matmul(h, w_rwkv, [F32], name="proj_rwkv")
    mu = p["rwkv_mu"][l]
    mpad = jnp.zeros((RWKV_LORA_PAD - RWKV_DECAY_LORA,), F32)
    mu_pad = jnp.concatenate([mu[:c0], mu[c0:c0 + RWKV_DECAY_LORA], mpad,
                              mu[c0 + RWKV_DECAY_LORA:c0 + 2 * RWKV_DECAY_LORA], mpad,
                              mu[c0 + 2 * RWKV_DECAY_LORA:]])
    rpad = ((0, RWKV_LORA_PAD - RWKV_DECAY_LORA), (0, 0))
    vres = None
    if l > 0:
        vres = (p["rwkv_v0"][l - 1], p["rwkv_v1"][l - 1].astype(BF16), p["rwkv_v2"][l - 1].astype(BF16), v_first)
    r, lw, k, v, kk, a, g = rwkv_prep(
        cols, seq, mu_pad, p["rwkv_w0"][l], jnp.pad(p["rwkv_w2"][l], rpad).astype(BF16),
        p["rwkv_a0"][l], jnp.pad(p["rwkv_a2"][l], rpad).astype(BF16), p["rwkv_g2"][l].astype(BF16),
        p["rwkv_k_k"][l], p["rwkv_k_a"][l], vres)
    if l == 0:
        v_first = v
    o_rw = rwkv_scan(r, lw, k, v, kk, a, g, p["rwkv_ln_g"][l], p["rwkv_ln_b"][l], p["rwkv_r_k"][l],
                     batch, seq)

    wm = w_in[:, o_mla:o_gate]
    w_mla = jnp.concatenate([wm[:, :MLA_Q_LORA + MLA_KV_LORA], jnp.zeros((d, HEAD_DIM), F32),
                             wm[:, MLA_Q_LORA + MLA_KV_LORA:],
                             jnp.zeros((d, LANES - HEAD_DIM - MLA_ROPE_DIM), F32)], axis=1).astype(BF16)
    (mcols,) = matmul(h, w_mla, [F32], name="proj_mla")
    cq = rmsnorm(mcols, p["mla_q_norm_g"][l], BF16, width=MLA_Q_LORA, col_block=0, name="mla_q_norm")
    ckv = rmsnorm(mcols, p["mla_kv_norm_g"][l], BF16, width=MLA_KV_LORA,
                  col_block=MLA_Q_LORA // MLA_KV_LORA, name="mla_kv_norm")
    mla_scale = (HEAD_DIM + MLA_ROPE_DIM) ** -0.5
    (q_mla,) = matmul(cq, _mla_q_weight(p["mla_w_uq"][l]), [BF16],
                      epilogue=lambda acc, c, s: (_rope_slots(acc, c, s) * mla_scale,),
                      extras=[(cos_t, "row", 0), (sin_t, "row", 0)], name="mla_q_up")
    wk, wv = _mla_kv_weights(p["mla_w_ukv"][l])
    kpe_block = (MLA_Q_LORA + MLA_KV_LORA) // LANES

    def k_epilogue(acc, kpe, c, s):
        kr = _rope_slots(kpe, c, s)
        return (acc + jnp.concatenate([kr] * (acc.shape[1] // LANES), axis=1),)

    kpe = mcols[:, kpe_block * LANES:(kpe_block + 1) * LANES]
    (k_mla,) = matmul(ckv, wk, [BF16], epilogue=k_epilogue,
                      extras=[(kpe, "row", 0), (cos_t, "row", 0), (sin_t, "row", 0)], name="mla_k_up")
    (v_mla,) = matmul(ckv, wv, [BF16], name="mla_v_up")
    o_mla_out = mla_attention(q_mla, k_mla, v_mla, batch, seq)

    merged = merge_branches(h, [o_sb, o_rw, o_mla_out], w_in[:, o_gate:].astype(BF16),
                            [p["w_br_sb"][l].astype(BF16), p["w_br_rwkv"][l].astype(BF16),
                             p["w_br_mla"][l].astype(BF16)])
    (x,) = matmul(merged, p["w_out"][l].astype(BF16), [F32],
                  epilogue=lambda acc, res: (res + acc,), extras=[(x, "tile", 0)], name="out_proj")
    return x, v_first


def kernel(x, positions, attn_norm_g, w_in, rwkv_mu, rwkv_w0, rwkv_w2, rwkv_a0, rwkv_a2, rwkv_g2, rwkv_k_k, rwkv_k_a, rwkv_r_k, rwkv_ln_g, rwkv_ln_b, rwkv_v0, rwkv_v1, rwkv_v2, mla_q_norm_g, mla_w_uq, mla_kv_norm_g, mla_w_ukv, w_br_sb, w_br_rwkv, w_br_mla, w_out, ffn_norm_g, router_group_w, router_group_b, router_expert_w, router_expert_b, expert_w_gate, expert_w_up, expert_w_down, final_norm_g):
    p = dict(attn_norm_g=attn_norm_g, w_in=w_in, rwkv_mu=rwkv_mu, rwkv_w0=rwkv_w0, rwkv_w2=rwkv_w2,
             rwkv_a0=rwkv_a0, rwkv_a2=rwkv_a2, rwkv_g2=rwkv_g2, rwkv_k_k=rwkv_k_k, rwkv_k_a=rwkv_k_a,
             rwkv_r_k=rwkv_r_k, rwkv_ln_g=rwkv_ln_g, rwkv_ln_b=rwkv_ln_b, rwkv_v0=rwkv_v0,
             rwkv_v1=rwkv_v1, rwkv_v2=rwkv_v2, mla_q_norm_g=mla_q_norm_g, mla_w_uq=mla_w_uq,
             mla_kv_norm_g=mla_kv_norm_g, mla_w_ukv=mla_w_ukv, w_br_sb=w_br_sb, w_br_rwkv=w_br_rwkv,
             w_br_mla=w_br_mla, w_out=w_out)
    batch, seq, d = x.shape
    depth = w_in.shape[0]
    cos_t, sin_t = _rope_slot_tables(positions)
    xt = x.reshape(batch * seq, d)
    v_first = None
    for l in range(depth):
        xt, v_first = _mixer_block(xt, l, batch, seq, cos_t, sin_t, v_first, p)
        xt = hier_moe_residual(xt, ffn_norm_g[l], router_group_w[l], router_group_b[l],
                               router_expert_w[l], router_expert_b[l],
                               l, expert_w_gate, expert_w_up, expert_w_down)
    return rmsnorm(xt, final_norm_g, x.dtype, name="final_norm").reshape(batch, seq, d)
```

```python
import functools

import jax
import jax.numpy as jnp
from jax import lax
from jax.experimental import pallas as pl
from jax.experimental.pallas import tpu as pltpu

F32 = jnp.float32
BF16 = jnp.bfloat16
HIGHEST = lax.Precision.HIGHEST

NORM_EPS = 1e-6
HEADS = 16
HEAD_DIM = 64
WIDTH = HEADS * HEAD_DIM
LANES = 128
RWKV_DECAY_LORA = 96
RWKV_ICLR_LORA = 96
RWKV_GATE_LORA = 256
RWKV_LORA_PAD = 128
RWKV_GN_EPS = 64e-5
RWKV_CHUNK = 64
MLA_Q_LORA = 768
MLA_KV_LORA = 256
MLA_ROPE_DIM = 32
ROPE_THETA = 10000.0
N_GROUPS = 8
EXPERTS_PER_GROUP = 8
N_EXPERTS = N_GROUPS * EXPERTS_PER_GROUP
TOP_K = 2
MOE_ROWS = 128
VMEM_LIMIT = 48 * 1024 * 1024


def _params(*sem):
    return pltpu.CompilerParams(dimension_semantics=sem, vmem_limit_bytes=VMEM_LIMIT)


def _pick(n, cands):
    for c in cands:
        if n % c == 0:
            return c
    raise ValueError(f"no tile for {n} in {cands}")


def _dot(x, y):
    return jnp.dot(x, y, preferred_element_type=F32)


def _dot_nt(x, y):
    return lax.dot_general(x, y, (((1,), (1,)), ((), ())), preferred_element_type=F32)


def _dot_tn(x, y):
    return lax.dot_general(x, y, (((0,), (0,)), ((), ())), preferred_element_type=F32)


def _split(x):
    hi = x.astype(BF16)
    return hi, (x - hi.astype(F32)).astype(BF16)


def _rmsnorm_body(x_ref, g_ref, o_ref):
    x = x_ref[...].astype(F32)
    y = x * lax.rsqrt(jnp.mean(x * x, axis=-1, keepdims=True) + NORM_EPS)
    o_ref[...] = (y * g_ref[...]).astype(o_ref.dtype)


def rmsnorm(x, g, out_dtype, *, width=None, col_block=0, name="rmsnorm"):
    m = x.shape[0]
    width = x.shape[1] if width is None else width
    tm = _pick(m, (512, 256, 128, 64, 32, 16, 8))
    return pl.pallas_call(
        _rmsnorm_body,
        grid=(m // tm,),
        in_specs=[pl.BlockSpec((tm, width), lambda i: (i, col_block)),
                  pl.BlockSpec((1, width), lambda i: (0, 0))],
        out_specs=pl.BlockSpec((tm, width), lambda i: (i, 0)),
        out_shape=jax.ShapeDtypeStruct((m, width), out_dtype),
        compiler_params=_params("parallel"),
        name=name,
    )(x, g.reshape(1, width).astype(F32))


def _mm_body(*refs, n_extra, epilogue):
    a_ref, w_ref = refs[:2]
    extra = [r[...] for r in refs[2:2 + n_extra]]
    outs = refs[2 + n_extra:]
    acc = _dot(a_ref[...], w_ref[...])
    res = epilogue(acc, *extra)
    for o, r in zip(outs, res):
        o[...] = r.astype(o.dtype)


def matmul(a, w, out_dtypes, *, epilogue=None, extras=(), name="matmul"):
    m = a.shape[0]
    k, n = w.shape
    tm = _pick(m, (1024, 512, 256, 128, 64, 32, 16, 8))
    tn = n if n <= 1280 else _pick(n, (512, 256, 128))
    if epilogue is None:
        epilogue = lambda acc: (acc,)
    in_specs = [pl.BlockSpec((tm, k), lambda i, j: (i, 0)),
                pl.BlockSpec((k, tn), lambda i, j: (0, j))]
    args = [a, w]
    for arr, kind, off in extras:
        if kind == "row":
            in_specs.append(pl.BlockSpec((tm, arr.shape[1]), lambda i, j: (i, 0)))
        elif kind == "col":
            in_specs.append(pl.BlockSpec((1, tn), lambda i, j: (0, j)))
        else:
            in_specs.append(pl.BlockSpec((tm, tn), lambda i, j, off=off: (i, j + off)))
        args.append(arr)
    return pl.pallas_call(
        functools.partial(_mm_body, n_extra=len(extras), epilogue=epilogue),
        grid=(m // tm, n // tn),
        in_specs=in_specs,
        out_specs=[pl.BlockSpec((tm, tn), lambda i, j: (i, j)) for _ in out_dtypes],
        out_shape=[jax.ShapeDtypeStruct((m, n), dt) for dt in out_dtypes],
        compiler_params=_params("parallel", "parallel"),
        name=name,
    )(*args)


def _rope_slots(x, cos_t, sin_t):
    n = x.shape[-1]
    reps = n // LANES
    lane = lax.broadcasted_iota(jnp.int32, x.shape, 1) % LANES
    from_hi = pltpu.roll(x, n - MLA_ROPE_DIM // 2, 1)
    from_lo = pltpu.roll(x, MLA_ROPE_DIM // 2, 1)
    swapped = jnp.where(lane < HEAD_DIM + MLA_ROPE_DIM // 2, from_hi, from_lo)
    if reps > 1:
        cos_t = jnp.concatenate([cos_t] * reps, axis=1)
        sin_t = jnp.concatenate([sin_t] * reps, axis=1)
    return x * cos_t + swapped * sin_t


def _sb_body(q_ref, k_ref, v_ref, o_ref, *, t, cw):
    qi = pl.program_id(2)
    n_sub = t // cw
    first = lax.broadcasted_iota(jnp.int32, (cw, cw), 0)
    second = lax.broadcasted_iota(jnp.int32, (cw, cw), 1)
    before = second < first
    later = (first > second).astype(BF16)
    later2 = jnp.concatenate([later, later], axis=0)
    heads = [slice(hh * HEAD_DIM, (hh + 1) * HEAD_DIM) for hh in range(2)]

    def sweep(q, carry, ks, n_keys, diag_sub):
        z = [_dot_nt(q[hh], k_ref[0, pl.ds(ks, n_keys), h]) for hh, h in enumerate(heads)]
        acc = [carry[0], carry[2]]
        run = [carry[1], carry[3]]
        subs = n_keys // cw
        weights = [[None] * subs for _ in heads]
        for sub in reversed(range(subs)):
            for hh in range(2):
                zz = z[hh][:, sub * cw:(sub + 1) * cw]
                neg_abs = pltpu.bitcast(pltpu.bitcast(zz, jnp.uint32) | jnp.uint32(0x80000000), F32)
                sp = jnp.maximum(zz, 0.0) + jnp.log(1.0 + jnp.exp(neg_abs))
                mass = jnp.where(before, sp, 0.0) if sub == diag_sub else sp
                hi, lo = _split(mass)
                after = _dot(jnp.concatenate([hi, lo], axis=1), later2)
                wgt = jnp.exp(((zz - sp) - after) - run[hh])
                if sub == diag_sub:
                    wgt = jnp.where(before, wgt, 0.0)
                weights[hh][sub] = wgt.astype(BF16)
                run[hh] = run[hh] + (after[:, :1] + mass[:, :1])
        for hh, h in enumerate(heads):
            acc[hh] = acc[hh] + _dot(jnp.concatenate(weights[hh], axis=1), v_ref[0, pl.ds(ks, n_keys), h])
        return acc[0], run[0], acc[1], run[1]

    k0 = pl.multiple_of(qi * t, t)
    zeros = (jnp.zeros((cw, HEAD_DIM), F32), jnp.zeros((cw, 1), F32)) * 2
    groups = [sweep([q_ref[0, j * cw:(j + 1) * cw, h] for h in heads], zeros, k0, (j + 1) * cw, j)
              for j in range(n_sub)]
    carry = tuple(jnp.concatenate([g[c] for g in groups], axis=0) for c in range(4))
    q = [q_ref[0, :, h] for h in heads]
    carry = lax.fori_loop(
        0, qi, lambda it, c: sweep(q, c, pl.multiple_of((qi - 1 - it) * t, t), t, None), carry)
    o_ref[0] = jnp.concatenate([carry[0], carry[2]], axis=1).astype(o_ref.dtype)


def sb_attention(qkv, batch, seq):
    t = _pick(seq, (512, 256, 128))
    x = qkv.reshape(batch, seq, 3 * WIDTH)
    hp = WIDTH // LANES
    out = pl.pallas_call(
        functools.partial(_sb_body, t=t, cw=min(t, 256)),
        grid=(batch, hp, seq // t),
        in_specs=[pl.BlockSpec((1, t, LANES), lambda b, h, i: (b, i, h)),
                  pl.BlockSpec((1, seq, LANES), lambda b, h, i: (b, 0, hp + h)),
                  pl.BlockSpec((1, seq, LANES), lambda b, h, i: (b, 0, 2 * hp + h))],
        out_specs=pl.BlockSpec((1, t, LANES), lambda b, h, i: (b, i, h)),
        out_shape=jax.ShapeDtypeStruct((batch, seq, WIDTH), BF16),
        compiler_params=_params("parallel", "parallel", "parallel"),
        name="sb_attention",
    )(x, x, x)
    return out.reshape(batch * seq, WIDTH)


def _mla_body(q_ref, k_ref, v_ref, o_ref, *, t):
    qi = pl.program_id(2)
    rows = lax.broadcasted_iota(jnp.int32, (t, t), 0)
    cols = lax.broadcasted_iota(jnp.int32, (t, t), 1)
    q = [q_ref[0, :, hh * LANES:(hh + 1) * LANES] for hh in range(2)]

    def block(ks, carry, diagonal):
        s = [_dot_nt(q[hh], k_ref[0, pl.ds(ks, t), hh * LANES:(hh + 1) * LANES]) for hh in range(2)]
        out = []
        for hh in range(2):
            m, l, acc = carry[3 * hh:3 * hh + 3]
            ss = jnp.where(cols <= rows, s[hh], -1e30) if diagonal else s[hh]
            m_new = jnp.maximum(m, jnp.max(ss, axis=-1, keepdims=True))
            p = jnp.exp(ss - m_new)
            corr = jnp.exp(m - m_new)
            l = corr * l + jnp.sum(p, axis=-1, keepdims=True)
            v = v_ref[0, pl.ds(ks, t), hh * HEAD_DIM:(hh + 1) * HEAD_DIM]
            acc = corr * acc + _dot(p.astype(BF16), v)
            out += [m_new, l, acc]
        return tuple(out)

    init = (jnp.full((t, 1), -1e30, F32), jnp.zeros((t, 1), F32), jnp.zeros((t, HEAD_DIM), F32)) * 2
    carry = block(pl.multiple_of(qi * t, t), init, True)
    carry = lax.fori_loop(0, qi, lambda kb, c: block(pl.multiple_of(kb * t, t), c, False), carry)
    o_ref[0] = jnp.concatenate([carry[2] / carry[1], carry[5] / carry[4]], axis=1).astype(o_ref.dtype)


def mla_attention(q, k, v, batch, seq):
    t = _pick(seq, (512, 256, 128))
    hp = HEADS // 2
    out = pl.pallas_call(
        functools.partial(_mla_body, t=t),
        grid=(batch, hp, seq // t),
        in_specs=[pl.BlockSpec((1, t, 2 * LANES), lambda b, h, i: (b, i, h)),
                  pl.BlockSpec((1, seq, 2 * LANES), lambda b, h, i: (b, 0, h)),
                  pl.BlockSpec((1, seq, LANES), lambda b, h, i: (b, 0, h))],
        out_specs=pl.BlockSpec((1, t, LANES), lambda b, h, i: (b, i, h)),
        out_shape=jax.ShapeDtypeStruct((batch, seq, WIDTH), BF16),
        compiler_params=_params("parallel", "parallel", "parallel"),
        name="mla_attention",
    )(q.reshape(batch, seq, HEADS * LANES), k.reshape(batch, seq, HEADS * LANES),
      v.reshape(batch, seq, WIDTH))
    return out.reshape(batch * seq, WIDTH)


RWKV_IN = 3 * WIDTH + 2 * RWKV_LORA_PAD + RWKV_GATE_LORA


def _rwkv_prep_body(*refs, tiles_per_seq, has_vres):
    if has_vres:
        (cols_ref, prev_ref, mu_ref, w0_ref, w2_ref, a0_ref, a2_ref, g2_ref, kk_ref, ka_ref,
         v0_ref, v1_ref, v2_ref, vfirst_ref,
         r_out, lw_out, k_out, v_out, kk_out, a_out, g_out) = refs
    else:
        (cols_ref, prev_ref, mu_ref, w0_ref, w2_ref, a0_ref, a2_ref, g2_ref, kk_ref, ka_ref,
         r_out, lw_out, k_out, v_out, kk_out, a_out, g_out) = refs
    x = cols_ref[...]
    starts_seq = (pl.program_id(0) % tiles_per_seq) == 0
    last_prev = jnp.where(starts_seq, 0.0, prev_ref[7:8, :])
    first_row = lax.broadcasted_iota(jnp.int32, x.shape, 0) == 0
    prev = jnp.where(first_row, last_prev, pltpu.roll(x, 1, 0))
    x = x + (prev - x) * mu_ref[...]
    r = x[:, :WIDTH]
    k = x[:, WIDTH:2 * WIDTH]
    v = x[:, 2 * WIDTH:3 * WIDTH]
    o = 3 * WIDTH
    wd = x[:, o:o + RWKV_LORA_PAD]
    ad = x[:, o + RWKV_LORA_PAD:o + 2 * RWKV_LORA_PAD]
    gd = x[:, o + 2 * RWKV_LORA_PAD:]
    wpre = w0_ref[...] + _dot(jnp.tanh(wd).astype(BF16), w2_ref[...])
    w = -jax.nn.softplus(-wpre) - 0.5
    lw_out[...] = -jnp.exp(w)
    a = jax.nn.sigmoid(a0_ref[...] + _dot(ad.astype(BF16), a2_ref[...]))
    g_out[...] = _dot(jax.nn.sigmoid(gd).astype(BF16), g2_ref[...])
    if has_vres:
        low = _dot(v.astype(BF16), v1_ref[...])
        mix = jax.nn.sigmoid(v0_ref[...] + _dot(low.astype(BF16), v2_ref[...]))
        v = v + (vfirst_ref[...] - v) * mix
    r_out[...] = r
    v_out[...] = v
    kk_out[...] = k * kk_ref[...]
    k_out[...] = k * (1.0 + (a - 1.0) * ka_ref[...])
    a_out[...] = a


def rwkv_prep(cols, seq, mu, w0, w2, a0, a2, g2, k_k, k_a, vres):
    m = cols.shape[0]
    tm = _pick(seq, (256, 128, 64, 32, 16, 8))
    row = lambda t: t.reshape(1, -1).astype(F32)
    args = [cols, cols, row(mu), row(w0), w2, row(a0), a2, g2, row(k_k), row(k_a)]
    full = lambda arr: pl.BlockSpec(arr.shape, lambda i: (0, 0))
    tile = pl.BlockSpec((tm, WIDTH), lambda i: (i, 0))
    in_specs = [pl.BlockSpec((tm, RWKV_IN), lambda i: (i, 0)),
                pl.BlockSpec((8, RWKV_IN), lambda i: (jnp.maximum(i * (tm // 8) - 1, 0), 0))]
    in_specs += [full(t) for t in args[2:]]
    if vres is not None:
        v0, v1, v2, v_first = vres
        extra = [row(v0), v1, v2]
        args += extra + [v_first]
        in_specs += [full(t) for t in extra] + [tile]
    return pl.pallas_call(
        functools.partial(_rwkv_prep_body, tiles_per_seq=seq // tm, has_vres=vres is not None),
        grid=(m // tm,),
        in_specs=in_specs,
        out_specs=[tile] * 7,
        out_shape=[jax.ShapeDtypeStruct((m, WIDTH), F32)] * 7,
        compiler_params=_params("parallel"),
        name="rwkv_prep",
    )(*args)


def _halves(x):
    return x[:, :HEAD_DIM], x[:, HEAD_DIM:]


def _head_sum(x, left):
    s0 = jnp.sum(jnp.where(left, x, 0.0), axis=-1, keepdims=True)
    s1 = jnp.sum(jnp.where(left, 0.0, x), axis=-1, keepdims=True)
    return jnp.where(left, s0, s1)


def _dot3(x, y, dot):
    xh, xl = _split(x)
    yh, yl = _split(y)
    return dot(xh, yh) + (dot(xh, yl) + dot(xl, yh))


def _rwkv_chunk_terms(r, lw, k, v, kk_raw, a):
    c = r.shape[0]
    left = lax.broadcasted_iota(jnp.int32, (c, LANES), 1) < HEAD_DIM
    trow = lax.broadcasted_iota(jnp.int32, (c, LANES), 0)
    kk = kk_raw / jnp.maximum(jnp.sqrt(_head_sum(kk_raw * kk_raw, left)), 1e-12)
    beta = kk * a
    cl = lw
    shift = 1
    while shift < c:
        cl = cl + jnp.where(trow >= shift, pltpu.roll(cl, shift, 0), 0.0)
        shift *= 2
    a_t = -kk * jnp.exp(cl - lw)
    r_t = r * jnp.exp(cl)
    inv = jnp.exp(-cl)
    cl_end = cl[c - 1:c, :]
    tail = jnp.exp(cl_end - cl)
    full = (a_t, r_t, beta * inv, k * inv, v, beta * tail, k * tail, jnp.exp(cl_end))
    return [tuple(_halves(x)[hh] for x in full) for hh in range(2)]


def _rwkv_masks(c):
    ti = lax.broadcasted_iota(jnp.int32, (c, c), 0)
    tj = lax.broadcasted_iota(jnp.int32, (c, c), 1)
    merges = []
    half = 1
    while half < c:
        merges.append((ti // (2 * half) == tj // (2 * half)) & (ti % (2 * half) >= half)
                      & (tj % (2 * half) < half))
        half *= 2
    return ti > tj, ti >= tj, ti == tj, merges


def _rwkv_chain(out, head_terms, masks):
    at_h, rt_h, bt_h, kt_h, v_h, bbar_h, kbar_h, gend_h = head_terms
    strict, incl, eye, merges = masks
    c = at_h.shape[0]
    ar = jnp.concatenate([at_h, rt_h], axis=0).astype(BF16)
    bk = jnp.concatenate([bt_h, kt_h], axis=0).astype(BF16)
    pair = _dot_nt(ar, bk)
    yield
    lower = jnp.where(strict, pair[:c, :c], 0.0)
    akv = _dot(jnp.where(strict, pair[:c, c:], 0.0).astype(BF16), v_h.astype(BF16))
    tinv = eye.astype(F32) + jnp.where(merges[0], lower, 0.0)
    for off in merges[1:]:
        tb = tinv.astype(BF16)
        step = _dot(tb, jnp.where(off, lower, 0.0).astype(BF16))
        yield
        tinv = tinv + _dot(step.astype(BF16), tb)
        yield
    solved = _dot(tinv.astype(BF16), jnp.concatenate([at_h, akv], axis=1).astype(BF16))
    yield
    ta, w = _halves(solved)
    rb = jnp.where(incl, pair[c:, :c], 0.0).astype(BF16)
    rk = jnp.where(incl, pair[c:, c:], 0.0).astype(BF16)
    wv = jnp.concatenate([w, v_h], axis=0).astype(BF16)
    q = rt_h + _dot(rb, ta.astype(BF16))
    y0 = _dot(jnp.concatenate([rb, rk], axis=1), wv)
    g = jnp.where(eye, gend_h, 0.0) + _dot_tn(ta.astype(BF16), bbar_h.astype(BF16))
    d = _dot_tn(wv, jnp.concatenate([bbar_h, kbar_h], axis=0).astype(BF16))
    out.append((q, y0, g, d))


def _rwkv_state_chain(hh, results, state_ref, ys):
    state = state_ref[hh]
    for ci, res in enumerate(results):
        while not res[hh]:
            yield
        q, y0, g, d = res[hh][0]
        ys[ci][hh] = _dot3(q, state, _dot_nt) + y0
        state = _dot3(state, g, _dot) + d
        yield
    state_ref[hh] = state


def _lockstep(chains):
    live = list(chains)
    while live:
        still = []
        for ch in live:
            try:
                next(ch)
                still.append(ch)
            except StopIteration:
                pass
        live = still


def _rwkv_scan_body(r_ref, lw_ref, k_ref, v_ref, kk_ref, a_ref, g_ref, lng_ref, lnb_ref, rk_ref,
                    o_ref, state_ref, *, chunk):
    @pl.when(pl.program_id(2) == 0)
    def _():
        state_ref[...] = jnp.zeros_like(state_ref)

    n_chunks = r_ref.shape[1] // chunk
    left = lax.broadcasted_iota(jnp.int32, (chunk, LANES), 1) < HEAD_DIM
    masks = _rwkv_masks(chunk)
    rows = [slice(ci * chunk, (ci + 1) * chunk) for ci in range(n_chunks)]
    results = [[[], []] for _ in range(n_chunks)]
    chains = []
    for ci in range(n_chunks):
        terms = _rwkv_chunk_terms(r_ref[0, rows[ci], :], lw_ref[0, rows[ci], :], k_ref[0, rows[ci], :],
                                  v_ref[0, rows[ci], :], kk_ref[0, rows[ci], :], a_ref[0, rows[ci], :])
        chains += [_rwkv_chain(results[ci][hh], terms[hh], masks) for hh in range(2)]
    _lockstep(chains)
    ys = [[None, None] for _ in range(n_chunks)]
    _lockstep([_rwkv_state_chain(hh, results, state_ref, ys) for hh in range(2)])

    for ci in range(n_chunks):
        y = jnp.concatenate(ys[ci], axis=1)
        r, k, v = r_ref[0, rows[ci], :], k_ref[0, rows[ci], :], v_ref[0, rows[ci], :]
        mean = _head_sum(y, left) * (1.0 / HEAD_DIM)
        var = _head_sum(jnp.square(y - mean), left) * (1.0 / HEAD_DIM)
        y = (y - mean) * lax.rsqrt(var + RWKV_GN_EPS) * lng_ref[...] + lnb_ref[...]
        bonus = _head_sum(r * k * rk_ref[...], left) * v
        o_ref[0, rows[ci], :] = ((y + bonus) * g_ref[0, rows[ci], :]).astype(o_ref.dtype)


def rwkv_scan(r, lw, k, v, kk, a, g, ln_g, ln_b, r_k, batch, seq):
    chunk = min(RWKV_CHUNK, seq)
    rows = _pick(seq, (512, 256, 128, 64, 32, 16, 8))
    hp = WIDTH // LANES
    seq_spec = pl.BlockSpec((1, rows, LANES), lambda b, h, c: (b, c, h))
    par_spec = pl.BlockSpec((1, LANES), lambda b, h, c: (0, h))
    shaped = [t.reshape(batch, seq, WIDTH) for t in (r, lw, k, v, kk, a, g)]
    pars = [t.reshape(1, WIDTH).astype(F32) for t in (ln_g, ln_b, r_k)]
    out = pl.pallas_call(
        functools.partial(_rwkv_scan_body, chunk=chunk),
        grid=(batch, hp, seq // rows),
        in_specs=[seq_spec] * 7 + [par_spec] * 3,
        out_specs=seq_spec,
        out_shape=jax.ShapeDtypeStruct((batch, seq, WIDTH), BF16),
        scratch_shapes=[pltpu.VMEM((2, HEAD_DIM, HEAD_DIM), F32)],
        compiler_params=_params("parallel", "parallel", "arbitrary"),
        name="rwkv_scan",
    )(*shaped, *pars)
    return out.reshape(batch * seq, WIDTH)


def _merge_body(h, a1, a2, a3, g1, g2, g3, w1, w2, w3, o_ref):
    hh = h[...]
    acc = jax.nn.sigmoid(_dot(hh, g1[...])) * _dot(a1[...], w1[...])
    acc += jax.nn.sigmoid(_dot(hh, g2[...])) * _dot(a2[...], w2[...])
    acc += jax.nn.sigmoid(_dot(hh, g3[...])) * _dot(a3[...], w3[...])
    o_ref[...] = acc.astype(o_ref.dtype)


def merge_branches(h, outs, w_gates, weights):
    m, d = h.shape
    tm = _pick(m, (512, 256, 128, 64, 32, 16, 8))
    tn = _pick(d, (512, 256, 128))
    nb = d // tn
    h_spec = pl.BlockSpec((tm, d), lambda i, j: (i, 0))
    a_spec = pl.BlockSpec((tm, WIDTH), lambda i, j: (i, 0))
    w_spec = pl.BlockSpec((WIDTH, tn), lambda i, j: (0, j))
    g_specs = [pl.BlockSpec((d, tn), lambda i, j, b=b: (0, j + b * nb)) for b in range(3)]
    return pl.pallas_call(
        _merge_body,
        grid=(m // tm, nb),
        in_specs=[h_spec] + [a_spec] * 3 + g_specs + [w_spec] * 3,
        out_specs=pl.BlockSpec((tm, tn), lambda i, j: (i, j)),
        out_shape=jax.ShapeDtypeStruct((m, d), BF16),
        compiler_params=_params("parallel", "parallel"),
        name="merge_branches",
    )(h, *outs, w_gates, w_gates, w_gates, *weights)


ROUTER_PAD = 128


def _slab_cols(ref, n, slabs):
    return jnp.concatenate([ref[pl.ds(s, n, stride=slabs), :] for s in range(slabs)], axis=1)


def _store_slabs(ref, value, slabs):
    n = value.shape[0]
    for s in range(slabs):
        ref[pl.ds(s, n, stride=slabs), :] = value[:, s * LANES:(s + 1) * LANES].astype(ref.dtype)


def _pack_bf16_pairs(h):
    half = h.shape[1] // 2
    bits = pltpu.bitcast(h.astype(BF16).astype(F32), jnp.uint32)
    return (bits[:, :half] >> 16) | bits[:, half:]


def _unpack_bf16_pairs(words):
    lo = pltpu.bitcast(words << 16, F32)
    hi = pltpu.bitcast(words & jnp.uint32(0xFFFF0000), F32)
    return jnp.concatenate([lo, hi], axis=1).astype(BF16)


def _router_body(x_ref, g_ref, w_ref, b_ref, h_ref, logit_ref, *, slabs):
    x = x_ref[...]
    y = x * lax.rsqrt(jnp.mean(x * x, axis=-1, keepdims=True) + NORM_EPS)
    h = y * g_ref[...]
    _store_slabs(h_ref, _pack_bf16_pairs(h), slabs)
    logit_ref[...] = jnp.dot(h, w_ref[...], preferred_element_type=F32, precision=HIGHEST) + b_ref[...]


def router(x, g, w_pad, b_pad):
    m, d = x.shape
    slabs = d // (2 * LANES)
    tm = _pick(m, (256, 128, 64, 32, 16, 8))
    return pl.pallas_call(
        functools.partial(_router_body, slabs=slabs),
        grid=(m // tm,),
        in_specs=[pl.BlockSpec((tm, d), lambda i: (i, 0)),
                  pl.BlockSpec((1, d), lambda i: (0, 0)),
                  pl.BlockSpec((d, ROUTER_PAD), lambda i: (0, 0)),
                  pl.BlockSpec((1, ROUTER_PAD), lambda i: (0, 0))],
        out_specs=[pl.BlockSpec((tm * slabs, LANES), lambda i: (i, 0)),
                   pl.BlockSpec((tm, ROUTER_PAD), lambda i: (i, 0))],
        out_shape=[jax.ShapeDtypeStruct((m * slabs, LANES), jnp.uint32),
                   jax.ShapeDtypeStruct((m, ROUTER_PAD), F32)],
        compiler_params=_params("parallel"),
        name="ffn_norm_router",
    )(x, g.reshape(1, d).astype(F32), w_pad, b_pad)


def _row_gather(idx_ref, first, n, src_ref, buf_ref, sem, slabs):
    def copy(r):
        src = pl.multiple_of(idx_ref[first + r] * slabs, slabs)
        return pltpu.make_async_copy(src_ref.at[pl.ds(src, slabs)],
                                     buf_ref.at[pl.ds(r * slabs, slabs)], sem)

    def start():
        lax.fori_loop(0, n, lambda r, c: (copy(r).start(), c)[1], 0, unroll=8)

    def wait():
        pltpu.make_async_copy(src_ref.at[pl.ds(0, n * slabs)], buf_ref, sem).wait()

    return start, wait


def _expert_body(be_ref, nb_ref, next_ref, tok_ref, h_ref, wg_hbm, wu_hbm, wd_hbm, o_ref,
                 xbuf, wg_f, wu_f, wd_f, wg_s, wu_s, wd_s, sem, wsem, *, layer, in_slabs, slabs):
    i = pl.program_id(0)
    n_blocks = nb_ref[0]
    slot = i % 2

    def gather(block, slot):
        return _row_gather(tok_ref, block * MOE_ROWS, MOE_ROWS, h_ref, xbuf.at[slot], sem.at[slot], in_slabs)

    def weight_copies(expert):
        pairs = ((wg_hbm, wg_f), (wu_hbm, wu_f), (wd_hbm, wd_f))
        return [pltpu.make_async_copy(src.at[layer, expert], dst, wsem.at[j])
                for j, (src, dst) in enumerate(pairs)]

    e = be_ref[i]

    @pl.when(i == 0)
    def _():
        gather(0, 0)[0]()
        for c in weight_copies(e):
            c.start()

    @pl.when(i + 1 < n_blocks)
    def _():
        gather(i + 1, 1 - slot)[0]()

    @pl.when((i < n_blocks) & ((i == 0) | (e != be_ref[jnp.maximum(i - 1, 0)])))
    def _():
        for c in weight_copies(e):
            c.wait()
        wg_s[...] = wg_f[...].astype(BF16)
        wu_s[...] = wu_f[...].astype(BF16)
        wd_s[...] = wd_f[...].astype(BF16)
        upcoming = next_ref[i]

        @pl.when(upcoming >= 0)
        def _():
            for c in weight_copies(upcoming):
                c.start()

    @pl.when(i < n_blocks)
    def _():
        gather(i, slot)[1]()
        x = _unpack_bf16_pairs(_slab_cols(xbuf.at[slot], MOE_ROWS, in_slabs))
        gate = _dot(x, wg_s[...])
        up = _dot(x, wu_s[...])
        mid = (jax.nn.silu(gate) * up).astype(BF16)
        _store_slabs(o_ref, _dot(mid, wd_s[...]), slabs)

    @pl.when(i >= n_blocks)
    def _():
        o_ref[...] = jnp.zeros_like(o_ref)


def expert_ffn(h_rows, row_tok, block_e, next_e, n_blocks, layer, w_gate, w_up, w_down):
    _, n_exp, d, ff = w_gate.shape
    slabs = d // LANES
    in_slabs = d // (2 * LANES)
    p = row_tok.shape[0]
    hbm = pl.BlockSpec(memory_space=pl.ANY)
    return pl.pallas_call(
        functools.partial(_expert_body, layer=layer, in_slabs=in_slabs, slabs=slabs),
        grid_spec=pltpu.PrefetchScalarGridSpec(
            num_scalar_prefetch=4,
            grid=(p // MOE_ROWS,),
            in_specs=[hbm, hbm, hbm, hbm],
            out_specs=pl.BlockSpec((MOE_ROWS * slabs, LANES), lambda i, be, nb, nxt, tok: (i, 0)),
            scratch_shapes=[pltpu.VMEM((2, MOE_ROWS * in_slabs, LANES), jnp.uint32),
                            pltpu.VMEM((d, ff), F32), pltpu.VMEM((d, ff), F32), pltpu.VMEM((ff, d), F32),
                            pltpu.VMEM((d, ff), BF16), pltpu.VMEM((d, ff), BF16),
                            pltpu.VMEM((ff, d), BF16),
                            pltpu.SemaphoreType.DMA((2,)), pltpu.SemaphoreType.DMA((3,))]),
        out_shape=jax.ShapeDtypeStruct((p * slabs, LANES), F32),
        compiler_params=_params("arbitrary"),
        name="expert_ffn",
    )(block_e, n_blocks, next_e, row_tok, h_rows, w_gate, w_up, w_down)


def _combine_body(dest_ref, x_ref, gate_ref, y_ref, o_ref, ybuf, sem, *, tm, slabs):
    i = pl.program_id(0)
    slot = i % 2

    def gather(step, slot):
        return _row_gather(dest_ref, step * tm * TOP_K, tm * TOP_K, y_ref, ybuf.at[slot], sem.at[slot], slabs)

    @pl.when(i == 0)
    def _():
        gather(0, 0)[0]()

    @pl.when(i + 1 < pl.num_programs(0))
    def _():
        gather(i + 1, 1 - slot)[0]()

    gather(i, slot)[1]()
    buf = ybuf.at[slot]
    gate = gate_ref[...]
    g0, g1 = gate[:, 0:1], gate[:, 1:2]
    for s in range(slabs):
        y0 = buf[pl.ds(s, tm, stride=slabs), :]
        y1 = buf[pl.ds(tm * slabs + s, tm, stride=slabs), :]
        cols = slice(s * LANES, (s + 1) * LANES)
        o_ref[:, cols] = x_ref[:, cols] + (g0 * y0 + g1 * y1)


def moe_combine(x, y_rows, dest, gate):
    m, d = x.shape
    slabs = d // LANES
    tm = _pick(m, (128, 64, 32, 16, 8))
    tile = lambda i, dest: (i, 0)
    return pl.pallas_call(
        functools.partial(_combine_body, tm=tm, slabs=slabs),
        grid_spec=pltpu.PrefetchScalarGridSpec(
            num_scalar_prefetch=1,
            grid=(m // tm,),
            in_specs=[pl.BlockSpec((tm, d), tile), pl.BlockSpec((tm, TOP_K), tile),
                      pl.BlockSpec(memory_space=pl.ANY)],
            out_specs=pl.BlockSpec((tm, d), tile),
            scratch_shapes=[pltpu.VMEM((2, tm * TOP_K * slabs, LANES), F32),
                            pltpu.SemaphoreType.DMA((2,))]),
        out_shape=jax.ShapeDtypeStruct((m, d), F32),
        compiler_params=_params("arbitrary"),
        name="moe_combine",
    )(dest.reshape(m // tm, tm, TOP_K).transpose(0, 2, 1).reshape(m * TOP_K), x, gate, y_rows)


def _routing_tables(logits):
    t = logits.shape[0]
    n_assign = t * TOP_K
    tok = jnp.arange(t)
    g_logits = logits[:, :N_GROUPS]
    grp = jnp.argmax(g_logits, axis=-1)
    p_grp = jax.nn.softmax(g_logits, axis=-1)[tok, grp]
    e_logits = logits[:, N_GROUPS:N_GROUPS + N_EXPERTS].reshape(t, N_GROUPS, EXPERTS_PER_GROUP)
    p_in, idx_in = lax.top_k(jax.nn.softmax(e_logits[tok, grp], axis=-1), TOP_K)
    gate = p_grp[:, None] * p_in / jnp.sum(p_in, axis=-1, keepdims=True)
    flat_e = (grp[:, None] * EXPERTS_PER_GROUP + idx_in).reshape(n_assign).astype(jnp.int32)
    onehot = (flat_e[:, None] == jnp.arange(N_EXPERTS, dtype=jnp.int32)[None, :]).astype(jnp.int32)
    rank = jnp.sum((jnp.cumsum(onehot, axis=0) - onehot) * onehot, axis=1)
    counts = jnp.sum(onehot, axis=0)
    padded = (counts + MOE_ROWS - 1) // MOE_ROWS * MOE_ROWS
    pend = jnp.cumsum(padded)
    dest = ((pend - padded)[flat_e] + rank).astype(jnp.int32)
    n_blocks_max = -(-n_assign // MOE_ROWS) + N_EXPERTS
    row_tok = jnp.zeros((n_blocks_max * MOE_ROWS,), jnp.int32).at[dest].set(
        jnp.repeat(tok, TOP_K).astype(jnp.int32))
    block_start = jnp.arange(n_blocks_max, dtype=pend.dtype) * MOE_ROWS
    block_e = jnp.minimum(jnp.sum(pend[None, :] <= block_start[:, None], axis=1),
                          N_EXPERTS - 1).astype(jnp.int32)
    n_blocks = (pend[-1:] // MOE_ROWS).astype(jnp.int32)
    after = jnp.sum(block_e[None, :] <= block_e[:, None], axis=1)
    next_e = jnp.where(after < n_blocks[0], block_e[jnp.minimum(after, n_blocks_max - 1)], -1)
    return gate.astype(F32), dest.reshape(t, TOP_K), row_tok, block_e, next_e.astype(jnp.int32), n_blocks


def hier_moe_residual(x, norm_g, rg_w, rg_b, re_w, re_b, layer, w_gate, w_up, w_down):
    t, d = x.shape
    w_pad = jnp.zeros((d, ROUTER_PAD), F32).at[:, :N_GROUPS].set(rg_w)
    w_pad = w_pad.at[:, N_GROUPS:N_GROUPS + N_EXPERTS].set(re_w)
    b_pad = jnp.zeros((1, ROUTER_PAD), F32).at[0, :N_GROUPS].set(rg_b)
    b_pad = b_pad.at[0, N_GROUPS:N_GROUPS + N_EXPERTS].set(re_b)
    h_rows, logits = router(x, norm_g, w_pad, b_pad)
    gate, dest, row_tok, block_e, next_e, n_blocks = _routing_tables(logits)
    y_rows = expert_ffn(h_rows, row_tok, block_e, next_e, n_blocks, layer, w_gate, w_up, w_down)
    return moe_combine(x, y_rows, dest, gate)


def _mla_q_weight(w_uq):
    w = w_uq.reshape(MLA_Q_LORA, HEADS, HEAD_DIM + MLA_ROPE_DIM)
    w = jnp.pad(w, ((0, 0), (0, 0), (0, LANES - HEAD_DIM - MLA_ROPE_DIM)))
    return w.reshape(MLA_Q_LORA, HEADS * LANES).astype(BF16)


def _mla_kv_weights(w_ukv):
    w = w_ukv.reshape(MLA_KV_LORA, HEADS, 2 * HEAD_DIM)
    wk = jnp.pad(w[:, :, :HEAD_DIM], ((0, 0), (0, 0), (0, LANES - HEAD_DIM)))
    wv = w[:, :, HEAD_DIM:]
    return (wk.reshape(MLA_KV_LORA, HEADS * LANES).astype(BF16),
            wv.reshape(MLA_KV_LORA, WIDTH).astype(BF16))


def _rope_slot_tables(positions):
    half = MLA_ROPE_DIM // 2
    inv_freq = ROPE_THETA ** (-jnp.arange(0, MLA_ROPE_DIM, 2, dtype=F32) / MLA_ROPE_DIM)
    ang = positions.astype(F32).reshape(-1, 1) * inv_freq[None, :]
    cos, sin = jnp.cos(ang), jnp.sin(ang)
    t = ang.shape[0]
    pad = jnp.zeros((t, LANES - HEAD_DIM - 2 * half), F32)
    cos_t = jnp.concatenate([jnp.ones((t, HEAD_DIM), F32), cos, cos, pad], axis=1)
    sin_t = jnp.concatenate([jnp.zeros((t, HEAD_DIM), F32), -sin, sin, pad], axis=1)
    return cos_t, sin_t


def _mixer_block(x, l, batch, seq, cos_t, sin_t, v_first, p):
    t, d = x.shape
    h = rmsnorm(x, p["attn_norm_g"][l], BF16, name="attn_norm")
    w_in = p["w_in"][l]
    o_rwkv = 3 * WIDTH
    n_rwkv = 3 * WIDTH + RWKV_DECAY_LORA + RWKV_ICLR_LORA + RWKV_GATE_LORA
    o_mla = o_rwkv + n_rwkv
    n_mla = MLA_Q_LORA + MLA_KV_LORA + MLA_ROPE_DIM
    o_gate = o_mla + n_mla

    qscale = jnp.concatenate([jnp.full((1, WIDTH), HEAD_DIM ** -0.5, F32), jnp.ones((1, 2 * WIDTH), F32)], axis=1)
    (qkv,) = matmul(h, w_in[:, :o_rwkv].astype(BF16), [BF16],
                    epilogue=lambda acc, s: (acc * s,), extras=[(qscale, "col", 0)], name="proj_sb")
    o_sb = sb_attention(qkv, batch, seq)

    wr = w_in[:, o_rwkv:o_mla]
    zpad = jnp.zeros((d, RWKV_LORA_PAD - RWKV_DECAY_LORA), F32)
    c0 = 3 * WIDTH
    w_rwkv = jnp.concatenate([wr[:, :c0], wr[:, c0:c0 + RWKV_DECAY_LORA], zpad,
                              wr[:, c0 + RWKV_DECAY_LORA:c0 + 2 * RWKV_DECAY_LORA], zpad,
                              wr[:, c0 + 2 * RWKV_DECAY_LORA:]], axis=1).astype(BF16)
    (cols,) = matmul(h, w_rwkv, [F32], name="proj_rwkv")
    mu = p["rwkv_mu"][l]
    mpad = jnp.zeros((RWKV_LORA_PAD - RWKV_DECAY_LORA,), F32)
    mu_pad = jnp.concatenate([mu[:c0], mu[c0:c0 + RWKV_DECAY_LORA], mpad,
                              mu[c0 + RWKV_DECAY_LORA:c0 + 2 * RWKV_DECAY_LORA], mpad,
                              mu[c0 + 2 * RWKV_DECAY_LORA:]])
    rpad = ((0, RWKV_LORA_PAD - RWKV_DECAY_LORA), (0, 0))
    vres = None
    if l > 0:
        vres = (p["rwkv_v0"][l - 1], p["rwkv_v1"][l - 1].astype(BF16), p["rwkv_v2"][l - 1].astype(BF16), v_first)
    r, lw, k, v, kk, a, g = rwkv_prep(
        cols, seq, mu_pad, p["rwkv_w0"][l], jnp.pad(p["rwkv_w2"][l], rpad).astype(BF16),
        p["rwkv_a0"][l], jnp.pad(p["rwkv_a2"][l], rpad).astype(BF16), p["rwkv_g2"][l].astype(BF16),
        p["rwkv_k_k"][l], p["rwkv_k_a"][l], vres)
    if l == 0:
        v_first = v
    o_rw = rwkv_scan(r, lw, k, v, kk, a, g, p["rwkv_ln_g"][l], p["rwkv_ln_b"][l], p["rwkv_r_k"][l],
                     batch, seq)

    wm = w_in[:, o_mla:o_gate]
    w_mla = jnp.concatenate([wm[:, :MLA_Q_LORA + MLA_KV_LORA], jnp.zeros((d, HEAD_DIM), F32),
                             wm[:, MLA_Q_LORA + MLA_KV_LORA:],
                             jnp.zeros((d, LANES - HEAD_DIM - MLA_ROPE_DIM), F32)], axis=1).astype(BF16)
    (mcols,) = matmul(h, w_mla, [F32], name="proj_mla")
    cq = rmsnorm(mcols, p["mla_q_norm_g"][l], BF16, width=MLA_Q_LORA, col_block=0, name="mla_q_norm")
    ckv = rmsnorm(mcols, p["mla_kv_norm_g"][l], BF16, width=MLA_KV_LORA,
                  col_block=MLA_Q_LORA // MLA_KV_LORA, name="mla_kv_norm")
    mla_scale = (HEAD_DIM + MLA_ROPE_DIM) ** -0.5
    (q_mla,) = matmul(cq, _mla_q_weight(p["mla_w_uq"][l]), [BF16],
                      epilogue=lambda acc, c, s: (_rope_slots(acc, c, s) * mla_scale,),
                      extras=[(cos_t, "row", 0), (sin_t, "row", 0)], name="mla_q_up")
    wk, wv = _mla_kv_weights(p["mla_w_ukv"][l])
    kpe_block = (MLA_Q_LORA + MLA_KV_LORA) // LANES

    def k_epilogue(acc, kpe, c, s):
        kr = _rope_slots(kpe, c, s)
        return (acc + jnp.concatenate([kr] * (acc.shape[1] // LANES), axis=1),)

    kpe = mcols[:, kpe_block * LANES:(kpe_block + 1) * LANES]
    (k_mla,) = matmul(ckv, wk, [BF16], epilogue=k_epilogue,
                      extras=[(kpe, "row", 0), (cos_t, "row", 0), (sin_t, "row", 0)], name="mla_k_up")
    (v_mla,) = matmul(ckv, wv, [BF16], name="mla_v_up")
    o_mla_out = mla_attention(q_mla, k_mla, v_mla, batch, seq)

    merged = merge_branches(h, [o_sb, o_rw, o_mla_out], w_in[:, o_gate:].astype(BF16),
                            [p["w_br_sb"][l].astype(BF16), p["w_br_rwkv"][l].astype(BF16),
                             p["w_br_mla"][l].astype(BF16)])
    (x,) = matmul(merged, p["w_out"][l].astype(BF16), [F32],
                  epilogue=lambda acc, res: (res + acc,), extras=[(x, "tile", 0)], name="out_proj")
    return x, v_first


def kernel(x, positions, attn_norm_g, w_in, rwkv_mu, rwkv_w0, rwkv_w2, rwkv_a0, rwkv_a2, rwkv_g2, rwkv_k_k, rwkv_k_a, rwkv_r_k, rwkv_ln_g, rwkv_ln_b, rwkv_v0, rwkv_v1, rwkv_v2, mla_q_norm_g, mla_w_uq, mla_kv_norm_g, mla_w_ukv, w_br_sb, w_br_rwkv, w_br_mla, w_out, ffn_norm_g, router_group_w, router_group_b, router_expert_w, router_expert_b, expert_w_gate, expert_w_up, expert_w_down, final_norm_g):
    p = dict(attn_norm_g=attn_norm_g, w_in=w_in, rwkv_mu=rwkv_mu, rwkv_w0=rwkv_w0, rwkv_w2=rwkv_w2,
             rwkv_a0=rwkv_a0, rwkv_a2=rwkv_a2, rwkv_g2=rwkv_g2, rwkv_k_k=rwkv_k_k, rwkv_k_a=rwkv_k_a,
             rwkv_r_k=rwkv_r_k, rwkv_ln_g=rwkv_ln_g, rwkv_ln_b=rwkv_ln_b, rwkv_v0=rwkv_v0,
             rwkv_v1=rwkv_v1, rwkv_v2=rwkv_v2, mla_q_norm_g=mla_q_norm_g, mla_w_uq=mla_w_uq,
             mla_kv_norm_g=mla_kv_norm_g, mla_w_ukv=mla_w_ukv, w_br_sb=w_br_sb, w_br_rwkv=w_br_rwkv,
             w_br_mla=w_br_mla, w_out=w_out)
    batch, seq, d = x.shape
    depth = w_in.shape[0]
    cos_t, sin_t = _rope_slot_tables(positions)
    xt = x.reshape(batch * seq, d)
    v_first = None
    for l in range(depth):
        xt, v_first = _mixer_block(xt, l, batch, seq, cos_t, sin_t, v_first, p)
        xt = hier_moe_residual(xt, ffn_norm_g[l], router_group_w[l], router_group_b[l],
                               router_expert_w[l], router_expert_b[l],
                               l, expert_w_gate, expert_w_up, expert_w_down)
    return rmsnorm(xt, final_norm_g, x.dtype, name="final_norm").reshape(batch, seq, d)
```

```python
import functools

import jax
import jax.numpy as jnp
from jax import lax
from jax.experimental import pallas as pl
from jax.experimental.pallas import tpu as pltpu

F32 = jnp.float32
BF16 = jnp.bfloat16
HIGHEST = lax.Precision.HIGHEST

NORM_EPS = 1e-6
HEADS = 16
HEAD_DIM = 64
WIDTH = HEADS * HEAD_DIM
LANES = 128
RWKV_DECAY_LORA = 96
RWKV_ICLR_LORA = 96
RWKV_GATE_LORA = 256
RWKV_LORA_PAD = 128
RWKV_GN_EPS = 64e-5
RWKV_CHUNK = 64
MLA_Q_LORA = 768
MLA_KV_LORA = 256
MLA_ROPE_DIM = 32
ROPE_THETA = 10000.0
N_GROUPS = 8
EXPERTS_PER_GROUP = 8
N_EXPERTS = N_GROUPS * EXPERTS_PER_GROUP
TOP_K = 2
MOE_ROWS = 128
VMEM_LIMIT = 48 * 1024 * 1024


def _params(*sem):
    return pltpu.CompilerParams(dimension_semantics=sem, vmem_limit_bytes=VMEM_LIMIT)


def _pick(n, cands):
    for c in cands:
        if n % c == 0:
            return c
    raise ValueError(f"no tile for {n} in {cands}")


def _dot(x, y):
    return jnp.dot(x, y, preferred_element_type=F32)


def _dot_nt(x, y):
    return lax.dot_general(x, y, (((1,), (1,)), ((), ())), preferred_element_type=F32)


def _dot_tn(x, y):
    return lax.dot_general(x, y, (((0,), (0,)), ((), ())), preferred_element_type=F32)


def _split(x):
    hi = x.astype(BF16)
    return hi, (x - hi.astype(F32)).astype(BF16)


def _rmsnorm_body(x_ref, g_ref, o_ref):
    x = x_ref[...].astype(F32)
    y = x * lax.rsqrt(jnp.mean(x * x, axis=-1, keepdims=True) + NORM_EPS)
    o_ref[...] = (y * g_ref[...]).astype(o_ref.dtype)


def rmsnorm(x, g, out_dtype, *, width=None, col_block=0, name="rmsnorm"):
    m = x.shape[0]
    width = x.shape[1] if width is None else width
    tm = _pick(m, (512, 256, 128, 64, 32, 16, 8))
    return pl.pallas_call(
        _rmsnorm_body,
        grid=(m // tm,),
        in_specs=[pl.BlockSpec((tm, width), lambda i: (i, col_block)),
                  pl.BlockSpec((1, width), lambda i: (0, 0))],
        out_specs=pl.BlockSpec((tm, width), lambda i: (i, 0)),
        out_shape=jax.ShapeDtypeStruct((m, width), out_dtype),
        compiler_params=_params("parallel"),
        name=name,
    )(x, g.reshape(1, width).astype(F32))


def _mm_body(*refs, n_extra, epilogue):
    a_ref, w_ref = refs[:2]
    extra = [r[...] for r in refs[2:2 + n_extra]]
    outs = refs[2 + n_extra:]
    acc = _dot(a_ref[...], w_ref[...])
    res = epilogue(acc, *extra)
    for o, r in zip(outs, res):
        o[...] = r.astype(o.dtype)


def matmul(a, w, out_dtypes, *, epilogue=None, extras=(), name="matmul"):
    m = a.shape[0]
    k, n = w.shape
    tm = _pick(m, (1024, 512, 256, 128, 64, 32, 16, 8))
    tn = n if n <= 1280 else _pick(n, (512, 256, 128))
    if epilogue is None:
        epilogue = lambda acc: (acc,)
    in_specs = [pl.BlockSpec((tm, k), lambda i, j: (i, 0)),
                pl.BlockSpec((k, tn), lambda i, j: (0, j))]
    args = [a, w]
    for arr, kind, off in extras:
        if kind == "row":
            in_specs.append(pl.BlockSpec((tm, arr.shape[1]), lambda i, j: (i, 0)))
        elif kind == "col":
            in_specs.append(pl.BlockSpec((1, tn), lambda i, j: (0, j)))
        else:
            in_specs.append(pl.BlockSpec((tm, tn), lambda i, j, off=off: (i, j + off)))
        args.append(arr)
    return pl.pallas_call(
        functools.partial(_mm_body, n_extra=len(extras), epilogue=epilogue),
        grid=(m // tm, n // tn),
        in_specs=in_specs,
        out_specs=[pl.BlockSpec((tm, tn), lambda i, j: (i, j)) for _ in out_dtypes],
        out_shape=[jax.ShapeDtypeStruct((m, n), dt) for dt in out_dtypes],
        compiler_params=_params("parallel", "parallel"),
        name=name,
    )(*args)


def _rope_slots(x, cos_t, sin_t):
    n = x.shape[-1]
    reps = n // LANES
    lane = lax.broadcasted_iota(jnp.int32, x.shape, 1) % LANES
    from_hi = pltpu.roll(x, n - MLA_ROPE_DIM // 2, 1)
    from_lo = pltpu.roll(x, MLA_ROPE_DIM // 2, 1)
    swapped = jnp.where(lane < HEAD_DIM + MLA_ROPE_DIM // 2, from_hi, from_lo)
    if reps > 1:
        cos_t = jnp.concatenate([cos_t] * reps, axis=1)
        sin_t = jnp.concatenate([sin_t] * reps, axis=1)
    return x * cos_t + swapped * sin_t


def _sb_body(q_ref, k_ref, v_ref, o_ref, *, t, cw):
    qi = pl.program_id(2)
    n_sub = t // cw
    first = lax.broadcasted_iota(jnp.int32, (cw, cw), 0)
    second = lax.broadcasted_iota(jnp.int32, (cw, cw), 1)
    before = second < first
    later = (first > second).astype(BF16)
    later2 = jnp.concatenate([later, later], axis=0)
    heads = [slice(hh * HEAD_DIM, (hh + 1) * HEAD_DIM) for hh in range(2)]

    def sweep(q, carry, ks, n_keys, diag_sub):
        z = [_dot_nt(q[hh], k_ref[0, pl.ds(ks, n_keys), h]) for hh, h in enumerate(heads)]
        acc = [carry[0], carry[2]]
        run = [carry[1], carry[3]]
        subs = n_keys // cw
        weights = [[None] * subs for _ in heads]
        for sub in reversed(range(subs)):
            for hh in range(2):
                zz = z[hh][:, sub * cw:(sub + 1) * cw]
                neg_abs = pltpu.bitcast(pltpu.bitcast(zz, jnp.uint32) | jnp.uint32(0x80000000), F32)
                sp = jnp.maximum(zz, 0.0) + jnp.log(1.0 + jnp.exp(neg_abs))
                mass = jnp.where(before, sp, 0.0) if sub == diag_sub else sp
                hi, lo = _split(mass)
                after = _dot(jnp.concatenate([hi, lo], axis=1), later2)
                wgt = jnp.exp(((zz - sp) - after) - run[hh])
                if sub == diag_sub:
                    wgt = jnp.where(before, wgt, 0.0)
                weights[hh][sub] = wgt.astype(BF16)
                run[hh] = run[hh] + (after[:, :1] + mass[:, :1])
        for hh, h in enumerate(heads):
            acc[hh] = acc[hh] + _dot(jnp.concatenate(weights[hh], axis=1), v_ref[0, pl.ds(ks, n_keys), h])
        return acc[0], run[0], acc[1], run[1]

    k0 = pl.multiple_of(qi * t, t)
    zeros = (jnp.zeros((cw, HEAD_DIM), F32), jnp.zeros((cw, 1), F32)) * 2
    groups = [sweep([q_ref[0, j * cw:(j + 1) * cw, h] for h in heads], zeros, k0, (j + 1) * cw, j)
              for j in range(n_sub)]
    carry = tuple(jnp.concatenate([g[c] for g in groups], axis=0) for c in range(4))
    q = [q_ref[0, :, h] for h in heads]
    carry = lax.fori_loop(
        0, qi, lambda it, c: sweep(q, c, pl.multiple_of((qi - 1 - it) * t, t), t, None), carry)
    o_ref[0] = jnp.concatenate([carry[0], carry[2]], axis=1).astype(o_ref.dtype)


def sb_attention(qkv, batch, seq):
    t = _pick(seq, (1024, 512, 256, 128))
    x = qkv.reshape(batch, seq, 3 * WIDTH)
    hp = WIDTH // LANES
    out = pl.pallas_call(
        functools.partial(_sb_body, t=t, cw=min(t, 256)),
        grid=(batch, hp, seq // t),
        in_specs=[pl.BlockSpec((1, t, LANES), lambda b, h, i: (b, i, h)),
                  pl.BlockSpec((1, seq, LANES), lambda b, h, i: (b, 0, hp + h)),
                  pl.BlockSpec((1, seq, LANES), lambda b, h, i: (b, 0, 2 * hp + h))],
        out_specs=pl.BlockSpec((1, t, LANES), lambda b, h, i: (b, i, h)),
        out_shape=jax.ShapeDtypeStruct((batch, seq, WIDTH), BF16),
        compiler_params=_params("parallel", "parallel", "parallel"),
        name="sb_attention",
    )(x, x, x)
    return out.reshape(batch * seq, WIDTH)


def _mla_body(q_ref, k_ref, v_ref, o_ref, *, t):
    qi = pl.program_id(2)
    rows = lax.broadcasted_iota(jnp.int32, (t, t), 0)
    cols = lax.broadcasted_iota(jnp.int32, (t, t), 1)
    q = [q_ref[0, :, hh * LANES:(hh + 1) * LANES] for hh in range(2)]

    def block(ks, carry, diagonal):
        s = [_dot_nt(q[hh], k_ref[0, pl.ds(ks, t), hh * LANES:(hh + 1) * LANES]) for hh in range(2)]
        out = []
        for hh in range(2):
            m, l, acc = carry[3 * hh:3 * hh + 3]
            ss = jnp.where(cols <= rows, s[hh], -1e30) if diagonal else s[hh]
            m_new = jnp.maximum(m, jnp.max(ss, axis=-1, keepdims=True))
            p = jnp.exp(ss - m_new)
            corr = jnp.exp(m - m_new)
            l = corr * l + jnp.sum(p, axis=-1, keepdims=True)
            v = v_ref[0, pl.ds(ks, t), hh * HEAD_DIM:(hh + 1) * HEAD_DIM]
            acc = corr * acc + _dot(p.astype(BF16), v)
            out += [m_new, l, acc]
        return tuple(out)

    init = (jnp.full((t, 1), -1e30, F32), jnp.zeros((t, 1), F32), jnp.zeros((t, HEAD_DIM), F32)) * 2
    carry = block(pl.multiple_of(qi * t, t), init, True)
    carry = lax.fori_loop(0, qi, lambda kb, c: block(pl.multiple_of(kb * t, t), c, False), carry)
    o_ref[0] = jnp.concatenate([carry[2] / carry[1], carry[5] / carry[4]], axis=1).astype(o_ref.dtype)


def mla_attention(q, k, v, batch, seq):
    t = _pick(seq, (512, 256, 128))
    hp = HEADS // 2
    out = pl.pallas_call(
        functools.partial(_mla_body, t=t),
        grid=(batch, hp, seq // t),
        in_specs=[pl.BlockSpec((1, t, 2 * LANES), lambda b, h, i: (b, i, h)),
                  pl.BlockSpec((1, seq, 2 * LANES), lambda b, h, i: (b, 0, h)),
                  pl.BlockSpec((1, seq, LANES), lambda b, h, i: (b, 0, h))],
        out_specs=pl.BlockSpec((1, t, LANES), lambda b, h, i: (b, i, h)),
        out_shape=jax.ShapeDtypeStruct((batch, seq, WIDTH), BF16),
        compiler_params=_params("parallel", "parallel", "parallel"),
        name="mla_attention",
    )(q.reshape(batch, seq, HEADS * LANES), k.reshape(batch, seq, HEADS * LANES),
      v.reshape(batch, seq, WIDTH))
    return out.reshape(batch * seq, WIDTH)


RWKV_IN = 3 * WIDTH + 2 * RWKV_LORA_PAD + RWKV_GATE_LORA


def _rwkv_prep_body(*refs, tiles_per_seq, has_vres):
    if has_vres:
        (cols_ref, prev_ref, mu_ref, w0_ref, w2_ref, a0_ref, a2_ref, g2_ref, kk_ref, ka_ref,
         v0_ref, v1_ref, v2_ref, vfirst_ref,
         r_out, lw_out, k_out, v_out, kk_out, a_out, g_out) = refs
    else:
        (cols_ref, prev_ref, mu_ref, w0_ref, w2_ref, a0_ref, a2_ref, g2_ref, kk_ref, ka_ref,
         r_out, lw_out, k_out, v_out, kk_out, a_out, g_out) = refs
    x = cols_ref[...]
    starts_seq = (pl.program_id(0) % tiles_per_seq) == 0
    last_prev = jnp.where(starts_seq, 0.0, prev_ref[7:8, :])
    first_row = lax.broadcasted_iota(jnp.int32, x.shape, 0) == 0
    prev = jnp.where(first_row, last_prev, pltpu.roll(x, 1, 0))
    x = x + (prev - x) * mu_ref[...]
    r = x[:, :WIDTH]
    k = x[:, WIDTH:2 * WIDTH]
    v = x[:, 2 * WIDTH:3 * WIDTH]
    o = 3 * WIDTH
    wd = x[:, o:o + RWKV_LORA_PAD]
    ad = x[:, o + RWKV_LORA_PAD:o + 2 * RWKV_LORA_PAD]
    gd = x[:, o + 2 * RWKV_LORA_PAD:]
    wpre = w0_ref[...] + _dot(jnp.tanh(wd).astype(BF16), w2_ref[...])
    w = -jax.nn.softplus(-wpre) - 0.5
    lw_out[...] = -jnp.exp(w)
    a = jax.nn.sigmoid(a0_ref[...] + _dot(ad.astype(BF16), a2_ref[...]))
    g_out[...] = _dot(jax.nn.sigmoid(gd).astype(BF16), g2_ref[...])
    if has_vres:
        low = _dot(v.astype(BF16), v1_ref[...])
        mix = jax.nn.sigmoid(v0_ref[...] + _dot(low.astype(BF16), v2_ref[...]))
        v = v + (vfirst_ref[...] - v) * mix
    r_out[...] = r
    v_out[...] = v
    kk_out[...] = k * kk_ref[...]
    k_out[...] = k * (1.0 + (a - 1.0) * ka_ref[...])
    a_out[...] = a


def rwkv_prep(cols, seq, mu, w0, w2, a0, a2, g2, k_k, k_a, vres):
    m = cols.shape[0]
    tm = _pick(seq, (256, 128, 64, 32, 16, 8))
    row = lambda t: t.reshape(1, -1).astype(F32)
    args = [cols, cols, row(mu), row(w0), w2, row(a0), a2, g2, row(k_k), row(k_a)]
    full = lambda arr: pl.BlockSpec(arr.shape, lambda i: (0, 0))
    tile = pl.BlockSpec((tm, WIDTH), lambda i: (i, 0))
    in_specs = [pl.BlockSpec((tm, RWKV_IN), lambda i: (i, 0)),
                pl.BlockSpec((8, RWKV_IN), lambda i: (jnp.maximum(i * (tm // 8) - 1, 0), 0))]
    in_specs += [full(t) for t in args[2:]]
    if vres is not None:
        v0, v1, v2, v_first = vres
        extra = [row(v0), v1, v2]
        args += extra + [v_first]
        in_specs += [full(t) for t in extra] + [tile]
    return pl.pallas_call(
        functools.partial(_rwkv_prep_body, tiles_per_seq=seq // tm, has_vres=vres is not None),
        grid=(m // tm,),
        in_specs=in_specs,
        out_specs=[tile] * 7,
        out_shape=[jax.ShapeDtypeStruct((m, WIDTH), F32)] * 7,
        compiler_params=_params("parallel"),
        name="rwkv_prep",
    )(*args)


def _halves(x):
    return x[:, :HEAD_DIM], x[:, HEAD_DIM:]


def _head_sum(x, left):
    s0 = jnp.sum(jnp.where(left, x, 0.0), axis=-1, keepdims=True)
    s1 = jnp.sum(jnp.where(left, 0.0, x), axis=-1, keepdims=True)
    return jnp.where(left, s0, s1)


def _dot3(x, y, dot):
    xh, xl = _split(x)
    yh, yl = _split(y)
    return dot(xh, yh) + (dot(xh, yl) + dot(xl, yh))


def _rwkv_chunk_terms(r, lw, k, v, kk_raw, a):
    c = r.shape[0]
    left = lax.broadcasted_iota(jnp.int32, (c, LANES), 1) < HEAD_DIM
    trow = lax.broadcasted_iota(jnp.int32, (c, LANES), 0)
    kk = kk_raw / jnp.maximum(jnp.sqrt(_head_sum(kk_raw * kk_raw, left)), 1e-12)
    beta = kk * a
    cl = lw
    shift = 1
    while shift < c:
        cl = cl + jnp.where(trow >= shift, pltpu.roll(cl, shift, 0), 0.0)
        shift *= 2
    a_t = -kk * jnp.exp(cl - lw)
    r_t = r * jnp.exp(cl)
    inv = jnp.exp(-cl)
    cl_end = cl[c - 1:c, :]
    tail = jnp.exp(cl_end - cl)
    full = (a_t, r_t, beta * inv, k * inv, v, beta * tail, k * tail, jnp.exp(cl_end))
    return [tuple(_halves(x)[hh] for x in full) for hh in range(2)]


def _rwkv_masks(c):
    ti = lax.broadcasted_iota(jnp.int32, (c, c), 0)
    tj = lax.broadcasted_iota(jnp.int32, (c, c), 1)
    merges = []
    half = 1
    while half < c:
        merges.append((ti // (2 * half) == tj // (2 * half)) & (ti % (2 * half) >= half)
                      & (tj % (2 * half) < half))
        half *= 2
    return ti > tj, ti >= tj, ti == tj, merges


def _rwkv_chain(out, head_terms, masks):
    at_h, rt_h, bt_h, kt_h, v_h, bbar_h, kbar_h, gend_h = head_terms
    strict, incl, eye, merges = masks
    c = at_h.shape[0]
    ar = jnp.concatenate([at_h, rt_h], axis=0).astype(BF16)
    bk = jnp.concatenate([bt_h, kt_h], axis=0).astype(BF16)
    pair = _dot_nt(ar, bk)
    yield
    lower = jnp.where(strict, pair[:c, :c], 0.0)
    akv = _dot(jnp.where(strict, pair[:c, c:], 0.0).astype(BF16), v_h.astype(BF16))
    tinv = eye.astype(F32) + jnp.where(merges[0], lower, 0.0)
    for off in merges[1:]:
        tb = tinv.astype(BF16)
        step = _dot(tb, jnp.where(off, lower, 0.0).astype(BF16))
        yield
        tinv = tinv + _dot(step.astype(BF16), tb)
        yield
    solved = _dot(tinv.astype(BF16), jnp.concatenate([at_h, akv], axis=1).astype(BF16))
    yield
    ta, w = _halves(solved)
    rb = jnp.where(incl, pair[c:, :c], 0.0).astype(BF16)
    rk = jnp.where(incl, pair[c:, c:], 0.0).astype(BF16)
    wv = jnp.concatenate([w, v_h], axis=0).astype(BF16)
    q = rt_h + _dot(rb, ta.astype(BF16))
    y0 = _dot(jnp.concatenate([rb, rk], axis=1), wv)
    g = jnp.where(eye, gend_h, 0.0) + _dot_tn(ta.astype(BF16), bbar_h.astype(BF16))
    d = _dot_tn(wv, jnp.concatenate([bbar_h, kbar_h], axis=0).astype(BF16))
    out.append((q, y0, g, d))


def _rwkv_state_chain(hh, results, state_ref, ys):
    state = state_ref[hh]
    for ci, res in enumerate(results):
        while not res[hh]:
            yield
        q, y0, g, d = res[hh][0]
        ys[ci][hh] = _dot3(q, state, _dot_nt) + y0
        state = _dot3(state, g, _dot) + d
        yield
    state_ref[hh] = state


def _lockstep(chains):
    live = list(chains)
    while live:
        still = []
        for ch in live:
            try:
                next(ch)
                still.append(ch)
            except StopIteration:
                pass
        live = still


def _rwkv_scan_body(r_ref, lw_ref, k_ref, v_ref, kk_ref, a_ref, g_ref, lng_ref, lnb_ref, rk_ref,
                    o_ref, state_ref, *, chunk):
    @pl.when(pl.program_id(2) == 0)
    def _():
        state_ref[...] = jnp.zeros_like(state_ref)

    n_chunks = r_ref.shape[1] // chunk
    left = lax.broadcasted_iota(jnp.int32, (chunk, LANES), 1) < HEAD_DIM
    masks = _rwkv_masks(chunk)
    rows = [slice(ci * chunk, (ci + 1) * chunk) for ci in range(n_chunks)]
    results = [[[], []] for _ in range(n_chunks)]
    chains = []
    for ci in range(n_chunks):
        terms = _rwkv_chunk_terms(r_ref[0, rows[ci], :], lw_ref[0, rows[ci], :], k_ref[0, rows[ci], :],
                                  v_ref[0, rows[ci], :], kk_ref[0, rows[ci], :], a_ref[0, rows[ci], :])
        chains += [_rwkv_chain(results[ci][hh], terms[hh], masks) for hh in range(2)]
    _lockstep(chains)
    ys = [[None, None] for _ in range(n_chunks)]
    _lockstep([_rwkv_state_chain(hh, results, state_ref, ys) for hh in range(2)])

    for ci in range(n_chunks):
        y = jnp.concatenate(ys[ci], axis=1)
        r, k, v = r_ref[0, rows[ci], :], k_ref[0, rows[ci], :], v_ref[0, rows[ci], :]
        mean = _head_sum(y, left) * (1.0 / HEAD_DIM)
        var = _head_sum(jnp.square(y - mean), left) * (1.0 / HEAD_DIM)
        y = (y - mean) * lax.rsqrt(var + RWKV_GN_EPS) * lng_ref[...] + lnb_ref[...]
        bonus = _head_sum(r * k * rk_ref[...], left) * v
        o_ref[0, rows[ci], :] = ((y + bonus) * g_ref[0, rows[ci], :]).astype(o_ref.dtype)


def rwkv_scan(r, lw, k, v, kk, a, g, ln_g, ln_b, r_k, batch, seq):
    chunk = min(RWKV_CHUNK, seq)
    rows = _pick(seq, (1024, 512, 256, 128, 64, 32, 16, 8))
    hp = WIDTH // LANES
    seq_spec = pl.BlockSpec((1, rows, LANES), lambda b, h, c: (b, c, h))
    par_spec = pl.BlockSpec((1, LANES), lambda b, h, c: (0, h))
    shaped = [t.reshape(batch, seq, WIDTH) for t in (r, lw, k, v, kk, a, g)]
    pars = [t.reshape(1, WIDTH).astype(F32) for t in (ln_g, ln_b, r_k)]
    out = pl.pallas_call(
        functools.partial(_rwkv_scan_body, chunk=chunk),
        grid=(batch, hp, seq // rows),
        in_specs=[seq_spec] * 7 + [par_spec] * 3,
        out_specs=seq_spec,
        out_shape=jax.ShapeDtypeStruct((batch, seq, WIDTH), BF16),
        scratch_shapes=[pltpu.VMEM((2, HEAD_DIM, HEAD_DIM), F32)],
        compiler_params=_params("parallel", "parallel", "arbitrary"),
        name="rwkv_scan",
    )(*shaped, *pars)
    return out.reshape(batch * seq, WIDTH)


def _merge_body(h, a1, a2, a3, g1, g2, g3, w1, w2, w3, o_ref):
    hh = h[...]
    acc = jax.nn.sigmoid(_dot(hh, g1[...])) * _dot(a1[...], w1[...])
    acc += jax.nn.sigmoid(_dot(hh, g2[...])) * _dot(a2[...], w2[...])
    acc += jax.nn.sigmoid(_dot(hh, g3[...])) * _dot(a3[...], w3[...])
    o_ref[...] = acc.astype(o_ref.dtype)


def merge_branches(h, outs, w_gates, weights):
    m, d = h.shape
    tm = _pick(m, (512, 256, 128, 64, 32, 16, 8))
    tn = _pick(d, (512, 256, 128))
    nb = d // tn
    h_spec = pl.BlockSpec((tm, d), lambda i, j: (i, 0))
    a_spec = pl.BlockSpec((tm, WIDTH), lambda i, j: (i, 0))
    w_spec = pl.BlockSpec((WIDTH, tn), lambda i, j: (0, j))
    g_specs = [pl.BlockSpec((d, tn), lambda i, j, b=b: (0, j + b * nb)) for b in range(3)]
    return pl.pallas_call(
        _merge_body,
        grid=(m // tm, nb),
        in_specs=[h_spec] + [a_spec] * 3 + g_specs + [w_spec] * 3,
        out_specs=pl.BlockSpec((tm, tn), lambda i, j: (i, j)),
        out_shape=jax.ShapeDtypeStruct((m, d), BF16),
        compiler_params=_params("parallel", "parallel"),
        name="merge_branches",
    )(h, *outs, w_gates, w_gates, w_gates, *weights)


ROUTER_PAD = 128


def _slab_cols(ref, n, slabs):
    return jnp.concatenate([ref[pl.ds(s, n, stride=slabs), :] for s in range(slabs)], axis=1)


def _store_slabs(ref, value, slabs):
    n = value.shape[0]
    for s in range(slabs):
        ref[pl.ds(s, n, stride=slabs), :] = value[:, s * LANES:(s + 1) * LANES].astype(ref.dtype)


def _pack_bf16_pairs(h):
    half = h.shape[1] // 2
    bits = pltpu.bitcast(h.astype(BF16).astype(F32), jnp.uint32)
    return (bits[:, :half] >> 16) | bits[:, half:]


def _unpack_bf16_pairs(words):
    lo = pltpu.bitcast(words << 16, F32)
    hi = pltpu.bitcast(words & jnp.uint32(0xFFFF0000), F32)
    return jnp.concatenate([lo, hi], axis=1).astype(BF16)


def _router_body(x_ref, g_ref, w_ref, b_ref, h_ref, logit_ref, *, slabs):
    x = x_ref[...]
    y = x * lax.rsqrt(jnp.mean(x * x, axis=-1, keepdims=True) + NORM_EPS)
    h = y * g_ref[...]
    _store_slabs(h_ref, _pack_bf16_pairs(h), slabs)
    logit_ref[...] = jnp.dot(h, w_ref[...], preferred_element_type=F32, precision=HIGHEST) + b_ref[...]


def router(x, g, w_pad, b_pad):
    m, d = x.shape
    slabs = d // (2 * LANES)
    tm = _pick(m, (256, 128, 64, 32, 16, 8))
    return pl.pallas_call(
        functools.partial(_router_body, slabs=slabs),
        grid=(m // tm,),
        in_specs=[pl.BlockSpec((tm, d), lambda i: (i, 0)),
                  pl.BlockSpec((1, d), lambda i: (0, 0)),
                  pl.BlockSpec((d, ROUTER_PAD), lambda i: (0, 0)),
                  pl.BlockSpec((1, ROUTER_PAD), lambda i: (0, 0))],
        out_specs=[pl.BlockSpec((tm * slabs, LANES), lambda i: (i, 0)),
                   pl.BlockSpec((tm, ROUTER_PAD), lambda i: (i, 0))],
        out_shape=[jax.ShapeDtypeStruct((m * slabs, LANES), jnp.uint32),
                   jax.ShapeDtypeStruct((m, ROUTER_PAD), F32)],
        compiler_params=_params("parallel"),
        name="ffn_norm_router",
    )(x, g.reshape(1, d).astype(F32), w_pad, b_pad)


def _row_gather(idx_ref, first, n, src_ref, buf_ref, sem, slabs):
    def copy(r):
        src = pl.multiple_of(idx_ref[first + r] * slabs, slabs)
        return pltpu.make_async_copy(src_ref.at[pl.ds(src, slabs)],
                                     buf_ref.at[pl.ds(r * slabs, slabs)], sem)

    def start():
        lax.fori_loop(0, n, lambda r, c: (copy(r).start(), c)[1], 0, unroll=8)

    def wait():
        pltpu.make_async_copy(src_ref.at[pl.ds(0, n * slabs)], buf_ref, sem).wait()

    return start, wait


def _expert_body(be_ref, nb_ref, next_ref, tok_ref, h_ref, wg_hbm, wu_hbm, wd_hbm, o_ref,
                 xbuf, wg_f, wu_f, wd_f, wg_s, wu_s, wd_s, sem, wsem, *, layer, in_slabs, slabs):
    i = pl.program_id(0)
    n_blocks = nb_ref[0]
    slot = i % 2

    def gather(block, slot):
        return _row_gather(tok_ref, block * MOE_ROWS, MOE_ROWS, h_ref, xbuf.at[slot], sem.at[slot], in_slabs)

    def weight_copies(expert):
        pairs = ((wg_hbm, wg_f), (wu_hbm, wu_f), (wd_hbm, wd_f))
        return [pltpu.make_async_copy(src.at[layer, expert], dst, wsem.at[j])
                for j, (src, dst) in enumerate(pairs)]

    e = be_ref[i]

    @pl.when(i == 0)
    def _():
        gather(0, 0)[0]()
        for c in weight_copies(e):
            c.start()

    @pl.when(i + 1 < n_blocks)
    def _():
        gather(i + 1, 1 - slot)[0]()

    @pl.when((i < n_blocks) & ((i == 0) | (e != be_ref[jnp.maximum(i - 1, 0)])))
    def _():
        for c in weight_copies(e):
            c.wait()
        wg_s[...] = wg_f[...].astype(BF16)
        wu_s[...] = wu_f[...].astype(BF16)
        wd_s[...] = wd_f[...].astype(BF16)
        upcoming = next_ref[i]

        @pl.when(upcoming >= 0)
        def _():
            for c in weight_copies(upcoming):
                c.start()

    @pl.when(i < n_blocks)
    def _():
        gather(i, slot)[1]()
        x = _unpack_bf16_pairs(_slab_cols(xbuf.at[slot], MOE_ROWS, in_slabs))
        gate = _dot(x, wg_s[...])
        up = _dot(x, wu_s[...])
        mid = (jax.nn.silu(gate) * up).astype(BF16)
        _store_slabs(o_ref, _dot(mid, wd_s[...]), slabs)

    @pl.when(i >= n_blocks)
    def _():
        o_ref[...] = jnp.zeros_like(o_ref)


def expert_ffn(h_rows, row_tok, block_e, next_e, n_blocks, layer, w_gate, w_up, w_down):
    _, n_exp, d, ff = w_gate.shape
    slabs = d // LANES
    in_slabs = d // (2 * LANES)
    p = row_tok.shape[0]
    hbm = pl.BlockSpec(memory_space=pl.ANY)
    return pl.pallas_call(
        functools.partial(_expert_body, layer=layer, in_slabs=in_slabs, slabs=slabs),
        grid_spec=pltpu.PrefetchScalarGridSpec(
            num_scalar_prefetch=4,
            grid=(p // MOE_ROWS,),
            in_specs=[hbm, hbm, hbm, hbm],
            out_specs=pl.BlockSpec((MOE_ROWS * slabs, LANES), lambda i, be, nb, nxt, tok: (i, 0)),
            scratch_shapes=[pltpu.VMEM((2, MOE_ROWS * in_slabs, LANES), jnp.uint32),
                            pltpu.VMEM((d, ff), F32), pltpu.VMEM((d, ff), F32), pltpu.VMEM((ff, d), F32),
                            pltpu.VMEM((d, ff), BF16), pltpu.VMEM((d, ff), BF16),
                            pltpu.VMEM((ff, d), BF16),
                            pltpu.SemaphoreType.DMA((2,)), pltpu.SemaphoreType.DMA((3,))]),
        out_shape=jax.ShapeDtypeStruct((p * slabs, LANES), F32),
        compiler_params=_params("arbitrary"),
        name="expert_ffn",
    )(block_e, n_blocks, next_e, row_tok, h_rows, w_gate, w_up, w_down)


def _combine_body(dest_ref, x_ref, gate_ref, y_ref, o_ref, ybuf, sem, *, tm, slabs):
    i = pl.program_id(0)
    slot = i % 2

    def gather(step, slot):
        return _row_gather(dest_ref, step * tm * TOP_K, tm * TOP_K, y_ref, ybuf.at[slot], sem.at[slot], slabs)

    @pl.when(i == 0)
    def _():
        gather(0, 0)[0]()

    @pl.when(i + 1 < pl.num_programs(0))
    def _():
        gather(i + 1, 1 - slot)[0]()

    gather(i, slot)[1]()
    buf = ybuf.at[slot]
    gate = gate_ref[...]
    g0, g1 = gate[:, 0:1], gate[:, 1:2]
    for s in range(slabs):
        y0 = buf[pl.ds(s, tm, stride=slabs), :]
        y1 = buf[pl.ds(tm * slabs + s, tm, stride=slabs), :]
        cols = slice(s * LANES, (s + 1) * LANES)
        o_ref[:, cols] = x_ref[:, cols] + (g0 * y0 + g1 * y1)


def moe_combine(x, y_rows, dest, gate):
    m, d = x.shape
    slabs = d // LANES
    tm = _pick(m, (128, 64, 32, 16, 8))
    tile = lambda i, dest: (i, 0)
    return pl.pallas_call(
        functools.partial(_combine_body, tm=tm, slabs=slabs),
        grid_spec=pltpu.PrefetchScalarGridSpec(
            num_scalar_prefetch=1,
            grid=(m // tm,),
            in_specs=[pl.BlockSpec((tm, d), tile), pl.BlockSpec((tm, TOP_K), tile),
                      pl.BlockSpec(memory_space=pl.ANY)],
            out_specs=pl.BlockSpec((tm, d), tile),
            scratch_shapes=[pltpu.VMEM((2, tm * TOP_K * slabs, LANES), F32),
                            pltpu.SemaphoreType.DMA((2,))]),
        out_shape=jax.ShapeDtypeStruct((m, d), F32),
        compiler_params=_params("arbitrary"),
        name="moe_combine",
    )(dest.reshape(m // tm, tm, TOP_K).transpose(0, 2, 1).reshape(m * TOP_K), x, gate, y_rows)


def _routing_tables(logits):
    t = logits.shape[0]
    n_assign = t * TOP_K
    tok = jnp.arange(t)
    g_logits = logits[:, :N_GROUPS]
    grp = jnp.argmax(g_logits, axis=-1)
    p_grp = jax.nn.softmax(g_logits, axis=-1)[tok, grp]
    e_logits = logits[:, N_GROUPS:N_GROUPS + N_EXPERTS].reshape(t, N_GROUPS, EXPERTS_PER_GROUP)
    p_in, idx_in = lax.top_k(jax.nn.softmax(e_logits[tok, grp], axis=-1), TOP_K)
    gate = p_grp[:, None] * p_in / jnp.sum(p_in, axis=-1, keepdims=True)
    flat_e = (grp[:, None] * EXPERTS_PER_GROUP + idx_in).reshape(n_assign).astype(jnp.int32)
    onehot = (flat_e[:, None] == jnp.arange(N_EXPERTS, dtype=jnp.int32)[None, :]).astype(jnp.int32)
    rank = jnp.sum((jnp.cumsum(onehot, axis=0) - onehot) * onehot, axis=1)
    counts = jnp.sum(onehot, axis=0)
    padded = (counts + MOE_ROWS - 1) // MOE_ROWS * MOE_ROWS
    pend = jnp.cumsum(padded)
    dest = ((pend - padded)[flat_e] + rank).astype(jnp.int32)
    n_blocks_max = -(-n_assign // MOE_ROWS) + N_EXPERTS
    row_tok = jnp.zeros((n_blocks_max * MOE_ROWS,), jnp.int32).at[dest].set(
        jnp.repeat(tok, TOP_K).astype(jnp.int32))
    block_start = jnp.arange(n_blocks_max, dtype=pend.dtype) * MOE_ROWS
    block_e = jnp.minimum(jnp.sum(pend[None, :] <= block_start[:, None], axis=1),
                          N_EXPERTS - 1).astype(jnp.int32)
    n_blocks = (pend[-1:] // MOE_ROWS).astype(jnp.int32)
    after = jnp.sum(block_e[None, :] <= block_e[:, None], axis=1)
    next_e = jnp.where(after < n_blocks[0], block_e[jnp.minimum(after, n_blocks_max - 1)], -1)
    return gate.astype(F32), dest.reshape(t, TOP_K), row_tok, block_e, next_e.astype(jnp.int32), n_blocks


def hier_moe_residual(x, norm_g, rg_w, rg_b, re_w, re_b, layer, w_gate, w_up, w_down):
    t, d = x.shape
    w_pad = jnp.zeros((d, ROUTER_PAD), F32).at[:, :N_GROUPS].set(rg_w)
    w_pad = w_pad.at[:, N_GROUPS:N_GROUPS + N_EXPERTS].set(re_w)
    b_pad = jnp.zeros((1, ROUTER_PAD), F32).at[0, :N_GROUPS].set(rg_b)
    b_pad = b_pad.at[0, N_GROUPS:N_GROUPS + N_EXPERTS].set(re_b)
    h_rows, logits = router(x, norm_g, w_pad, b_pad)
    gate, dest, row_tok, block_e, next_e, n_blocks = _routing_tables(logits)
    y_rows = expert_ffn(h_rows, row_tok, block_e, next_e, n_blocks, layer, w_gate, w_up, w_down)
    return moe_combine(x, y_rows, dest, gate)


def _mla_q_weight(w_uq):
    w = w_uq.reshape(MLA_Q_LORA, HEADS, HEAD_DIM + MLA_ROPE_DIM)
    w = jnp.pad(w, ((0, 0), (0, 0), (0, LANES - HEAD_DIM - MLA_ROPE_DIM)))
    return w.reshape(MLA_Q_LORA, HEADS * LANES).astype(BF16)


def _mla_kv_weights(w_ukv):
    w = w_ukv.reshape(MLA_KV_LORA, HEADS, 2 * HEAD_DIM)
    wk = jnp.pad(w[:, :, :HEAD_DIM], ((0, 0), (0, 0), (0, LANES - HEAD_DIM)))
    wv = w[:, :, HEAD_DIM:]
    return (wk.reshape(MLA_KV_LORA, HEADS * LANES).astype(BF16),
            wv.reshape(MLA_KV_LORA, WIDTH).astype(BF16))


def _rope_slot_tables(positions):
    half = MLA_ROPE_DIM // 2
    inv_freq = ROPE_THETA ** (-jnp.arange(0, MLA_ROPE_DIM, 2, dtype=F32) / MLA_ROPE_DIM)
    ang = positions.astype(F32).reshape(-1, 1) * inv_freq[None, :]
    cos, sin = jnp.cos(ang), jnp.sin(ang)
    t = ang.shape[0]
    pad = jnp.zeros((t, LANES - HEAD_DIM - 2 * half), F32)
    cos_t = jnp.concatenate([jnp.ones((t, HEAD_DIM), F32), cos, cos, pad], axis=1)
    sin_t = jnp.concatenate([jnp.zeros((t, HEAD_DIM), F32), -sin, sin, pad], axis=1)
    return cos_t, sin_t


def _mixer_block(x, l, batch, seq, cos_t, sin_t, v_first, p):
    t, d = x.shape
    h = rmsnorm(x, p["attn_norm_g"][l], BF16, name="attn_norm")
    w_in = p["w_in"][l]
    o_rwkv = 3 * WIDTH
    n_rwkv = 3 * WIDTH + RWKV_DECAY_LORA + RWKV_ICLR_LORA + RWKV_GATE_LORA
    o_mla = o_rwkv + n_rwkv
    n_mla = MLA_Q_LORA + MLA_KV_LORA + MLA_ROPE_DIM
    o_gate = o_mla + n_mla

    qscale = jnp.concatenate([jnp.full((1, WIDTH), HEAD_DIM ** -0.5, F32), jnp.ones((1, 2 * WIDTH), F32)], axis=1)
    (qkv,) = matmul(h, w_in[:, :o_rwkv].astype(BF16), [BF16],
                    epilogue=lambda acc, s: (acc * s,), extras=[(qscale, "col", 0)], name="proj_sb")
    o_sb = sb_attention(qkv, batch, seq)

    wr = w_in[:, o_rwkv:o_mla]
    zpad = jnp.zeros((d, RWKV_LORA_PAD - RWKV_DECAY_LORA), F32)
    c0 = 3 * WIDTH
    w_rwkv = jnp.concatenate([wr[:, :c0], wr[:, c0:c0 + RWKV_DECAY_LORA], zpad,
                              wr[:, c0 + RWKV_DECAY_LORA:c0 + 2 * RWKV_DECAY_LORA], zpad,
                              wr[:, c0 + 2 * RWKV_DECAY_LORA:]], axis=1).astype(BF16)
    (cols,) = matmul(h, w_rwkv, [F32], name="proj_rwkv")
    mu = p["rwkv_mu"][l]
    mpad = jnp.zeros((RWKV_LORA_PAD - RWKV_DECAY_LORA,), F32)
    mu_pad = jnp.concatenate([mu[:c0], mu[c0:c0 + RWKV_DECAY_LORA], mpad,
                              mu[c0 + RWKV_DECAY_LORA:c0 + 2 * RWKV_DECAY_LORA], mpad,
                              mu[c0 + 2 * RWKV_DECAY_LORA:]])
    rpad = ((0, RWKV_LORA_PAD - RWKV_DECAY_LORA), (0, 0))
    vres = None
    if l > 0:
        vres = (p["rwkv_v0"][l - 1], p["rwkv_v1"][l - 1].astype(BF16), p["rwkv_v2"][l - 1].astype(BF16), v_first)
    r, lw, k, v, kk, a, g = rwkv_prep(
        cols, seq, mu_pad, p["rwkv_w0"][l], jnp.pad(p["rwkv_w2"][l], rpad).astype(BF16),
        p["rwkv_a0"][l], jnp.pad(p["rwkv_a2"][l], rpad).astype(BF16), p["rwkv_g2"][l].astype(BF16),
        p["rwkv_k_k"][l], p["rwkv_k_a"][l], vres)
    if l == 0:
        v_first = v
    o_rw = rwkv_scan(r, lw, k, v, kk, a, g, p["rwkv_ln_g"][l], p["rwkv_ln_b"][l], p["rwkv_r_k"][l],
                     batch, seq)

    wm = w_in[:, o_mla:o_gate]
    w_mla = jnp.concatenate([wm[:, :MLA_Q_LORA + MLA_KV_LORA], jnp.zeros((d, HEAD_DIM), F32),
                             wm[:, MLA_Q_LORA + MLA_KV_LORA:],
                             jnp.zeros((d, LANES - HEAD_DIM - MLA_ROPE_DIM), F32)], axis=1).astype(BF16)
    (mcols,) = matmul(h, w_mla, [F32], name="proj_mla")
    cq = rmsnorm(mcols, p["mla_q_norm_g"][l], BF16, width=MLA_Q_LORA, col_block=0, name="mla_q_norm")
    ckv = rmsnorm(mcols, p["mla_kv_norm_g"][l], BF16, width=MLA_KV_LORA,
                  col_block=MLA_Q_LORA // MLA_KV_LORA, name="mla_kv_norm")
    mla_scale = (HEAD_DIM + MLA_ROPE_DIM) ** -0.5
    (q_mla,) = matmul(cq, _mla_q_weight(p["mla_w_uq"][l]), [BF16],
                      epilogue=lambda acc, c, s: (_rope_slots(acc, c, s) * mla_scale,),
                      extras=[(cos_t, "row", 0), (sin_t, "row", 0)], name="mla_q_up")
    wk, wv = _mla_kv_weights(p["mla_w_ukv"][l])
    kpe_block = (MLA_Q_LORA + MLA_KV_LORA) // LANES

    def k_epilogue(acc, kpe, c, s):
        kr = _rope_slots(kpe, c, s)
        return (acc + jnp.concatenate([kr] * (acc.shape[1] // LANES), axis=1),)

    kpe = mcols[:, kpe_block * LANES:(kpe_block + 1) * LANES]
    (k_mla,) = matmul(ckv, wk, [BF16], epilogue=k_epilogue,
                      extras=[(kpe, "row", 0), (cos_t, "row", 0), (sin_t, "row", 0)], name="mla_k_up")
    (v_mla,) = matmul(ckv, wv, [BF16], name="mla_v_up")
    o_mla_out = mla_attention(q_mla, k_mla, v_mla, batch, seq)

    merged = merge_branches(h, [o_sb, o_rw, o_mla_out], w_in[:, o_gate:].astype(BF16),
                            [p["w_br_sb"][l].astype(BF16), p["w_br_rwkv"][l].astype(BF16),
                             p["w_br_mla"][l].astype(BF16)])
    (x,) = matmul(merged, p["w_out"][l].astype(BF16), [F32],
                  epilogue=lambda acc, res: (res + acc,), extras=[(x, "tile", 0)], name="out_proj")
    return x, v_first


def kernel(x, positions, attn_norm_g, w_in, rwkv_mu, rwkv_w0, rwkv_w2, rwkv_a0, rwkv_a2, rwkv_g2, rwkv_k_k, rwkv_k_a, rwkv_r_k, rwkv_ln_g, rwkv_ln_b, rwkv_v0, rwkv_v1, rwkv_v2, mla_q_norm_g, mla_w_uq, mla_kv_norm_g, mla_w_ukv, w_br_sb, w_br_rwkv, w_br_mla, w_out, ffn_norm_g, router_group_w, router_group_b, router_expert_w, router_expert_b, expert_w_gate, expert_w_up, expert_w_down, final_norm_g):
    p = dict(attn_norm_g=attn_norm_g, w_in=w_in, rwkv_mu=rwkv_mu, rwkv_w0=rwkv_w0, rwkv_w2=rwkv_w2,
             rwkv_a0=rwkv_a0, rwkv_a2=rwkv_a2, rwkv_g2=rwkv_g2, rwkv_k_k=rwkv_k_k, rwkv_k_a=rwkv_k_a,
             rwkv_r_k=rwkv_r_k, rwkv_ln_g=rwkv_ln_g, rwkv_ln_b=rwkv_ln_b, rwkv_v0=rwkv_v0,
             rwkv_v1=rwkv_v1, rwkv_v2=rwkv_v2, mla_q_norm_g=mla_q_norm_g, mla_w_uq=mla_w_uq,
             mla_kv_norm_g=mla_kv_norm_g, mla_w_ukv=mla_w_ukv, w_br_sb=w_br_sb, w_br_rwkv=w_br_rwkv,
             w_br_mla=w_br_mla, w_out=w_out)
    batch, seq, d = x.shape
    depth = w_in.shape[0]
    cos_t, sin_t = _rope_slot_tables(positions)
    xt = x.reshape(batch * seq, d)
    v_first = None
    for l in range(depth):
        xt, v_first = _mixer_block(xt, l, batch, seq, cos_t, sin_t, v_first, p)
        xt = hier_moe_residual(xt, ffn_norm_g[l], router_group_w[l], router_group_b[l],
                               router_expert_w[l], router_expert_b[l],
                               l, expert_w_gate, expert_w_up, expert_w_down)
    return rmsnorm(xt, final_norm_g, x.dtype, name="final_norm").reshape(batch, seq, d)
```

```python
import functools

import jax
import jax.numpy as jnp
from jax import lax
from jax.experimental import pallas as pl
from jax.experimental.pallas import tpu as pltpu

F32 = jnp.float32
BF16 = jnp.bfloat16
HIGHEST = lax.Precision.HIGHEST

NORM_EPS = 1e-6
HEADS = 16
HEAD_DIM = 64
WIDTH = HEADS * HEAD_DIM
LANES = 128
RWKV_DECAY_LORA = 96
RWKV_ICLR_LORA = 96
RWKV_GATE_LORA = 256
RWKV_LORA_PAD = 128
RWKV_GN_EPS = 64e-5
RWKV_CHUNK = 64
MLA_Q_LORA = 768
MLA_KV_LORA = 256
MLA_ROPE_DIM = 32
ROPE_THETA = 10000.0
N_GROUPS = 8
EXPERTS_PER_GROUP = 8
N_EXPERTS = N_GROUPS * EXPERTS_PER_GROUP
TOP_K = 2
MOE_ROWS = 128
VMEM_LIMIT = 48 * 1024 * 1024


def _params(*sem):
    return pltpu.CompilerParams(dimension_semantics=sem, vmem_limit_bytes=VMEM_LIMIT)


def _pick(n, cands):
    for c in cands:
        if n % c == 0:
            return c
    raise ValueError(f"no tile for {n} in {cands}")


def _dot(x, y):
    return jnp.dot(x, y, preferred_element_type=F32)


def _dot_nt(x, y):
    return lax.dot_general(x, y, (((1,), (1,)), ((), ())), preferred_element_type=F32)


def _dot_tn(x, y):
    return lax.dot_general(x, y, (((0,), (0,)), ((), ())), preferred_element_type=F32)


def _split(x):
    hi = x.astype(BF16)
    return hi, (x - hi.astype(F32)).astype(BF16)


def _rmsnorm_body(x_ref, g_ref, o_ref):
    x = x_ref[...].astype(F32)
    y = x * lax.rsqrt(jnp.mean(x * x, axis=-1, keepdims=True) + NORM_EPS)
    o_ref[...] = (y * g_ref[...]).astype(o_ref.dtype)


def rmsnorm(x, g, out_dtype, *, width=None, col_block=0, name="rmsnorm"):
    m = x.shape[0]
    width = x.shape[1] if width is None else width
    tm = _pick(m, (512, 256, 128, 64, 32, 16, 8))
    return pl.pallas_call(
        _rmsnorm_body,
        grid=(m // tm,),
        in_specs=[pl.BlockSpec((tm, width), lambda i: (i, col_block)),
                  pl.BlockSpec((1, width), lambda i: (0, 0))],
        out_specs=pl.BlockSpec((tm, width), lambda i: (i, 0)),
        out_shape=jax.ShapeDtypeStruct((m, width), out_dtype),
        compiler_params=_params("parallel"),
        name=name,
    )(x, g.reshape(1, width).astype(F32))


def _mm_body(*refs, n_extra, epilogue):
    a_ref, w_ref = refs[:2]
    extra = [r[...] for r in refs[2:2 + n_extra]]
    outs = refs[2 + n_extra:]
    acc = _dot(a_ref[...], w_ref[...])
    res = epilogue(acc, *extra)
    for o, r in zip(outs, res):
        o[...] = r.astype(o.dtype)


def matmul(a, w, out_dtypes, *, epilogue=None, extras=(), name="matmul"):
    m = a.shape[0]
    k, n = w.shape
    tm = _pick(m, (1024, 512, 256, 128, 64, 32, 16, 8))
    tn = n if n <= 1280 else _pick(n, (512, 256, 128))
    if epilogue is None:
        epilogue = lambda acc: (acc,)
    in_specs = [pl.BlockSpec((tm, k), lambda i, j: (i, 0)),
                pl.BlockSpec((k, tn), lambda i, j: (0, j))]
    args = [a, w]
    for arr, kind, off in extras:
        if kind == "row":
            in_specs.append(pl.BlockSpec((tm, arr.shape[1]), lambda i, j: (i, 0)))
        elif kind == "col":
            in_specs.append(pl.BlockSpec((1, tn), lambda i, j: (0, j)))
        else:
            in_specs.append(pl.BlockSpec((tm, tn), lambda i, j, off=off: (i, j + off)))
        args.append(arr)
    return pl.pallas_call(
        functools.partial(_mm_body, n_extra=len(extras), epilogue=epilogue),
        grid=(m // tm, n // tn),
        in_specs=in_specs,
        out_specs=[pl.BlockSpec((tm, tn), lambda i, j: (i, j)) for _ in out_dtypes],
        out_shape=[jax.ShapeDtypeStruct((m, n), dt) for dt in out_dtypes],
        compiler_params=_params("parallel", "parallel"),
        name=name,
    )(*args)


def _rope_slots(x, cos_t, sin_t):
    n = x.shape[-1]
    reps = n // LANES
    lane = lax.broadcasted_iota(jnp.int32, x.shape, 1) % LANES
    from_hi = pltpu.roll(x, n - MLA_ROPE_DIM // 2, 1)
    from_lo = pltpu.roll(x, MLA_ROPE_DIM // 2, 1)
    swapped = jnp.where(lane < HEAD_DIM + MLA_ROPE_DIM // 2, from_hi, from_lo)
    if reps > 1:
        cos_t = jnp.concatenate([cos_t] * reps, axis=1)
        sin_t = jnp.concatenate([sin_t] * reps, axis=1)
    return x * cos_t + swapped * sin_t


def _sb_body(q_ref, k_ref, v_ref, o_ref, *, t, cw):
    qi = pl.program_id(2)
    n_sub = t // cw
    first = lax.broadcasted_iota(jnp.int32, (cw, cw), 0)
    second = lax.broadcasted_iota(jnp.int32, (cw, cw), 1)
    before = second < first
    later = (first > second).astype(BF16)
    later2 = jnp.concatenate([later, later], axis=0)
    heads = [slice(hh * HEAD_DIM, (hh + 1) * HEAD_DIM) for hh in range(2)]

    def sweep(q, carry, ks, n_keys, diag_sub):
        z = [_dot_nt(q[hh], k_ref[0, pl.ds(ks, n_keys), h]) for hh, h in enumerate(heads)]
        acc = [carry[0], carry[2]]
        run = [carry[1], carry[3]]
        subs = n_keys // cw
        weights = [[None] * subs for _ in heads]
        for sub in reversed(range(subs)):
            for hh in range(2):
                zz = z[hh][:, sub * cw:(sub + 1) * cw]
                neg_abs = pltpu.bitcast(pltpu.bitcast(zz, jnp.uint32) | jnp.uint32(0x80000000), F32)
                sp = jnp.maximum(zz, 0.0) + jnp.log(1.0 + jnp.exp(neg_abs))
                mass = jnp.where(before, sp, 0.0) if sub == diag_sub else sp
                hi, lo = _split(mass)
                after = _dot(jnp.concatenate([hi, lo], axis=1), later2)
                wgt = jnp.exp(((zz - sp) - after) - run[hh])
                if sub == diag_sub:
                    wgt = jnp.where(before, wgt, 0.0)
                weights[hh][sub] = wgt.astype(BF16)
                run[hh] = run[hh] + (after[:, :1] + mass[:, :1])
        for hh, h in enumerate(heads):
            acc[hh] = acc[hh] + _dot(jnp.concatenate(weights[hh], axis=1), v_ref[0, pl.ds(ks, n_keys), h])
        return acc[0], run[0], acc[1], run[1]

    k0 = pl.multiple_of(qi * t, t)
    zeros = (jnp.zeros((cw, HEAD_DIM), F32), jnp.zeros((cw, 1), F32)) * 2
    groups = [sweep([q_ref[0, j * cw:(j + 1) * cw, h] for h in heads], zeros, k0, (j + 1) * cw, j)
              for j in range(n_sub)]
    carry = tuple(jnp.concatenate([g[c] for g in groups], axis=0) for c in range(4))
    q = [q_ref[0, :, h] for h in heads]
    carry = lax.fori_loop(
        0, qi, lambda it, c: sweep(q, c, pl.multiple_of((qi - 1 - it) * t, t), t, None), carry)
    o_ref[0] = jnp.concatenate([carry[0], carry[2]], axis=1).astype(o_ref.dtype)


def sb_attention(qkv, batch, seq):
    t = _pick(seq, (1024, 512, 256, 128))
    x = qkv.reshape(batch, seq, 3 * WIDTH)
    hp = WIDTH // LANES
    out = pl.pallas_call(
        functools.partial(_sb_body, t=t, cw=min(t, 256)),
        grid=(batch, hp, seq // t),
        in_specs=[pl.BlockSpec((1, t, LANES), lambda b, h, i: (b, i, h)),
                  pl.BlockSpec((1, seq, LANES), lambda b, h, i: (b, 0, hp + h)),
                  pl.BlockSpec((1, seq, LANES), lambda b, h, i: (b, 0, 2 * hp + h))],
        out_specs=pl.BlockSpec((1, t, LANES), lambda b, h, i: (b, i, h)),
        out_shape=jax.ShapeDtypeStruct((batch, seq, WIDTH), BF16),
        compiler_params=_params("parallel", "parallel", "parallel"),
        name="sb_attention",
    )(x, x, x)
    return out.reshape(batch * seq, WIDTH)


def _mla_body(q_ref, k_ref, v_ref, o_ref, *, t):
    qi = pl.program_id(2)
    rows = lax.broadcasted_iota(jnp.int32, (t, t), 0)
    cols = lax.broadcasted_iota(jnp.int32, (t, t), 1)
    q = [q_ref[0, :, hh * LANES:(hh + 1) * LANES] for hh in range(2)]

    def block(ks, carry, diagonal):
        s = [_dot_nt(q[hh], k_ref[0, pl.ds(ks, t), hh * LANES:(hh + 1) * LANES]) for hh in range(2)]
        out = []
        for hh in range(2):
            m, l, acc = carry[3 * hh:3 * hh + 3]
            ss = jnp.where(cols <= rows, s[hh], -1e30) if diagonal else s[hh]
            m_new = jnp.maximum(m, jnp.max(ss, axis=-1, keepdims=True))
            p = jnp.exp(ss - m_new)
            corr = jnp.exp(m - m_new)
            l = corr * l + jnp.sum(p, axis=-1, keepdims=True)
            v = v_ref[0, pl.ds(ks, t), hh * HEAD_DIM:(hh + 1) * HEAD_DIM]
            acc = corr * acc + _dot(p.astype(BF16), v)
            out += [m_new, l, acc]
        return tuple(out)

    init = (jnp.full((t, 1), -1e30, F32), jnp.zeros((t, 1), F32), jnp.zeros((t, HEAD_DIM), F32)) * 2
    carry = block(pl.multiple_of(qi * t, t), init, True)
    carry = lax.fori_loop(0, qi, lambda kb, c: block(pl.multiple_of(kb * t, t), c, False), carry)
    o_ref[0] = jnp.concatenate([carry[2] / carry[1], carry[5] / carry[4]], axis=1).astype(o_ref.dtype)


def mla_attention(q, k, v, batch, seq):
    t = _pick(seq, (512, 256, 128))
    hp = HEADS // 2
    out = pl.pallas_call(
        functools.partial(_mla_body, t=t),
        grid=(batch, hp, seq // t),
        in_specs=[pl.BlockSpec((1, t, 2 * LANES), lambda b, h, i: (b, i, h)),
                  pl.BlockSpec((1, seq, 2 * LANES), lambda b, h, i: (b, 0, h)),
                  pl.BlockSpec((1, seq, LANES), lambda b, h, i: (b, 0, h))],
        out_specs=pl.BlockSpec((1, t, LANES), lambda b, h, i: (b, i, h)),
        out_shape=jax.ShapeDtypeStruct((batch, seq, WIDTH), BF16),
        compiler_params=_params("parallel", "parallel", "parallel"),
        name="mla_attention",
    )(q.reshape(batch, seq, HEADS * LANES), k.reshape(batch, seq, HEADS * LANES),
      v.reshape(batch, seq, WIDTH))
    return out.reshape(batch * seq, WIDTH)


RWKV_IN = 3 * WIDTH + 2 * RWKV_LORA_PAD + RWKV_GATE_LORA


def _rwkv_prep_body(*refs, tiles_per_seq, has_vres):
    if has_vres:
        (cols_ref, prev_ref, mu_ref, w0_ref, w2_ref, a0_ref, a2_ref, g2_ref, kk_ref, ka_ref,
         v0_ref, v1_ref, v2_ref, vfirst_ref,
         r_out, lw_out, k_out, v_out, kk_out, a_out, g_out) = refs
    else:
        (cols_ref, prev_ref, mu_ref, w0_ref, w2_ref, a0_ref, a2_ref, g2_ref, kk_ref, ka_ref,
         r_out, lw_out, k_out, v_out, kk_out, a_out, g_out) = refs
    x = cols_ref[...]
    starts_seq = (pl.program_id(0) % tiles_per_seq) == 0
    last_prev = jnp.where(starts_seq, 0.0, prev_ref[7:8, :])
    first_row = lax.broadcasted_iota(jnp.int32, x.shape, 0) == 0
    prev = jnp.where(first_row, last_prev, pltpu.roll(x, 1, 0))
    x = x + (prev - x) * mu_ref[...]
    r = x[:, :WIDTH]
    k = x[:, WIDTH:2 * WIDTH]
    v = x[:, 2 * WIDTH:3 * WIDTH]
    o = 3 * WIDTH
    wd = x[:, o:o + RWKV_LORA_PAD]
    ad = x[:, o + RWKV_LORA_PAD:o + 2 * RWKV_LORA_PAD]
    gd = x[:, o + 2 * RWKV_LORA_PAD:]
    wpre = w0_ref[...] + _dot(jnp.tanh(wd).astype(BF16), w2_ref[...])
    w = -jax.nn.softplus(-wpre) - 0.5
    lw_out[...] = -jnp.exp(w)
    a = jax.nn.sigmoid(a0_ref[...] + _dot(ad.astype(BF16), a2_ref[...]))
    g_out[...] = _dot(jax.nn.sigmoid(gd).astype(BF16), g2_ref[...])
    if has_vres:
        low = _dot(v.astype(BF16), v1_ref[...])
        mix = jax.nn.sigmoid(v0_ref[...] + _dot(low.astype(BF16), v2_ref[...]))
        v = v + (vfirst_ref[...] - v) * mix
    r_out[...] = r
    v_out[...] = v
    kk_out[...] = k * kk_ref[...]
    k_out[...] = k * (1.0 + (a - 1.0) * ka_ref[...])
    a_out[...] = a


def rwkv_prep(cols, seq, mu, w0, w2, a0, a2, g2, k_k, k_a, vres):
    m = cols.shape[0]
    tm = _pick(seq, (256, 128, 64, 32, 16, 8))
    row = lambda t: t.reshape(1, -1).astype(F32)
    args = [cols, cols, row(mu), row(w0), w2, row(a0), a2, g2, row(k_k), row(k_a)]
    full = lambda arr: pl.BlockSpec(arr.shape, lambda i: (0, 0))
    tile = pl.BlockSpec((tm, WIDTH), lambda i: (i, 0))
    in_specs = [pl.BlockSpec((tm, RWKV_IN), lambda i: (i, 0)),
                pl.BlockSpec((8, RWKV_IN), lambda i: (jnp.maximum(i * (tm // 8) - 1, 0), 0))]
    in_specs += [full(t) for t in args[2:]]
    if vres is not None:
        v0, v1, v2, v_first = vres
        extra = [row(v0), v1, v2]
        args += extra + [v_first]
        in_specs += [full(t) for t in extra] + [tile]
    return pl.pallas_call(
        functools.partial(_rwkv_prep_body, tiles_per_seq=seq // tm, has_vres=vres is not None),
        grid=(m // tm,),
        in_specs=in_specs,
        out_specs=[tile] * 7,
        out_shape=[jax.ShapeDtypeStruct((m, WIDTH), F32)] * 7,
        compiler_params=_params("parallel"),
        name="rwkv_prep",
    )(*args)


def _halves(x):
    return x[:, :HEAD_DIM], x[:, HEAD_DIM:]


def _head_sum(x, left):
    s0 = jnp.sum(jnp.where(left, x, 0.0), axis=-1, keepdims=True)
    s1 = jnp.sum(jnp.where(left, 0.0, x), axis=-1, keepdims=True)
    return jnp.where(left, s0, s1)


def _dot3(x, y, dot):
    xh, xl = _split(x)
    yh, yl = _split(y)
    return dot(xh, yh) + (dot(xh, yl) + dot(xl, yh))


def _rwkv_chunk_terms(r, lw, k, v, kk_raw, a):
    c = r.shape[0]
    left = lax.broadcasted_iota(jnp.int32, (c, LANES), 1) < HEAD_DIM
    trow = lax.broadcasted_iota(jnp.int32, (c, LANES), 0)
    kk = kk_raw / jnp.maximum(jnp.sqrt(_head_sum(kk_raw * kk_raw, left)), 1e-12)
    beta = kk * a
    cl = lw
    shift = 1
    while shift < c:
        cl = cl + jnp.where(trow >= shift, pltpu.roll(cl, shift, 0), 0.0)
        shift *= 2
    a_t = -kk * jnp.exp(cl - lw)
    r_t = r * jnp.exp(cl)
    inv = jnp.exp(-cl)
    cl_end = cl[c - 1:c, :]
    tail = jnp.exp(cl_end - cl)
    full = (a_t, r_t, beta * inv, k * inv, v, beta * tail, k * tail, jnp.exp(cl_end))
    return [tuple(_halves(x)[hh] for x in full) for hh in range(2)]


def _rwkv_masks(c):
    ti = lax.broadcasted_iota(jnp.int32, (c, c), 0)
    tj = lax.broadcasted_iota(jnp.int32, (c, c), 1)
    merges = []
    half = 1
    while half < c:
        merges.append((ti // (2 * half) == tj // (2 * half)) & (ti % (2 * half) >= half)
                      & (tj % (2 * half) < half))
        half *= 2
    return ti > tj, ti >= tj, ti == tj, merges


def _rwkv_chain(out, head_terms, masks):
    at_h, rt_h, bt_h, kt_h, v_h, bbar_h, kbar_h, gend_h = head_terms
    strict, incl, eye, merges = masks
    c = at_h.shape[0]
    ar = jnp.concatenate([at_h, rt_h], axis=0).astype(BF16)
    bk = jnp.concatenate([bt_h, kt_h], axis=0).astype(BF16)
    pair = _dot_nt(ar, bk)
    yield
    lower = jnp.where(strict, pair[:c, :c], 0.0)
    akv = _dot(jnp.where(strict, pair[:c, c:], 0.0).astype(BF16), v_h.astype(BF16))
    tinv = eye.astype(F32) + jnp.where(merges[0], lower, 0.0)
    for off in merges[1:]:
        tb = tinv.astype(BF16)
        step = _dot(tb, jnp.where(off, lower, 0.0).astype(BF16))
        yield
        tinv = tinv + _dot(step.astype(BF16), tb)
        yield
    solved = _dot(tinv.astype(BF16), jnp.concatenate([at_h, akv], axis=1).astype(BF16))
    yield
    ta, w = _halves(solved)
    rb = jnp.where(incl, pair[c:, :c], 0.0).astype(BF16)
    rk = jnp.where(incl, pair[c:, c:], 0.0).astype(BF16)
    wv = jnp.concatenate([w, v_h], axis=0).astype(BF16)
    q = rt_h + _dot(rb, ta.astype(BF16))
    y0 = _dot(jnp.concatenate([rb, rk], axis=1), wv)
    g = jnp.where(eye, gend_h, 0.0) + _dot_tn(ta.astype(BF16), bbar_h.astype(BF16))
    d = _dot_tn(wv, jnp.concatenate([bbar_h, kbar_h], axis=0).astype(BF16))
    out.append((q, y0, g, d))


def _rwkv_state_chain(hh, results, state_ref, ys):
    state = state_ref[hh]
    for ci, res in enumerate(results):
        while not res[hh]:
            yield
        q, y0, g, d = res[hh][0]
        ys[ci][hh] = _dot3(q, state, _dot_nt) + y0
        state = _dot3(state, g, _dot) + d
        yield
    state_ref[hh] = state


def _lockstep(chains):
    live = list(chains)
    while live:
        still = []
        for ch in live:
            try:
                next(ch)
                still.append(ch)
            except StopIteration:
                pass
        live = still


def _rwkv_scan_body(r_ref, lw_ref, k_ref, v_ref, kk_ref, a_ref, g_ref, lng_ref, lnb_ref, rk_ref,
                    o_ref, state_ref, *, chunk):
    @pl.when(pl.program_id(2) == 0)
    def _():
        state_ref[...] = jnp.zeros_like(state_ref)

    n_chunks = r_ref.shape[1] // chunk
    left = lax.broadcasted_iota(jnp.int32, (chunk, LANES), 1) < HEAD_DIM
    masks = _rwkv_masks(chunk)
    rows = [slice(ci * chunk, (ci + 1) * chunk) for ci in range(n_chunks)]
    results = [[[], []] for _ in range(n_chunks)]
    chains = []
    for ci in range(n_chunks):
        terms = _rwkv_chunk_terms(r_ref[0, rows[ci], :], lw_ref[0, rows[ci], :], k_ref[0, rows[ci], :],
                                  v_ref[0, rows[ci], :], kk_ref[0, rows[ci], :], a_ref[0, rows[ci], :])
        chains += [_rwkv_chain(results[ci][hh], terms[hh], masks) for hh in range(2)]
    _lockstep(chains)
    ys = [[None, None] for _ in range(n_chunks)]
    _lockstep([_rwkv_state_chain(hh, results, state_ref, ys) for hh in range(2)])

    for ci in range(n_chunks):
        y = jnp.concatenate(ys[ci], axis=1)
        r, k, v = r_ref[0, rows[ci], :], k_ref[0, rows[ci], :], v_ref[0, rows[ci], :]
        mean = _head_sum(y, left) * (1.0 / HEAD_DIM)
        var = _head_sum(jnp.square(y - mean), left) * (1.0 / HEAD_DIM)
        y = (y - mean) * lax.rsqrt(var + RWKV_GN_EPS) * lng_ref[...] + lnb_ref[...]
        bonus = _head_sum(r * k * rk_ref[...], left) * v
        o_ref[0, rows[ci], :] = ((y + bonus) * g_ref[0, rows[ci], :]).astype(o_ref.dtype)


def rwkv_scan(r, lw, k, v, kk, a, g, ln_g, ln_b, r_k, batch, seq):
    chunk = min(RWKV_CHUNK, seq)
    rows = _pick(seq, (1024, 512, 256, 128, 64, 32, 16, 8))
    hp = WIDTH // LANES
    seq_spec = pl.BlockSpec((1, rows, LANES), lambda b, h, c: (b, c, h))
    par_spec = pl.BlockSpec((1, LANES), lambda b, h, c: (0, h))
    shaped = [t.reshape(batch, seq, WIDTH) for t in (r, lw, k, v, kk, a, g)]
    pars = [t.reshape(1, WIDTH).astype(F32) for t in (ln_g, ln_b, r_k)]
    out = pl.pallas_call(
        functools.partial(_rwkv_scan_body, chunk=chunk),
        grid=(batch, hp, seq // rows),
        in_specs=[seq_spec] * 7 + [par_spec] * 3,
        out_specs=seq_spec,
        out_shape=jax.ShapeDtypeStruct((batch, seq, WIDTH), BF16),
        scratch_shapes=[pltpu.VMEM((2, HEAD_DIM, HEAD_DIM), F32)],
        compiler_params=_params("parallel", "parallel", "arbitrary"),
        name="rwkv_scan",
    )(*shaped, *pars)
    return out.reshape(batch * seq, WIDTH)


def _merge_body(h, a1, a2, a3, g1, g2, g3, w1, w2, w3, o_ref):
    hh = h[...]
    acc = jax.nn.sigmoid(_dot(hh, g1[...])) * _dot(a1[...], w1[...])
    acc += jax.nn.sigmoid(_dot(hh, g2[...])) * _dot(a2[...], w2[...])
    acc += jax.nn.sigmoid(_dot(hh, g3[...])) * _dot(a3[...], w3[...])
    o_ref[...] = acc.astype(o_ref.dtype)


def merge_branches(h, outs, w_gates, weights):
    m, d = h.shape
    tm = _pick(m, (512, 256, 128, 64, 32, 16, 8))
    tn = _pick(d, (512, 256, 128))
    nb = d // tn
    h_spec = pl.BlockSpec((tm, d), lambda i, j: (i, 0))
    a_spec = pl.BlockSpec((tm, WIDTH), lambda i, j: (i, 0))
    w_spec = pl.BlockSpec((WIDTH, tn), lambda i, j: (0, j))
    g_specs = [pl.BlockSpec((d, tn), lambda i, j, b=b: (0, j + b * nb)) for b in range(3)]
    return pl.pallas_call(
        _merge_body,
        grid=(m // tm, nb),
        in_specs=[h_spec] + [a_spec] * 3 + g_specs + [w_spec] * 3,
        out_specs=pl.BlockSpec((tm, tn), lambda i, j: (i, j)),
        out_shape=jax.ShapeDtypeStruct((m, d), BF16),
        compiler_params=_params("parallel", "parallel"),
        name="merge_branches",
    )(h, *outs, w_gates, w_gates, w_gates, *weights)


ROUTER_PAD = 128


def _slab_cols(ref, n, slabs):
    return jnp.concatenate([ref[pl.ds(s, n, stride=slabs), :] for s in range(slabs)], axis=1)


def _store_slabs(ref, value, slabs):
    n = value.shape[0]
    for s in range(slabs):
        ref[pl.ds(s, n, stride=slabs), :] = value[:, s * LANES:(s + 1) * LANES].astype(ref.dtype)


def _pack_bf16_pairs(h):
    half = h.shape[1] // 2
    bits = pltpu.bitcast(h.astype(BF16).astype(F32), jnp.uint32)
    return (bits[:, :half] >> 16) | bits[:, half:]


def _unpack_bf16_pairs(words):
    lo = pltpu.bitcast(words << 16, F32)
    hi = pltpu.bitcast(words & jnp.uint32(0xFFFF0000), F32)
    return jnp.concatenate([lo, hi], axis=1).astype(BF16)


def _router_body(x_ref, g_ref, w_ref, b_ref, h_ref, logit_ref, *, slabs):
    x = x_ref[...]
    y = x * lax.rsqrt(jnp.mean(x * x, axis=-1, keepdims=True) + NORM_EPS)
    h = y * g_ref[...]
    _store_slabs(h_ref, _pack_bf16_pairs(h), slabs)
    logit_ref[...] = jnp.dot(h, w_ref[...], preferred_element_type=F32, precision=HIGHEST) + b_ref[...]


def router(x, g, w_pad, b_pad):
    m, d = x.shape
    slabs = d // (2 * LANES)
    tm = _pick(m, (256, 128, 64, 32, 16, 8))
    return pl.pallas_call(
        functools.partial(_router_body, slabs=slabs),
        grid=(m // tm,),
        in_specs=[pl.BlockSpec((tm, d), lambda i: (i, 0)),
                  pl.BlockSpec((1, d), lambda i: (0, 0)),
                  pl.BlockSpec((d, ROUTER_PAD), lambda i: (0, 0)),
                  pl.BlockSpec((1, ROUTER_PAD), lambda i: (0, 0))],
        out_specs=[pl.BlockSpec((tm * slabs, LANES), lambda i: (i, 0)),
                   pl.BlockSpec((tm, ROUTER_PAD), lambda i: (i, 0))],
        out_shape=[jax.ShapeDtypeStruct((m * slabs, LANES), jnp.uint32),
                   jax.ShapeDtypeStruct((m, ROUTER_PAD), F32)],
        compiler_params=_params("parallel"),
        name="ffn_norm_router",
    )(x, g.reshape(1, d).astype(F32), w_pad, b_pad)


def _row_gather(idx_ref, first, n, src_ref, buf_ref, sem, slabs):
    def copy(r):
        src = pl.multiple_of(idx_ref[first + r] * slabs, slabs)
        return pltpu.make_async_copy(src_ref.at[pl.ds(src, slabs)],
                                     buf_ref.at[pl.ds(r * slabs, slabs)], sem)

    def start():
        lax.fori_loop(0, n, lambda r, c: (copy(r).start(), c)[1], 0, unroll=8)

    def wait():
        pltpu.make_async_copy(src_ref.at[pl.ds(0, n * slabs)], buf_ref, sem).wait()

    return start, wait


def _expert_body(be_ref, nb_ref, next_ref, tok_ref, h_ref, wg_hbm, wu_hbm, wd_hbm, o_ref,
                 xbuf, wg_f, wu_f, wd_f, wg_s, wu_s, wd_s, sem, wsem, *, layer, in_slabs, slabs):
    i = pl.program_id(0)
    n_blocks = nb_ref[0]
    slot = i % 2

    def gather(block, slot):
        return _row_gather(tok_ref, block * MOE_ROWS, MOE_ROWS, h_ref, xbuf.at[slot], sem.at[slot], in_slabs)

    def weight_copies(expert):
        pairs = ((wg_hbm, wg_f), (wu_hbm, wu_f), (wd_hbm, wd_f))
        return [pltpu.make_async_copy(src.at[layer, expert], dst, wsem.at[j])
                for j, (src, dst) in enumerate(pairs)]

    e = be_ref[i]

    @pl.when(i == 0)
    def _():
        gather(0, 0)[0]()
        for c in weight_copies(e):
            c.start()

    @pl.when(i + 1 < n_blocks)
    def _():
        gather(i + 1, 1 - slot)[0]()

    @pl.when((i < n_blocks) & ((i == 0) | (e != be_ref[jnp.maximum(i - 1, 0)])))
    def _():
        for c in weight_copies(e):
            c.wait()
        wg_s[...] = wg_f[...].astype(BF16)
        wu_s[...] = wu_f[...].astype(BF16)
        wd_s[...] = wd_f[...].astype(BF16)
        upcoming = next_ref[i]

        @pl.when(upcoming >= 0)
        def _():
            for c in weight_copies(upcoming):
                c.start()

    @pl.when(i < n_blocks)
    def _():
        gather(i, slot)[1]()
        x = _unpack_bf16_pairs(_slab_cols(xbuf.at[slot], MOE_ROWS, in_slabs))
        gate = _dot(x, wg_s[...])
        up = _dot(x, wu_s[...])
        mid = (jax.nn.silu(gate) * up).astype(BF16)
        _store_slabs(o_ref, _dot(mid, wd_s[...]), slabs)

    @pl.when(i >= n_blocks)
    def _():
        o_ref[...] = jnp.zeros_like(o_ref)


def expert_ffn(h_rows, row_tok, block_e, next_e, n_blocks, layer, w_gate, w_up, w_down):
    _, n_exp, d, ff = w_gate.shape
    slabs = d // LANES
    in_slabs = d // (2 * LANES)
    p = row_tok.shape[0]
    hbm = pl.BlockSpec(memory_space=pl.ANY)
    return pl.pallas_call(
        functools.partial(_expert_body, layer=layer, in_slabs=in_slabs, slabs=slabs),
        grid_spec=pltpu.PrefetchScalarGridSpec(
            num_scalar_prefetch=4,
            grid=(p // MOE_ROWS,),
            in_specs=[hbm, hbm, hbm, hbm],
            out_specs=pl.BlockSpec((MOE_ROWS * slabs, LANES), lambda i, be, nb, nxt, tok: (i, 0)),
            scratch_shapes=[pltpu.VMEM((2, MOE_ROWS * in_slabs, LANES), jnp.uint32),
                            pltpu.VMEM((d, ff), F32), pltpu.VMEM((d, ff), F32), pltpu.VMEM((ff, d), F32),
                            pltpu.VMEM((d, ff), BF16), pltpu.VMEM((d, ff), BF16),
                            pltpu.VMEM((ff, d), BF16),
                            pltpu.SemaphoreType.DMA((2,)), pltpu.SemaphoreType.DMA((3,))]),
        out_shape=jax.ShapeDtypeStruct((p * slabs, LANES), F32),
        compiler_params=_params("arbitrary"),
        name="expert_ffn",
    )(block_e, n_blocks, next_e, row_tok, h_rows, w_gate, w_up, w_down)


def _combine_body(dest_ref, x_ref, gate_ref, g_ref, y_ref, *rest, tm, slabs, keep_x):
    if keep_x:
        x_out, h_out, ybuf, sem = rest
    else:
        h_out, ybuf, sem = rest
    i = pl.program_id(0)
    slot = i % 2

    def gather(step, slot):
        return _row_gather(dest_ref, step * tm * TOP_K, tm * TOP_K, y_ref, ybuf.at[slot], sem.at[slot], slabs)

    @pl.when(i == 0)
    def _():
        gather(0, 0)[0]()

    @pl.when(i + 1 < pl.num_programs(0))
    def _():
        gather(i + 1, 1 - slot)[0]()

    gather(i, slot)[1]()
    buf = ybuf.at[slot]
    gate = gate_ref[...]
    g0, g1 = gate[:, 0:1], gate[:, 1:2]
    pieces = []
    squares = jnp.zeros((tm, 1), F32)
    for s in range(slabs):
        y0 = buf[pl.ds(s, tm, stride=slabs), :]
        y1 = buf[pl.ds(tm * slabs + s, tm, stride=slabs), :]
        cols = slice(s * LANES, (s + 1) * LANES)
        piece = x_ref[:, cols] + (g0 * y0 + g1 * y1)
        if keep_x:
            x_out[:, cols] = piece
        squares = squares + jnp.sum(piece * piece, axis=-1, keepdims=True)
        pieces.append(piece)
    inv = lax.rsqrt(squares * (1.0 / (slabs * LANES)) + NORM_EPS)
    for s, piece in enumerate(pieces):
        cols = slice(s * LANES, (s + 1) * LANES)
        h_out[:, cols] = (piece * inv * g_ref[:, cols]).astype(h_out.dtype)


def moe_combine(x, y_rows, dest, gate, norm_g, norm_dtype, keep_x):
    m, d = x.shape
    slabs = d // LANES
    tm = _pick(m, (128, 64, 32, 16, 8))
    tile = lambda i, dest: (i, 0)
    row_tile = pl.BlockSpec((tm, d), tile)
    outs = ([jax.ShapeDtypeStruct((m, d), F32)] if keep_x else []) + [jax.ShapeDtypeStruct((m, d), norm_dtype)]
    return pl.pallas_call(
        functools.partial(_combine_body, tm=tm, slabs=slabs, keep_x=keep_x),
        grid_spec=pltpu.PrefetchScalarGridSpec(
            num_scalar_prefetch=1,
            grid=(m // tm,),
            in_specs=[row_tile, pl.BlockSpec((tm, TOP_K), tile),
                      pl.BlockSpec((1, d), lambda i, dest: (0, 0)),
                      pl.BlockSpec(memory_space=pl.ANY)],
            out_specs=[row_tile] * len(outs),
            scratch_shapes=[pltpu.VMEM((2, tm * TOP_K * slabs, LANES), F32),
                            pltpu.SemaphoreType.DMA((2,))]),
        out_shape=outs,
        compiler_params=_params("arbitrary"),
        name="moe_combine",
    )(dest.reshape(m // tm, tm, TOP_K).transpose(0, 2, 1).reshape(m * TOP_K), x, gate,
      norm_g.reshape(1, d).astype(F32), y_rows)


def _routing_tables(logits):
    t = logits.shape[0]
    n_assign = t * TOP_K
    tok = jnp.arange(t)
    g_logits = logits[:, :N_GROUPS]
    grp = jnp.argmax(g_logits, axis=-1)
    p_grp = jax.nn.softmax(g_logits, axis=-1)[tok, grp]
    e_logits = logits[:, N_GROUPS:N_GROUPS + N_EXPERTS].reshape(t, N_GROUPS, EXPERTS_PER_GROUP)
    p_in, idx_in = lax.top_k(jax.nn.softmax(e_logits[tok, grp], axis=-1), TOP_K)
    gate = p_grp[:, None] * p_in / jnp.sum(p_in, axis=-1, keepdims=True)
    flat_e = (grp[:, None] * EXPERTS_PER_GROUP + idx_in).reshape(n_assign).astype(jnp.int32)
    onehot = (flat_e[:, None] == jnp.arange(N_EXPERTS, dtype=jnp.int32)[None, :]).astype(jnp.int32)
    rank = jnp.sum((jnp.cumsum(onehot, axis=0) - onehot) * onehot, axis=1)
    counts = jnp.sum(onehot, axis=0)
    padded = (counts + MOE_ROWS - 1) // MOE_ROWS * MOE_ROWS
    pend = jnp.cumsum(padded)
    dest = ((pend - padded)[flat_e] + rank).astype(jnp.int32)
    n_blocks_max = -(-n_assign // MOE_ROWS) + N_EXPERTS
    row_tok = jnp.zeros((n_blocks_max * MOE_ROWS,), jnp.int32).at[dest].set(
        jnp.repeat(tok, TOP_K).astype(jnp.int32))
    block_start = jnp.arange(n_blocks_max, dtype=pend.dtype) * MOE_ROWS
    block_e = jnp.minimum(jnp.sum(pend[None, :] <= block_start[:, None], axis=1),
                          N_EXPERTS - 1).astype(jnp.int32)
    n_blocks = (pend[-1:] // MOE_ROWS).astype(jnp.int32)
    after = jnp.sum(block_e[None, :] <= block_e[:, None], axis=1)
    next_e = jnp.where(after < n_blocks[0], block_e[jnp.minimum(after, n_blocks_max - 1)], -1)
    return gate.astype(F32), dest.reshape(t, TOP_K), row_tok, block_e, next_e.astype(jnp.int32), n_blocks


def hier_moe_residual(x, norm_g, rg_w, rg_b, re_w, re_b, layer, w_gate, w_up, w_down, next_norm):
    t, d = x.shape
    w_pad = jnp.zeros((d, ROUTER_PAD), F32).at[:, :N_GROUPS].set(rg_w)
    w_pad = w_pad.at[:, N_GROUPS:N_GROUPS + N_EXPERTS].set(re_w)
    b_pad = jnp.zeros((1, ROUTER_PAD), F32).at[0, :N_GROUPS].set(rg_b)
    b_pad = b_pad.at[0, N_GROUPS:N_GROUPS + N_EXPERTS].set(re_b)
    h_rows, logits = router(x, norm_g, w_pad, b_pad)
    gate, dest, row_tok, block_e, next_e, n_blocks = _routing_tables(logits)
    y_rows = expert_ffn(h_rows, row_tok, block_e, next_e, n_blocks, layer, w_gate, w_up, w_down)
    return moe_combine(x, y_rows, dest, gate, *next_norm)


def _mla_q_weight(w_uq):
    w = w_uq.reshape(MLA_Q_LORA, HEADS, HEAD_DIM + MLA_ROPE_DIM)
    w = jnp.pad(w, ((0, 0), (0, 0), (0, LANES - HEAD_DIM - MLA_ROPE_DIM)))
    return w.reshape(MLA_Q_LORA, HEADS * LANES).astype(BF16)


def _mla_kv_weights(w_ukv):
    w = w_ukv.reshape(MLA_KV_LORA, HEADS, 2 * HEAD_DIM)
    wk = jnp.pad(w[:, :, :HEAD_DIM], ((0, 0), (0, 0), (0, LANES - HEAD_DIM)))
    wv = w[:, :, HEAD_DIM:]
    return (wk.reshape(MLA_KV_LORA, HEADS * LANES).astype(BF16),
            wv.reshape(MLA_KV_LORA, WIDTH).astype(BF16))


def _rope_slot_tables(positions):
    half = MLA_ROPE_DIM // 2
    inv_freq = ROPE_THETA ** (-jnp.arange(0, MLA_ROPE_DIM, 2, dtype=F32) / MLA_ROPE_DIM)
    ang = positions.astype(F32).reshape(-1, 1) * inv_freq[None, :]
    cos, sin = jnp.cos(ang), jnp.sin(ang)
    t = ang.shape[0]
    pad = jnp.zeros((t, LANES - HEAD_DIM - 2 * half), F32)
    cos_t = jnp.concatenate([jnp.ones((t, HEAD_DIM), F32), cos, cos, pad], axis=1)
    sin_t = jnp.concatenate([jnp.zeros((t, HEAD_DIM), F32), -sin, sin, pad], axis=1)
    return cos_t, sin_t


def _mixer_block(x, h, l, batch, seq, cos_t, sin_t, v_first, p):
    t, d = x.shape
    w_in = p["w_in"][l]
    o_rwkv = 3 * WIDTH
    n_rwkv = 3 * WIDTH + RWKV_DECAY_LORA + RWKV_ICLR_LORA + RWKV_GATE_LORA
    o_mla = o_rwkv + n_rwkv
    n_mla = MLA_Q_LORA + MLA_KV_LORA + MLA_ROPE_DIM
    o_gate = o_mla + n_mla

    qscale = jnp.concatenate([jnp.full((1, WIDTH), HEAD_DIM ** -0.5, F32), jnp.ones((1, 2 * WIDTH), F32)], axis=1)
    (qkv,) = matmul(h, w_in[:, :o_rwkv].astype(BF16), [BF16],
                    epilogue=lambda acc, s: (acc * s,), extras=[(qscale, "col", 0)], name="proj_sb")
    o_sb = sb_attention(qkv, batch, seq)

    wr = w_in[:, o_rwkv:o_mla]
    zpad = jnp.zeros((d, RWKV_LORA_PAD - RWKV_DECAY_LORA), F32)
    c0 = 3 * WIDTH
    w_rwkv = jnp.concatenate([wr[:, :c0], wr[:, c0:c0 + RWKV_DECAY_LORA], zpad,
                              wr[:, c0 + RWKV_DECAY_LORA:c0 + 2 * RWKV_DECAY_LORA], zpad,
                              wr[:, c0 + 2 * RWKV_DECAY_LORA:]], axis=1).astype(BF16)
    (cols,) = matmul(h, w_rwkv, [F32], name="proj_rwkv")
    mu = p["rwkv_mu"][l]
    mpad = jnp.zeros((RWKV_LORA_PAD - RWKV_DECAY_LORA,), F32)
    mu_pad = jnp.concatenate([mu[:c0], mu[c0:c0 + RWKV_DECAY_LORA], mpad,
                              mu[c0 + RWKV_DECAY_LORA:c0 + 2 * RWKV_DECAY_LORA], mpad,
                              mu[c0 + 2 * RWKV_DECAY_LORA:]])
    rpad = ((0, RWKV_LORA_PAD - RWKV_DECAY_LORA), (0, 0))
    vres = None
    if l > 0:
        vres = (p["rwkv_v0"][l - 1], p["rwkv_v1"][l - 1].astype(BF16), p["rwkv_v2"][l - 1].astype(BF16), v_first)
    r, lw, k, v, kk, a, g = rwkv_prep(
        cols, seq, mu_pad, p["rwkv_w0"][l], jnp.pad(p["rwkv_w2"][l], rpad).astype(BF16),
        p["rwkv_a0"][l], jnp.pad(p["rwkv_a2"][l], rpad).astype(BF16), p["rwkv_g2"][l].astype(BF16),
        p["rwkv_k_k"][l], p["rwkv_k_a"][l], vres)
    if l == 0:
        v_first = v
    o_rw = rwkv_scan(r, lw, k, v, kk, a, g, p["rwkv_ln_g"][l], p["rwkv_ln_b"][l], p["rwkv_r_k"][l],
                     batch, seq)

    wm = w_in[:, o_mla:o_gate]
    w_mla = jnp.concatenate([wm[:, :MLA_Q_LORA + MLA_KV_LORA], jnp.zeros((d, HEAD_DIM), F32),
                             wm[:, MLA_Q_LORA + MLA_KV_LORA:],
                             jnp.zeros((d, LANES - HEAD_DIM - MLA_ROPE_DIM), F32)], axis=1).astype(BF16)
    (mcols,) = matmul(h, w_mla, [F32], name="proj_mla")
    cq = rmsnorm(mcols, p["mla_q_norm_g"][l], BF16, width=MLA_Q_LORA, col_block=0, name="mla_q_norm")
    ckv = rmsnorm(mcols, p["mla_kv_norm_g"][l], BF16, width=MLA_KV_LORA,
                  col_block=MLA_Q_LORA // MLA_KV_LORA, name="mla_kv_norm")
    mla_scale = (HEAD_DIM + MLA_ROPE_DIM) ** -0.5
    (q_mla,) = matmul(cq, _mla_q_weight(p["mla_w_uq"][l]), [BF16],
                      epilogue=lambda acc, c, s: (_rope_slots(acc, c, s) * mla_scale,),
                      extras=[(cos_t, "row", 0), (sin_t, "row", 0)], name="mla_q_up")
    wk, wv = _mla_kv_weights(p["mla_w_ukv"][l])
    kpe_block = (MLA_Q_LORA + MLA_KV_LORA) // LANES

    def k_epilogue(acc, kpe, c, s):
        kr = _rope_slots(kpe, c, s)
        return (acc + jnp.concatenate([kr] * (acc.shape[1] // LANES), axis=1),)

    kpe = mcols[:, kpe_block * LANES:(kpe_block + 1) * LANES]
    (k_mla,) = matmul(ckv, wk, [BF16], epilogue=k_epilogue,
                      extras=[(kpe, "row", 0), (cos_t, "row", 0), (sin_t, "row", 0)], name="mla_k_up")
    (v_mla,) = matmul(ckv, wv, [BF16], name="mla_v_up")
    o_mla_out = mla_attention(q_mla, k_mla, v_mla, batch, seq)

    merged = merge_branches(h, [o_sb, o_rw, o_mla_out], w_in[:, o_gate:].astype(BF16),
                            [p["w_br_sb"][l].astype(BF16), p["w_br_rwkv"][l].astype(BF16),
                             p["w_br_mla"][l].astype(BF16)])
    (x,) = matmul(merged, p["w_out"][l].astype(BF16), [F32],
                  epilogue=lambda acc, res: (res + acc,), extras=[(x, "tile", 0)], name="out_proj")
    return x, v_first


def kernel(x, positions, attn_norm_g, w_in, rwkv_mu, rwkv_w0, rwkv_w2, rwkv_a0, rwkv_a2, rwkv_g2, rwkv_k_k, rwkv_k_a, rwkv_r_k, rwkv_ln_g, rwkv_ln_b, rwkv_v0, rwkv_v1, rwkv_v2, mla_q_norm_g, mla_w_uq, mla_kv_norm_g, mla_w_ukv, w_br_sb, w_br_rwkv, w_br_mla, w_out, ffn_norm_g, router_group_w, router_group_b, router_expert_w, router_expert_b, expert_w_gate, expert_w_up, expert_w_down, final_norm_g):
    p = dict(attn_norm_g=attn_norm_g, w_in=w_in, rwkv_mu=rwkv_mu, rwkv_w0=rwkv_w0, rwkv_w2=rwkv_w2,
             rwkv_a0=rwkv_a0, rwkv_a2=rwkv_a2, rwkv_g2=rwkv_g2, rwkv_k_k=rwkv_k_k, rwkv_k_a=rwkv_k_a,
             rwkv_r_k=rwkv_r_k, rwkv_ln_g=rwkv_ln_g, rwkv_ln_b=rwkv_ln_b, rwkv_v0=rwkv_v0,
             rwkv_v1=rwkv_v1, rwkv_v2=rwkv_v2, mla_q_norm_g=mla_q_norm_g, mla_w_uq=mla_w_uq,
             mla_kv_norm_g=mla_kv_norm_g, mla_w_ukv=mla_w_ukv, w_br_sb=w_br_sb, w_br_rwkv=w_br_rwkv,
             w_br_mla=w_br_mla, w_out=w_out)
    batch, seq, d = x.shape
    depth = w_in.shape[0]
    cos_t, sin_t = _rope_slot_tables(positions)
    xt = x.reshape(batch * seq, d)
    v_first = None
    h = rmsnorm(xt, attn_norm_g[0], BF16, name="attn_norm")
    for l in range(depth):
        xt, v_first = _mixer_block(xt, h, l, batch, seq, cos_t, sin_t, v_first, p)
        last = l == depth - 1
        next_norm = (final_norm_g, x.dtype, False) if last else (attn_norm_g[l + 1], BF16, True)
        out = hier_moe_residual(xt, ffn_norm_g[l], router_group_w[l], router_group_b[l],
                                router_expert_w[l], router_expert_b[l],
                                l, expert_w_gate, expert_w_up, expert_w_down, next_norm)
        if last:
            return out[0].reshape(batch, seq, d)
        xt, h = out
```
